```python
import jax
import jax.numpy as jnp
from jax import lax
import numpy as np

D_MODEL = 1024
BATCH = 2
SEQ = 8192
DEPTH = 4
DEC_BATCH = 128
DEC_SEQ = 1
PAST_LEN = 8192
PAGE_SIZE = 128

N_META = 16
D_FF = 2048
EPS = 1e-6
LRU_WIDTH = D_MODEL // 2
LRU_BLOCKS = 8
LRU_BW = LRU_WIDTH // LRU_BLOCKS
CONV_WIDTH = 4
LRU_C = 8.0
SWA_HEAD_DIM = 64
SWA_HEADS = D_MODEL // (2 * SWA_HEAD_DIM)
SWA_KV_HEADS = SWA_HEADS // 4
SWA_GROUP = SWA_HEADS // SWA_KV_HEADS
WINDOW = 128
RET_DK = 64
RET_DV = 128
RET_HEADS = D_MODEL // (2 * RET_DV)
RET_CHUNK = 128
ROPE_BASE = 10000.0
GN_EPS = 1e-5
N_BRANCH = 3
IN_SIZES = (LRU_WIDTH, LRU_WIDTH,
            SWA_HEADS * SWA_HEAD_DIM, SWA_KV_HEADS * SWA_HEAD_DIM, SWA_KV_HEADS * SWA_HEAD_DIM,
            RET_HEADS * RET_DK, RET_HEADS * RET_DK, RET_HEADS * RET_DV, RET_HEADS * RET_DV,
            N_BRANCH * D_MODEL)
IN_DIM = sum(IN_SIZES)
IN_SPLITS = tuple(int(c) for c in np.cumsum(IN_SIZES)[:-1])

kernel_name = 'hybrid_lru_swa_retention_step'


def rms_norm(x, g):
    x32 = x.astype(jnp.float32)
    y = x32 * lax.rsqrt(jnp.mean(x32 * x32, axis=-1, keepdims=True) + EPS)
    return (y * g.astype(jnp.float32)).astype(x.dtype)


def swiglu(x, w_gu, w_down):
    gate, up = jnp.split(x @ w_gu, 2, axis=-1)
    return (jax.nn.silu(gate) * up) @ w_down


def rope(x, pos):
    half = x.shape[-1] // 2
    inv = ROPE_BASE ** (-jnp.arange(half, dtype=jnp.float32) / half)
    ang = pos.astype(jnp.float32)[:, None] * inv[None, :]
    cos = jnp.cos(ang)[:, None, :]
    sin = jnp.sin(ang)[:, None, :]
    x1, x2 = x[..., :half], x[..., half:]
    return jnp.concatenate([x1 * cos - x2 * sin, x1 * sin + x2 * cos], axis=-1)


def causal_conv(x_ext, w, b):
    t = x_ext.shape[1] - (CONV_WIDTH - 1)
    out = b[None, None, :] + x_ext[:, 0:t] * w[0]
    for j in range(1, CONV_WIDTH):
        out = out + x_ext[:, j:j + t] * w[j]
    return out


def linear_scan(log_a, b, h0):
    def combine(e1, e2):
        la1, b1 = e1
        la2, b2 = e2
        return la1 + la2, jnp.exp(la2) * b1 + b2
    cum_la, h = lax.associative_scan(combine, (log_a, b), axis=1)
    return h + jnp.exp(cum_la) * h0[:, None, :]


def rg_lru(xc, h0, w_a, b_a, w_x, b_x, lam):
    bsz, t, _ = xc.shape
    xb = xc.reshape(bsz, t, LRU_BLOCKS, LRU_BW)
    r = jax.nn.sigmoid(jnp.einsum('btnc,ncd->btnd', xb, w_a).reshape(bsz, t, LRU_WIDTH) + b_a)
    i = jax.nn.sigmoid(jnp.einsum('btnc,ncd->btnd', xb, w_x).reshape(bsz, t, LRU_WIDTH) + b_x)
    log_a = -LRU_C * r * jax.nn.softplus(-lam)
    b = jnp.sqrt(-jnp.expm1(2.0 * log_a)) * (i * xc)
    return linear_scan(log_a, b, h0)


def sink_probs(s, mask, sinks):
    sk = sinks.astype(jnp.float32).reshape(SWA_KV_HEADS, SWA_GROUP, 1, 1)
    s = jnp.where(mask, s, -jnp.inf)
    m = jnp.maximum(jnp.max(s, axis=-1, keepdims=True), sk)
    e = jnp.exp(s - m)
    return e / (jnp.sum(e, axis=-1, keepdims=True) + jnp.exp(sk - m))


def swa_prompt(q, k, v, sinks):
    bsz, t = q.shape[:2]
    pad = (-t) % WINDOW
    nb = (t + pad) // WINDOW

    def blocks(a):
        a = jnp.pad(a, ((0, 0), (pad, 0), (0, 0), (0, 0)))
        return a.reshape((bsz, nb, WINDOW) + a.shape[2:])

    def with_prev(a):
        prev = jnp.pad(a[:, :-1], ((0, 0), (1, 0), (0, 0), (0, 0), (0, 0)))
        return jnp.concatenate([prev, a], axis=2)

    qb = blocks(q).reshape(bsz, nb, WINDOW, SWA_KV_HEADS, SWA_GROUP, SWA_HEAD_DIM)
    kb = with_prev(blocks(k))
    vb = with_prev(blocks(v))
    s = jnp.einsum('bnqhgd,bnkhd->bnhgqk', qb, kb) * SWA_HEAD_DIM ** -0.5
    qi = jnp.arange(WINDOW)[:, None] + WINDOW
    kj = jnp.arange(2 * WINDOW)[None, :]
    in_win = (qi - kj >= 0) & (qi - kj < WINDOW)
    key_idx = jnp.arange(nb)[:, None] * WINDOW - WINDOW + kj
    mask = in_win[None] & (key_idx >= pad)[:, None, :]
    p = sink_probs(s, mask[None, :, None, None], sinks)
    o = jnp.einsum('bnhgqk,bnkhd->bnqhgd', p, vb)
    return o.reshape(bsz, nb * WINDOW, SWA_HEADS * SWA_HEAD_DIM)[:, pad:]


def swa_sample(q, k, v, ck, cv, sinks):
    bsz, t = q.shape[:2]
    buf = ck.shape[1]
    kk = jnp.concatenate([ck, k], axis=1)
    vv = jnp.concatenate([cv, v], axis=1)
    qpos = PAST_LEN + jnp.arange(t)
    kpos = PAST_LEN - buf + jnp.arange(buf + t)
    diff = qpos[:, None] - kpos[None, :]
    mask = (diff >= 0) & (diff < WINDOW)
    qg = q.reshape(bsz, t, SWA_KV_HEADS, SWA_GROUP, SWA_HEAD_DIM)
    s = jnp.einsum('bqhgd,bkhd->bhgqk', qg, kk) * SWA_HEAD_DIM ** -0.5
    p = sink_probs(s, mask, sinks)
    o = jnp.einsum('bhgqk,bkhd->bqhgd', p, vv).reshape(bsz, t, SWA_HEADS * SWA_HEAD_DIM)
    return o, kk[:, -buf:], vv[:, -buf:]


def retention_log_gamma():
    return jnp.log1p(-jnp.exp2(-5.0 - jnp.arange(RET_HEADS, dtype=jnp.float32)))


def retention_chunk(state, q, k, v, log_g):
    c = q.shape[1]
    n = jnp.arange(c, dtype=jnp.float32)
    diff = n[:, None] - n[None, :]
    expo = jnp.where(diff[None] >= 0, diff[None] * log_g[:, None, None], -jnp.inf)
    scores = jnp.einsum('bihd,bjhd->bhij', q, k) * jnp.exp(expo)
    o = jnp.einsum('bhij,bjhe->bihe', scores, v)
    cross_decay = jnp.exp((n[:, None] + 1.0) * log_g[None, :])
    o = o + jnp.einsum('bihd,bhde->bihe', q, state) * cross_decay[None, :, :, None]
    k_decay = jnp.exp((c - 1.0 - n)[:, None] * log_g[None, :])
    new_state = jnp.exp(c * log_g)[None, :, None, None] * state + jnp.einsum('bjhd,bjhe,jh->bhde', k, v, k_decay)
    return new_state, o


def retention_prompt(q, k, v, log_g):
    bsz, t = q.shape[:2]
    pad = (-t) % RET_CHUNK
    nc = (t + pad) // RET_CHUNK

    def chunks(a):
        a = jnp.pad(a, ((0, 0), (pad, 0), (0, 0), (0, 0)))
        return jnp.moveaxis(a.reshape((bsz, nc, RET_CHUNK) + a.shape[2:]), 1, 0)

    def step(state, xs):
        return retention_chunk(state, xs[0], xs[1], xs[2], log_g)

    s0 = jnp.zeros((bsz, RET_HEADS, RET_DK, RET_DV), jnp.float32)
    state, o = lax.scan(step, s0, (chunks(q), chunks(k), chunks(v)))
    o = jnp.moveaxis(o, 0, 1).reshape(bsz, nc * RET_CHUNK, RET_HEADS, RET_DV)[:, pad:]
    return o, state


def group_norm(o, gain):
    bsz, t = o.shape[:2]
    mu = jnp.mean(o, axis=-1, keepdims=True)
    var = jnp.mean(jnp.square(o - mu), axis=-1, keepdims=True)
    y = (o - mu) * lax.rsqrt(var + GN_EPS)
    return y.reshape(bsz, t, RET_HEADS * RET_DV) * gain


def token_mixer(u, pos, lp, state, buf):
    f32 = jnp.float32
    bsz, t, _ = u.shape
    (xa, ya, qs, ks, vs, qr, kr, vr, gr, gates) = jnp.split((u @ lp['w_in']).astype(f32), IN_SPLITS, axis=-1)
    if state is None:
        conv_hist = jnp.zeros((bsz, CONV_WIDTH - 1, LRU_WIDTH), f32)
        h0 = jnp.zeros((bsz, LRU_WIDTH), f32)
        ck = cv = s_ret = None
    else:
        ck, cv, conv_hist, h0, s_ret = [a.astype(f32) for a in state]
    x_ext = jnp.concatenate([conv_hist, xa], axis=1)
    xc = causal_conv(x_ext, lp['conv_w'].astype(f32), lp['conv_b'].astype(f32))
    h = rg_lru(xc, h0, lp['lru_w_a'].astype(f32), lp['lru_b_a'].astype(f32),
               lp['lru_w_x'].astype(f32), lp['lru_b_x'].astype(f32), lp['lru_lambda'].astype(f32))
    o_a = h * jax.nn.gelu(ya)
    new_conv = x_ext[:, -(CONV_WIDTH - 1):]
    new_h = h[:, -1]
    q = qs.reshape(bsz, t, SWA_HEADS, SWA_HEAD_DIM)
    k = ks.reshape(bsz, t, SWA_KV_HEADS, SWA_HEAD_DIM)
    v = vs.reshape(bsz, t, SWA_KV_HEADS, SWA_HEAD_DIM)
    if state is None:
        o_b = swa_prompt(q, k, v, lp['swa_sinks'])
        new_k, new_v = k[:, -buf:], v[:, -buf:]
    else:
        o_b, new_k, new_v = swa_sample(q, k, v, ck, cv, lp['swa_sinks'])
    log_g = retention_log_gamma()
    qc = rope(qr.reshape(bsz, t, RET_HEADS, RET_DK), pos)
    kc = rope(kr.reshape(bsz, t, RET_HEADS, RET_DK), pos) * RET_DK ** -0.5
    vc = vr.reshape(bsz, t, RET_HEADS, RET_DV)
    if state is None:
        o_r, new_s = retention_prompt(qc, kc, vc, log_g)
    else:
        new_s, o_r = retention_chunk(s_ret, qc, kc, vc, log_g)
    o_c = group_norm(o_r, lp['ret_norm'].astype(f32)) * jax.nn.silu(gr)
    g_a, g_b, g_c = jnp.split(jax.nn.sigmoid(gates), N_BRANCH, axis=-1)
    merged = (g_a * (o_a @ lp['w_branch_a'].astype(f32))
              + g_b * (o_b @ lp['w_branch_b'].astype(f32))
              + g_c * (o_c @ lp['w_branch_c'].astype(f32)))
    out = (merged @ lp['w_out'].astype(f32)).astype(u.dtype)
    new_state = [a.astype(u.dtype) for a in (new_k, new_v, new_conv, new_h, new_s)]
    return out, new_state


def trunk(x, pos, cache, params, final_norm, buf):
    new = ([], [], [], [], [])
    for l in range(DEPTH):
        lp = {name: w[l] for name, w in params.items()}
        st = None if cache is None else [c[l] for c in cache]
        x = x + 0.5 * swiglu(rms_norm(x, lp['ffn1_norm']), lp['ffn1_w_gu'], lp['ffn1_w_down'])
        mo, ns = token_mixer(rms_norm(x, lp['mix_norm']), pos, lp, st, buf)
        x = x + mo
        x = x + 0.5 * swiglu(rms_norm(x, lp['ffn2_norm']), lp['ffn2_w_gu'], lp['ffn2_w_down'])
        for acc, s in zip(new, ns):
            acc.append(s)
    return rms_norm(x, final_norm), [jnp.stack(acc) for acc in new]


def setup_inputs(seed: int = 0) -> dict:
    key = jax.random.key(seed)
    ks = jax.random.split(key, 40)
    f32 = jnp.float32

    def nrm(k, shape, scale):
        return scale * jax.random.normal(k, shape, f32)

    def gain(k, shape):
        return 1.0 + 0.02 * jax.random.normal(k, shape, f32)

    buf = min(WINDOW, PAST_LEN)
    s = jax.random.uniform(ks[20], (DEPTH, LRU_WIDTH), f32, 0.9, 0.999) ** (1.0 / LRU_C)
    lam = jnp.log(s) - jnp.log1p(-s)
    hv = SWA_HEADS * SWA_HEAD_DIM
    rv = RET_HEADS * RET_DV
    return {
        'x_prompt': nrm(ks[0], (BATCH, SEQ, D_MODEL), 1.0),
        'x_sample': nrm(ks[1], (DEC_BATCH, DEC_SEQ, D_MODEL), 1.0),
        'cache_swa_k': nrm(ks[2], (DEPTH, DEC_BATCH, buf, SWA_KV_HEADS, SWA_HEAD_DIM), 1.0),
        'cache_swa_v': nrm(ks[3], (DEPTH, DEC_BATCH, buf, SWA_KV_HEADS, SWA_HEAD_DIM), 1.0),
        'state_conv': nrm(ks[4], (DEPTH, DEC_BATCH, CONV_WIDTH - 1, LRU_WIDTH), 1.0),
        'state_lru': nrm(ks[5], (DEPTH, DEC_BATCH, LRU_WIDTH), 0.5),
        'state_ret': nrm(ks[6], (DEPTH, DEC_BATCH, RET_HEADS, RET_DK, RET_DV), 1.0),
        'meta_tokens': nrm(ks[7], (N_META, D_MODEL), 1.0),
        'ffn1_norm': gain(ks[8], (DEPTH, D_MODEL)),
        'ffn1_w_gu': nrm(ks[9], (DEPTH, D_MODEL, 2 * D_FF), D_MODEL ** -0.5),
        'ffn1_w_down': nrm(ks[10], (DEPTH, D_FF, D_MODEL), D_FF ** -0.5),
        'mix_norm': gain(ks[11], (DEPTH, D_MODEL)),
        'w_in': nrm(ks[12], (DEPTH, D_MODEL, IN_DIM), D_MODEL ** -0.5),
        'conv_w': nrm(ks[13], (DEPTH, CONV_WIDTH, LRU_WIDTH), CONV_WIDTH ** -0.5),
        'conv_b': nrm(ks[14], (DEPTH, LRU_WIDTH), 0.01),
        'lru_w_a': nrm(ks[15], (DEPTH, LRU_BLOCKS, LRU_BW, LRU_BW), LRU_BW ** -0.5),
        'lru_b_a': nrm(ks[16], (DEPTH, LRU_WIDTH), 0.01),
        'lru_w_x': nrm(ks[17], (DEPTH, LRU_BLOCKS, LRU_BW, LRU_BW), LRU_BW ** -0.5),
        'lru_b_x': nrm(ks[18], (DEPTH, LRU_WIDTH), 0.01),
        'lru_lambda': lam,
        'swa_sinks': nrm(ks[21], (DEPTH, SWA_HEADS), 1.0),
        'ret_norm': gain(ks[22], (DEPTH, rv)),
        'w_branch_a': nrm(ks[23], (DEPTH, LRU_WIDTH, D_MODEL), LRU_WIDTH ** -0.5),
        'w_branch_b': nrm(ks[24], (DEPTH, hv, D_MODEL), hv ** -0.5),
        'w_branch_c': nrm(ks[25], (DEPTH, rv, D_MODEL), rv ** -0.5),
        'w_out': nrm(ks[26], (DEPTH, D_MODEL, D_MODEL), D_MODEL ** -0.5),
        'ffn2_norm': gain(ks[27], (DEPTH, D_MODEL)),
        'ffn2_w_gu': nrm(ks[28], (DEPTH, D_MODEL, 2 * D_FF), D_MODEL ** -0.5),
        'ffn2_w_down': nrm(ks[29], (DEPTH, D_FF, D_MODEL), D_FF ** -0.5),
        'final_norm': gain(ks[30], (D_MODEL,)),
    }


def reference(x_prompt, x_sample, cache_swa_k, cache_swa_v, state_conv, state_lru, state_ret,
              meta_tokens, ffn1_norm, ffn1_w_gu, ffn1_w_down, mix_norm, w_in, conv_w, conv_b,
              lru_w_a, lru_b_a, lru_w_x, lru_b_x, lru_lambda, swa_sinks, ret_norm,
              w_branch_a, w_branch_b, w_branch_c, w_out, ffn2_norm, ffn2_w_gu, ffn2_w_down, final_norm):
    params = {
        'ffn1_norm': ffn1_norm, 'ffn1_w_gu': ffn1_w_gu, 'ffn1_w_down': ffn1_w_down,
        'mix_norm': mix_norm, 'w_in': w_in, 'conv_w': conv_w, 'conv_b': conv_b,
        'lru_w_a': lru_w_a, 'lru_b_a': lru_b_a, 'lru_w_x': lru_w_x, 'lru_b_x': lru_b_x,
        'lru_lambda': lru_lambda, 'swa_sinks': swa_sinks, 'ret_norm': ret_norm,
        'w_branch_a': w_branch_a, 'w_branch_b': w_branch_b, 'w_branch_c': w_branch_c,
        'w_out': w_out, 'ffn2_norm': ffn2_norm, 'ffn2_w_gu': ffn2_w_gu, 'ffn2_w_down': ffn2_w_down,
    }
    buf = cache_swa_k.shape[2]
    bsz = x_prompt.shape[0]
    meta = jnp.broadcast_to(meta_tokens.astype(x_prompt.dtype)[None], (bsz, N_META, D_MODEL))
    xp = jnp.concatenate([meta, x_prompt], axis=1)
    pos_p = jnp.arange(xp.shape[1])
    pos_s = PAST_LEN + jnp.arange(x_sample.shape[1])
    yp, sp = trunk(xp, pos_p, None, params, final_norm, buf)
    ys, ss = trunk(x_sample, pos_s, [cache_swa_k, cache_swa_v, state_conv, state_lru, state_ret],
                   params, final_norm, buf)
    return (yp[:, N_META:], ys, sp[0], sp[1], sp[2], sp[3], sp[4], ss[0], ss[1], ss[2], ss[3], ss[4])
```

```python
import functools
import math

import jax
import jax.numpy as jnp
from jax import lax
from jax.experimental import pallas as pl
from jax.experimental.pallas import tpu as pltpu

F32 = jnp.float32
BF16 = jnp.bfloat16

D_MODEL = 1024
D_FF = 2048
N_META = 16
EPS = 1e-6
LRU_WIDTH = 512
LRU_BLOCKS = 8
LRU_BW = 64
CONV_WIDTH = 4
LRU_C = 8.0
SWA_HEAD_DIM = 64
SWA_HEADS = 8
SWA_KV_HEADS = 2
SWA_GROUP = 4
WINDOW = 128
RET_DK = 64
RET_DV = 128
RET_HEADS = 4
ROPE_BASE = 10000.0
GN_EPS = 1e-5
PAST_LEN = 8192

BLK = 128
N_MIX = 3328
N_GATE = 3 * D_MODEL
C_XA, C_YA, C_QS, C_KS, C_VS, C_QR, C_KR, C_VR, C_GR = 0, 512, 1024, 1536, 1664, 1792, 2048, 2304, 2816
N_OC = 3 * LRU_WIDTH
LOG_G = tuple(math.log1p(-(2.0 ** (-5.0 - h))) for h in range(RET_HEADS))

TM_DENSE = 256
G_SEQ = 16
VMEM_LIMIT = 56 * 1024 * 1024


def _dot(a, b):
    return jnp.dot(a, b, preferred_element_type=F32)


def _dot_nt(a, b):
    return lax.dot_general(a, b, (((1,), (1,)), ((), ())), preferred_element_type=F32)


def _dot_tn(a, b):
    return lax.dot_general(a, b, (((0,), (0,)), ((), ())), preferred_element_type=F32)


def _rms(x, g):
    return x * lax.rsqrt(jnp.mean(x * x, axis=-1, keepdims=True) + EPS) * g


def _softplus(x):
    return jnp.maximum(x, 0.0) + jnp.log1p(jnp.exp(-jnp.abs(x)))


def _swiglu(u, wgu_ref, wd_ref):
    gu = _dot(u, wgu_ref[...])
    act = (jax.nn.silu(gu[:, :D_FF]) * gu[:, D_FF:]).astype(BF16)
    return _dot(act, wd_ref[...])


def _in_body(x_ref, n1_ref, wgu_ref, wd_ref, n2_ref, wm_ref, wg_ref, x1_ref, pm_ref, pg_ref):
    x = x_ref[...]
    x1 = x + 0.5 * _swiglu(_rms(x, n1_ref[...]).astype(BF16), wgu_ref, wd_ref)
    x1_ref[...] = x1
    u2 = _rms(x1, n2_ref[...]).astype(BF16)
    pm_ref[...] = _dot(u2, wm_ref[...])
    pg_ref[...] = _dot(u2, wg_ref[...])


def _out_body(x1_ref, pg_ref, oc_ref, wa_ref, wb_ref, wc_ref, wo_ref, n_ref, wgu_ref, wd_ref, fn_ref, out_ref,
              *, final):
    g = jax.nn.sigmoid(pg_ref[...])
    oc = oc_ref[...]
    merged = (g[:, :D_MODEL] * _dot(oc[:, :LRU_WIDTH], wa_ref[...])
              + g[:, D_MODEL:2 * D_MODEL] * _dot(oc[:, LRU_WIDTH:2 * LRU_WIDTH], wb_ref[...])
              + g[:, 2 * D_MODEL:] * _dot(oc[:, 2 * LRU_WIDTH:], wc_ref[...]))
    x2 = x1_ref[...] + _dot(merged.astype(BF16), wo_ref[...])
    x3 = x2 + 0.5 * _swiglu(_rms(x2, n_ref[...]).astype(BF16), wgu_ref, wd_ref)
    if final:
        x3 = _rms(x3, fn_ref[...])
    out_ref[...] = x3


def _layer_spec(shape, layer):
    nd = len(shape)
    return pl.BlockSpec((None,) + tuple(shape[1:]), lambda *_: (layer,) + (0,) * (nd - 1),
                        pipeline_mode=pl.Buffered(1))


def _row_spec(tm, width):
    return pl.BlockSpec((tm, width), lambda i: (i, 0))


def _dense_params():
    return pltpu.CompilerParams(dimension_semantics=("arbitrary",), vmem_limit_bytes=VMEM_LIMIT)


def _call_in(x, p, layer, tm):
    n = x.shape[0]
    return pl.pallas_call(
        _in_body,
        grid=(n // tm,),
        in_specs=[_row_spec(tm, D_MODEL),
                  _layer_spec(p['ffn1_norm'].shape, layer), _layer_spec(p['ffn1_w_gu'].shape, layer),
                  _layer_spec(p['ffn1_w_down'].shape, layer), _layer_spec(p['mix_norm'].shape, layer),
                  _layer_spec(p['w_mix'].shape, layer), _layer_spec(p['w_gate'].shape, layer)],
        out_specs=[_row_spec(tm, D_MODEL), _row_spec(tm, N_MIX), _row_spec(tm, N_GATE)],
        out_shape=[jax.ShapeDtypeStruct((n, D_MODEL), F32), jax.ShapeDtypeStruct((n, N_MIX), F32),
                   jax.ShapeDtypeStruct((n, N_GATE), F32)],
        compiler_params=_dense_params(),
        name='layer_in',
    )(x, p['ffn1_norm'], p['ffn1_w_gu'], p['ffn1_w_down'], p['mix_norm'], p['w_mix'], p['w_gate'])


def _call_out(x1, pg, oc, p, layer, tm, final):
    n = x1.shape[0]
    return pl.pallas_call(
        functools.partial(_out_body, final=final),
        grid=(n // tm,),
        in_specs=[_row_spec(tm, D_MODEL), _row_spec(tm, N_GATE), _row_spec(tm, N_OC),
                  _layer_spec(p['w_branch_a'].shape, layer), _layer_spec(p['w_branch_b'].shape, layer),
                  _layer_spec(p['w_branch_c'].shape, layer), _layer_spec(p['w_out'].shape, layer),
                  _layer_spec(p['ffn2_norm'].shape, layer), _layer_spec(p['ffn2_w_gu'].shape, layer),
                  _layer_spec(p['ffn2_w_down'].shape, layer),
                  pl.BlockSpec((1, D_MODEL), lambda i: (0, 0))],
        out_specs=_row_spec(tm, D_MODEL),
        out_shape=jax.ShapeDtypeStruct((n, D_MODEL), F32),
        compiler_params=_dense_params(),
        name='layer_out',
    )(x1, pg, oc, p['w_branch_a'], p['w_branch_b'], p['w_branch_c'], p['w_out'], p['ffn2_norm'],
      p['ffn2_w_gu'], p['ffn2_w_down'], p['final_norm'])


def _lru_gates(xc, wa_ref, wx_ref, ba_ref, bx_ref, lam_ref):
    xcb = xc.astype(BF16)
    half = LRU_WIDTH // 2
    rpre = jnp.concatenate([_dot(xcb[:, :half], wa_ref[0]), _dot(xcb[:, half:], wa_ref[1])], axis=1)
    ipre = jnp.concatenate([_dot(xcb[:, :half], wx_ref[0]), _dot(xcb[:, half:], wx_ref[1])], axis=1)
    r = jax.nn.sigmoid(rpre + ba_ref[...])
    i = jax.nn.sigmoid(ipre + bx_ref[...])
    log_a = -LRU_C * r * _softplus(-lam_ref[...])
    a = jnp.exp(log_a)
    return a, jnp.sqrt(-jnp.tanh(log_a) * (a * a + 1.0)) * (i * xc)


def _rope(x, cos2, sin_signed, first_half):
    swapped = jnp.where(first_half, pltpu.roll(x, 4 * RET_DK - RET_DK // 2, 1), pltpu.roll(x, RET_DK // 2, 1))
    return x * cos2 + swapped * sin_signed


def _rope_tables(pos, inv_ref):
    ang = pos * inv_ref[...]
    cos = jnp.cos(ang)
    sin = jnp.sin(ang)
    lane = lax.broadcasted_iota(jnp.int32, (1, 4 * RET_DK), 1)
    first_half = (lane & (RET_DK - 1)) < RET_DK // 2
    cos2 = jnp.concatenate([cos, cos], axis=1)
    sin2 = jnp.concatenate([sin, sin], axis=1)
    return cos2, jnp.where(first_half, -sin2, sin2), first_half


def _group_norm_gate(o, gain, gr):
    mu = jnp.mean(o, axis=-1, keepdims=True)
    d = o - mu
    var = jnp.mean(d * d, axis=-1, keepdims=True)
    return d * lax.rsqrt(var + GN_EPS) * gain * jax.nn.silu(gr)


def _mixp_body(sinks_ref, pm_ref, cw_ref, cb_ref, wa_ref, wx_ref, ba_ref, bx_ref, lam_ref, inv_ref, gn_ref,
               oc_ref, klast_ref, vlast_ref, conv_ref, hlast_ref, sret_ref,
               xext, a_s, b_s, h_s, hcar, kprev, vprev, state, *, layer, pad, nblk):
    j = pl.program_id(1)

    @pl.when(j == 0)
    def _():
        xext[0:8, :] = jnp.zeros((8, LRU_WIDTH), F32)
        hcar[...] = jnp.zeros_like(hcar)
        kprev[...] = jnp.zeros_like(kprev)
        vprev[...] = jnp.zeros_like(vprev)
        state[...] = jnp.zeros_like(state)

    rows = lax.broadcasted_iota(jnp.int32, (BLK, 1), 0)
    valid = (j * BLK + rows) >= pad
    lane = lax.broadcasted_iota(jnp.int32, (1, 128), 1)
    lo = lane < 64

    xa = jnp.where(valid, pm_ref[0, :, C_XA:C_XA + LRU_WIDTH], 0.0)
    xext[8:8 + BLK, :] = xa
    xc = cb_ref[...] + xext[5:5 + BLK, :] * cw_ref[0:1, :]
    xc = xc + xext[6:6 + BLK, :] * cw_ref[1:2, :]
    xc = xc + xext[7:7 + BLK, :] * cw_ref[2:3, :]
    xc = xc + xa * cw_ref[3:4, :]
    xext[0:8, :] = xext[BLK:BLK + 8, :]
    a, bt = _lru_gates(xc, wa_ref, wx_ref, ba_ref, bx_ref, lam_ref)
    a_s[...] = a
    b_s[...] = jnp.where(valid, bt, 0.0)

    def scan_group(g, h):
        r0 = pl.multiple_of(g * 8, 8)
        av = a_s[pl.ds(r0, 8), :]
        bv = b_s[pl.ds(r0, 8), :]
        outs = []
        for r in range(8):
            h = av[r:r + 1, :] * h + bv[r:r + 1, :]
            outs.append(h)
        h_s[pl.ds(r0, 8), :] = jnp.concatenate(outs, axis=0)
        return h

    hcar[0:1, :] = lax.fori_loop(0, BLK // 8, scan_group, hcar[0:1, :])
    o_a = h_s[...] * jax.nn.gelu(pm_ref[0, :, C_YA:C_YA + LRU_WIDTH])
    oc_ref[0, :, 0:LRU_WIDTH] = o_a.astype(BF16)

    k = pm_ref[0, :, C_KS:C_KS + 128]
    v = pm_ref[0, :, C_VS:C_VS + 128]
    k_sw = pltpu.roll(k, 64, 1)
    v_sw = pltpu.roll(v, 64, 1)
    kdup = (jnp.where(lo, k, k_sw).astype(BF16), jnp.where(lo, k_sw, k).astype(BF16))
    vdup = (jnp.where(lo, v, v_sw).astype(BF16), jnp.where(lo, v_sw, v).astype(BF16))
    row4 = lax.broadcasted_iota(jnp.int32, (SWA_GROUP * BLK, 1), 0)
    t4 = row4 & (BLK - 1)
    col = lax.broadcasted_iota(jnp.int32, (1, 2 * BLK), 1)
    ok = ((j - 1) * BLK + col >= pad) & (col > t4) & (col <= t4 + BLK)
    for h in range(SWA_KV_HEADS):
        parts = []
        for g in range(SWA_GROUP):
            head = SWA_GROUP * h + g
            slab = pm_ref[0, :, C_QS + (head // 2) * 128:C_QS + (head // 2 + 1) * 128]
            parts.append(jnp.where(lo if head % 2 == 0 else jnp.logical_not(lo), slab, 0.0))
        qst = jnp.concatenate(parts, axis=0).astype(BF16)
        kcat = jnp.concatenate([kprev[h], kdup[h]], axis=0)
        vcat = jnp.concatenate([vprev[h], vdup[h]], axis=0)
        s = _dot_nt(qst, kcat) * SWA_HEAD_DIM ** -0.5
        s = jnp.where(ok, s, -jnp.inf)
        sk = jnp.full((SWA_GROUP * BLK, 1), sinks_ref[layer, SWA_GROUP * h], F32)
        for g in range(1, SWA_GROUP):
            sk = jnp.where(row4 >= g * BLK, sinks_ref[layer, SWA_GROUP * h + g], sk)
        m = jnp.maximum(jnp.max(s, axis=-1, keepdims=True), sk)
        e = jnp.exp(s - m)
        den = jnp.sum(e, axis=-1, keepdims=True) + jnp.exp(sk - m)
        o = _dot(e.astype(BF16), vcat) / den
        for sl in range(2):
            ge = 2 * sl
            slab = jnp.where(lo, o[ge * BLK:(ge + 1) * BLK, :], o[(ge + 1) * BLK:(ge + 2) * BLK, :])
            c0 = LRU_WIDTH + (2 * h + sl) * 128
            oc_ref[0, :, c0:c0 + 128] = slab.astype(BF16)
        kprev[h] = kdup[h]
        vprev[h] = vdup[h]

    pos = (j * BLK + rows - pad).astype(F32)
    cos2, sin_signed, first_half = _rope_tables(pos, inv_ref)
    qc = _rope(pm_ref[0, :, C_QR:C_QR + 256], cos2, sin_signed, first_half)
    kc = _rope(pm_ref[0, :, C_KR:C_KR + 256], cos2, sin_signed, first_half) * RET_DK ** -0.5
    kc = jnp.where(valid, kc, 0.0)
    ti = lax.broadcasted_iota(jnp.int32, (BLK, 1), 0).astype(F32)
    tj = lax.broadcasted_iota(jnp.int32, (1, BLK), 1).astype(F32)
    diff = ti - tj
    upd = [None, None]
    for h in range(RET_HEADS):
        sl = h // 2
        half = lo if h % 2 == 0 else jnp.logical_not(lo)
        qm = jnp.where(half, qc[:, sl * 128:(sl + 1) * 128], 0.0).astype(BF16)
        kslab = kc[:, sl * 128:(sl + 1) * 128]
        vh = jnp.where(valid, pm_ref[0, :, C_VR + h * RET_DV:C_VR + (h + 1) * RET_DV], 0.0).astype(BF16)
        sc = _dot_nt(qm, kslab.astype(BF16))
        dec = jnp.exp(jnp.where(diff >= 0, diff * LOG_G[h], -jnp.inf))
        o = _dot((sc * dec).astype(BF16), vh)
        o = o + _dot(qm, state[sl * 128:(sl + 1) * 128, :].astype(BF16)) * jnp.exp((ti + 1.0) * LOG_G[h])
        c0 = 2 * LRU_WIDTH + h * RET_DV
        oc = _group_norm_gate(o, gn_ref[:, h * RET_DV:(h + 1) * RET_DV],
                              pm_ref[0, :, C_GR + h * RET_DV:C_GR + (h + 1) * RET_DV])
        oc_ref[0, :, c0:c0 + RET_DV] = oc.astype(BF16)
        km = jnp.where(half, kslab * jnp.exp((BLK - 1.0 - ti) * LOG_G[h]), 0.0).astype(BF16)
        u = _dot_tn(km, vh)
        upd[sl] = u if upd[sl] is None else upd[sl] + u
    srow = lax.broadcasted_iota(jnp.int32, (128, 1), 0)
    for sl in range(2):
        gcol = jnp.where(srow < RET_DK, math.exp(BLK * LOG_G[2 * sl]), math.exp(BLK * LOG_G[2 * sl + 1]))
        state[sl * 128:(sl + 1) * 128, :] = gcol * state[sl * 128:(sl + 1) * 128, :] + upd[sl]

    @pl.when(j == nblk - 1)
    def _():
        klast_ref[0] = k
        vlast_ref[0] = v
        conv_ref[0] = xext[0:8, :]
        hlast_ref[0] = h_s[BLK - 8:BLK, :]
        sret_ref[0] = state[...]


def _call_mix_prompt(pm, p, layer, pad):
    bsz, tp, _ = pm.shape
    nblk = tp // BLK

    def lspec(shape):
        nd = len(shape)
        return pl.BlockSpec((None,) + tuple(shape[1:]), lambda b, j: (layer,) + (0,) * (nd - 1))

    def last(shape):
        return pl.BlockSpec((1,) + shape, lambda b, j: (b,) + (0,) * len(shape))

    return pl.pallas_call(
        functools.partial(_mixp_body, layer=layer, pad=pad, nblk=nblk),
        grid=(bsz, nblk),
        in_specs=[pl.BlockSpec(memory_space=pltpu.SMEM),
                  pl.BlockSpec((1, BLK, N_MIX), lambda b, j: (b, j, 0)),
                  lspec(p['conv_w'].shape), lspec(p['conv_b'].shape), lspec(p['lru_wa_bd'].shape),
                  lspec(p['lru_wx_bd'].shape), lspec(p['lru_b_a'].shape), lspec(p['lru_b_x'].shape),
                  lspec(p['lru_lambda'].shape), pl.BlockSpec((1, 128), lambda b, j: (0, 0)),
                  lspec(p['ret_norm'].shape)],
        out_specs=[pl.BlockSpec((1, BLK, N_OC), lambda b, j: (b, j, 0)),
                   last((BLK, 128)), last((BLK, 128)), last((8, LRU_WIDTH)), last((8, LRU_WIDTH)),
                   last((RET_HEADS * RET_DK, RET_DV))],
        out_shape=[jax.ShapeDtypeStruct((bsz, tp, N_OC), BF16),
                   jax.ShapeDtypeStruct((bsz, BLK, 128), F32), jax.ShapeDtypeStruct((bsz, BLK, 128), F32),
                   jax.ShapeDtypeStruct((bsz, 8, LRU_WIDTH), F32), jax.ShapeDtypeStruct((bsz, 8, LRU_WIDTH), F32),
                   jax.ShapeDtypeStruct((bsz, RET_HEADS * RET_DK, RET_DV), F32)],
        scratch_shapes=[pltpu.VMEM((BLK + 8, LRU_WIDTH), F32), pltpu.VMEM((BLK, LRU_WIDTH), F32),
                        pltpu.VMEM((BLK, LRU_WIDTH), F32), pltpu.VMEM((BLK, LRU_WIDTH), F32),
                        pltpu.VMEM((8, LRU_WIDTH), F32),
                        pltpu.VMEM((SWA_KV_HEADS, BLK, 128), BF16), pltpu.VMEM((SWA_KV_HEADS, BLK, 128), BF16),
                        pltpu.VMEM((RET_HEADS * RET_DK, RET_DV), F32)],
        compiler_params=pltpu.CompilerParams(dimension_semantics=("arbitrary", "arbitrary"),
                                             vmem_limit_bytes=VMEM_LIMIT),
        name='mix_prompt',
    )(p['swa_sinks'], pm, p['conv_w'], p['conv_b'], p['lru_wa_bd'], p['lru_wx_bd'], p['lru_b_a'], p['lru_b_x'],
      p['lru_lambda'], p['rope_inv'], p['ret_norm'])


def _mixs_body(pm_ref, ck_ref, cv_ref, conv_ref, h0_ref, s_ref, cw_ref, cb_ref, wa_ref, wx_ref, ba_ref, bx_ref,
               lam_ref, inv_ref, gn_ref, sk_ref,
               oc_ref, nk_ref, nv_ref, nconv_ref, nh_ref, ns_ref,
               qb, o8, qr_s, kr_s, v4_s, o2_s):
    gsz = G_SEQ
    nrow = 16

    @pl.when(pl.program_id(0) == 0)
    def _():
        qb[...] = jnp.zeros_like(qb)
        qr_s[...] = jnp.zeros_like(qr_s)
        kr_s[...] = jnp.zeros_like(kr_s)
        v4_s[...] = jnp.zeros_like(v4_s)

    lane = lax.broadcasted_iota(jnp.int32, (1, 128), 1)
    lo = lane < 64

    xa = pm_ref[:, C_XA:C_XA + LRU_WIDTH]
    h1 = conv_ref[:, LRU_WIDTH:2 * LRU_WIDTH]
    h2 = conv_ref[:, 2 * LRU_WIDTH:3 * LRU_WIDTH]
    xc = cb_ref[...] + conv_ref[:, 0:LRU_WIDTH] * cw_ref[0:1, :]
    xc = xc + h1 * cw_ref[1:2, :]
    xc = xc + h2 * cw_ref[2:3, :]
    xc = xc + xa * cw_ref[3:4, :]
    nconv_ref[:, 0:LRU_WIDTH] = h1
    nconv_ref[:, LRU_WIDTH:2 * LRU_WIDTH] = h2
    nconv_ref[:, 2 * LRU_WIDTH:3 * LRU_WIDTH] = xa
    a, bt = _lru_gates(xc, wa_ref, wx_ref, ba_ref, bx_ref, lam_ref)
    hn = bt + a * h0_ref[...]
    nh_ref[...] = hn
    oc_ref[:, 0:LRU_WIDTH] = (hn * jax.nn.gelu(pm_ref[:, C_YA:C_YA + LRU_WIDTH])).astype(BF16)

    nk_ref[:, 0:WINDOW - 1, :] = ck_ref[:, 1:WINDOW, :]
    nv_ref[:, 0:WINDOW - 1, :] = cv_ref[:, 1:WINDOW, :]
    for b in range(gsz):
        nk_ref[b, WINDOW - 1:WINDOW, :] = pm_ref[b:b + 1, C_KS:C_KS + 128]
        nv_ref[b, WINDOW - 1:WINDOW, :] = pm_ref[b:b + 1, C_VS:C_VS + 128]

    for r in range(SWA_HEADS):
        h = r // SWA_GROUP
        slab = pm_ref[:, C_QS + (r // 2) * 128:C_QS + (r // 2 + 1) * 128]
        if r % 2 != h:
            slab = pltpu.roll(slab, 64, 1)
        qb[r * gsz:(r + 1) * gsz, :] = jnp.where(lo if h == 0 else jnp.logical_not(lo), slab, 0.0)

    pos = jnp.full((1, 1), float(PAST_LEN), F32)
    cos2, sin_signed, first_half = _rope_tables(pos, inv_ref)
    qc = _rope(pm_ref[:, C_QR:C_QR + 256], cos2, sin_signed, first_half)
    kc = _rope(pm_ref[:, C_KR:C_KR + 256], cos2, sin_signed, first_half) * RET_DK ** -0.5
    lane256 = lax.broadcasted_iota(jnp.int32, (1, RET_HEADS * RET_DK), 1)
    for r in range(RET_HEADS):
        hm = (lane256 >= r * RET_DK) & (lane256 < (r + 1) * RET_DK)
        qm = jnp.where(hm, qc, 0.0)
        km = jnp.where(hm, kc, 0.0)
        for c in range(2):
            qr_s[c, r * gsz:(r + 1) * gsz, :] = qm[:, c * 128:(c + 1) * 128]
            kr_s[c, r * gsz:(r + 1) * gsz, :] = km[:, c * 128:(c + 1) * 128]
        v4_s[r * gsz:(r + 1) * gsz, :] = pm_ref[:, C_VR + r * RET_DV:C_VR + (r + 1) * RET_DV]

    srow = lax.broadcasted_iota(jnp.int32, (RET_HEADS * RET_DK, 1), 0)
    gcol = jnp.full((RET_HEADS * RET_DK, 1), math.exp(LOG_G[0]), F32)
    for r in range(1, RET_HEADS):
        gcol = jnp.where(srow >= r * RET_DK, math.exp(LOG_G[r]), gcol)
    sk = sk_ref[:, 0:1]

    def per_seq(b, carry):
        rows = pl.ds(b, nrow, stride=gsz)
        s = _dot_nt(qb[rows, :].astype(BF16), nk_ref[b].astype(BF16)) * SWA_HEAD_DIM ** -0.5
        m = jnp.maximum(jnp.max(s, axis=-1, keepdims=True), sk)
        e = jnp.exp(s - m)
        den = jnp.sum(e, axis=-1, keepdims=True) + jnp.exp(sk - m)
        o8[rows, :] = _dot(e.astype(BF16), nv_ref[b].astype(BF16)) / den
        sb = s_ref[b]
        sbb = sb.astype(BF16)
        o2_s[rows, :] = (_dot(qr_s[0, rows, :].astype(BF16), sbb[0:128, :])
                         + _dot(qr_s[1, rows, :].astype(BF16), sbb[128:256, :]))
        v4 = v4_s[rows, :].astype(BF16)
        kv = jnp.concatenate([_dot_tn(kr_s[0, rows, :].astype(BF16), v4),
                              _dot_tn(kr_s[1, rows, :].astype(BF16), v4)], axis=0)
        ns_ref[b] = gcol * sb + kv
        return carry

    lax.fori_loop(0, gsz, per_seq, 0)

    for sl in range(4):
        h = sl // 2
        ev = o8[(2 * sl) * gsz:(2 * sl + 1) * gsz, :]
        od = o8[(2 * sl + 1) * gsz:(2 * sl + 2) * gsz, :]
        if h != 0:
            ev = pltpu.roll(ev, 64, 1)
        if h != 1:
            od = pltpu.roll(od, 64, 1)
        oc_ref[:, LRU_WIDTH + sl * 128:LRU_WIDTH + (sl + 1) * 128] = jnp.where(lo, ev, od).astype(BF16)

    prod = qc * kc
    p_hi = prod.astype(BF16)
    p_lo = (prod - p_hi.astype(F32)).astype(BF16)
    er = lax.broadcasted_iota(jnp.int32, (RET_HEADS * RET_DK, RET_HEADS * RET_DV), 0) // RET_DK
    ec = lax.broadcasted_iota(jnp.int32, (RET_HEADS * RET_DK, RET_HEADS * RET_DV), 1) // RET_DV
    expand = jnp.where(er == ec, 1.0, 0.0).astype(BF16)
    qk = _dot(p_hi, expand) + _dot(p_lo, expand)
    for r in range(RET_HEADS):
        cs = slice(r * RET_DV, (r + 1) * RET_DV)
        o = qk[:, cs] * pm_ref[:, C_VR + r * RET_DV:C_VR + (r + 1) * RET_DV]
        o = o + o2_s[r * gsz:(r + 1) * gsz, :] * math.exp(LOG_G[r])
        oc = _group_norm_gate(o, gn_ref[:, cs], pm_ref[:, C_GR + r * RET_DV:C_GR + (r + 1) * RET_DV])
        oc_ref[:, 2 * LRU_WIDTH + r * RET_DV:2 * LRU_WIDTH + (r + 1) * RET_DV] = oc.astype(BF16)


def _call_mix_sample(pm, ck, cv, conv, h0, sret, p, layer):
    nseq = pm.shape[0]
    gsz = G_SEQ

    def lspec(shape):
        nd = len(shape)
        return pl.BlockSpec((None,) + tuple(shape[1:]), lambda i: (layer,) + (0,) * (nd - 1))

    def seq2(width):
        return pl.BlockSpec((gsz, width), lambda i: (i, 0))

    def seq3(layered, d1, d2):
        if layered:
            return pl.BlockSpec((None, gsz, d1, d2), lambda i: (layer, i, 0, 0))
        return pl.BlockSpec((gsz, d1, d2), lambda i: (i, 0, 0))

    sdim = RET_HEADS * RET_DK
    return pl.pallas_call(
        _mixs_body,
        grid=(nseq // gsz,),
        in_specs=[seq2(N_MIX), seq3(True, WINDOW, 128), seq3(True, WINDOW, 128),
                  pl.BlockSpec((None, gsz, 3 * LRU_WIDTH), lambda i: (layer, i, 0)),
                  pl.BlockSpec((None, gsz, LRU_WIDTH), lambda i: (layer, i, 0)),
                  seq3(True, sdim, RET_DV),
                  lspec(p['conv_w'].shape), lspec(p['conv_b'].shape), lspec(p['lru_wa_bd'].shape),
                  lspec(p['lru_wx_bd'].shape), lspec(p['lru_b_a'].shape), lspec(p['lru_b_x'].shape),
                  lspec(p['lru_lambda'].shape), pl.BlockSpec((1, 128), lambda i: (0, 0)),
                  lspec(p['ret_norm'].shape), lspec(p['sinks16'].shape)],
        out_specs=[seq2(N_OC), seq3(False, WINDOW, 128), seq3(False, WINDOW, 128), seq2(3 * LRU_WIDTH),
                   seq2(LRU_WIDTH), seq3(False, sdim, RET_DV)],
        out_shape=[jax.ShapeDtypeStruct((nseq, N_OC), BF16),
                   jax.ShapeDtypeStruct((nseq, WINDOW, 128), F32), jax.ShapeDtypeStruct((nseq, WINDOW, 128), F32),
                   jax.ShapeDtypeStruct((nseq, 3 * LRU_WIDTH), F32), jax.ShapeDtypeStruct((nseq, LRU_WIDTH), F32),
                   jax.ShapeDtypeStruct((nseq, sdim, RET_DV), F32)],
        scratch_shapes=[pltpu.VMEM((16 * gsz, 128), F32), pltpu.VMEM((16 * gsz, 128), F32),
                        pltpu.VMEM((2, 16 * gsz, 128), F32), pltpu.VMEM((2, 16 * gsz, 128), F32),
                        pltpu.VMEM((16 * gsz, 128), F32), pltpu.VMEM((16 * gsz, 128), F32)],
        compiler_params=pltpu.CompilerParams(dimension_semantics=("arbitrary",), vmem_limit_bytes=VMEM_LIMIT),
        name='mix_sample',
    )(pm, ck, cv, conv, h0, sret, p['conv_w'], p['conv_b'], p['lru_wa_bd'], p['lru_wx_bd'], p['lru_b_a'],
      p['lru_b_x'], p['lru_lambda'], p['rope_inv'], p['ret_norm'], p['sinks16'])


def _block_diag(w):
    depth = w.shape[0]
    w = w.reshape(depth, 2, 4, LRU_BW, LRU_BW)
    eye = jnp.eye(4, dtype=w.dtype)
    return jnp.einsum('lsncd,nm->lsncmd', w, eye).reshape(depth, 2, 4 * LRU_BW, 4 * LRU_BW)


def kernel(x_prompt, x_sample, cache_swa_k, cache_swa_v, state_conv, state_lru, state_ret, meta_tokens, ffn1_norm,
           ffn1_w_gu, ffn1_w_down, mix_norm, w_in, conv_w, conv_b, lru_w_a, lru_b_a, lru_w_x, lru_b_x, lru_lambda,
           swa_sinks, ret_norm, w_branch_a, w_branch_b, w_branch_c, w_out, ffn2_norm, ffn2_w_gu, ffn2_w_down,
           final_norm):
    depth = w_in.shape[0]
    bsz, seq, _ = x_prompt.shape
    nseq = x_sample.shape[0]
    buf = cache_swa_k.shape[2]
    assert buf == WINDOW == BLK and x_sample.shape[1] == 1 and nseq % G_SEQ == 0
    t = seq + N_META
    pad = (-t) % BLK
    tp = t + pad
    assert (bsz * tp) % TM_DENSE == 0

    def row(v):
        return v.reshape(depth, 1, -1).astype(F32)

    half = jnp.arange(128) % (RET_DK // 2)
    p = {
        'ffn1_norm': row(ffn1_norm), 'ffn1_w_gu': ffn1_w_gu.astype(BF16), 'ffn1_w_down': ffn1_w_down.astype(BF16),
        'mix_norm': row(mix_norm), 'w_mix': w_in[:, :, :N_MIX].astype(BF16), 'w_gate': w_in[:, :, N_MIX:].astype(BF16),
        'conv_w': conv_w.astype(F32), 'conv_b': row(conv_b),
        'lru_wa_bd': _block_diag(lru_w_a).astype(BF16), 'lru_wx_bd': _block_diag(lru_w_x).astype(BF16),
        'lru_b_a': row(lru_b_a), 'lru_b_x': row(lru_b_x), 'lru_lambda': row(lru_lambda),
        'swa_sinks': swa_sinks.astype(F32),
        'sinks16': jnp.pad(jnp.broadcast_to(swa_sinks.astype(F32)[:, :, None], (depth, SWA_HEADS, 128)),
                           ((0, 0), (0, 16 - SWA_HEADS), (0, 0))),
        'ret_norm': row(ret_norm),
        'w_branch_a': w_branch_a.astype(BF16), 'w_branch_b': w_branch_b.astype(BF16),
        'w_branch_c': w_branch_c.astype(BF16), 'w_out': w_out.astype(BF16),
        'ffn2_norm': row(ffn2_norm), 'ffn2_w_gu': ffn2_w_gu.astype(BF16), 'ffn2_w_down': ffn2_w_down.astype(BF16),
        'final_norm': final_norm.reshape(1, D_MODEL).astype(F32),
        'rope_inv': (ROPE_BASE ** (-half.astype(F32) / (RET_DK // 2))).reshape(1, 128),
    }

    meta = jnp.broadcast_to(meta_tokens.astype(F32)[None], (bsz, N_META, D_MODEL))
    xp = jnp.concatenate([jnp.zeros((bsz, pad, D_MODEL), F32), meta, x_prompt], axis=1).reshape(bsz * tp, D_MODEL)
    xs = x_sample.reshape(nseq, D_MODEL)
    ck = cache_swa_k.reshape(depth, nseq, buf, 128)
    cv = cache_swa_v.reshape(depth, nseq, buf, 128)
    conv = state_conv.reshape(depth, nseq, 3 * LRU_WIDTH)
    sret = state_ret.reshape(depth, nseq, RET_HEADS * RET_DK, RET_DV)

    outs_p = [[] for _ in range(5)]
    outs_s = [[] for _ in range(5)]
    for layer in range(depth):
        final = layer == depth - 1
        x1, pm, pg = _call_in(xp, p, layer, TM_DENSE)
        oc, kl, vl, cl, hl, sl = _call_mix_prompt(pm.reshape(bsz, tp, N_MIX), p, layer, pad)
        xp = _call_out(x1, pg, oc.reshape(bsz * tp, N_OC), p, layer, TM_DENSE, final)
        for acc, o in zip(outs_p, (kl.reshape(bsz, buf, SWA_KV_HEADS, SWA_HEAD_DIM),
                                   vl.reshape(bsz, buf, SWA_KV_HEADS, SWA_HEAD_DIM),
                                   cl[:, 8 - (CONV_WIDTH - 1):, :], hl[:, 7, :],
                                   sl.reshape(bsz, RET_HEADS, RET_DK, RET_DV))):
            acc.append(o)

        x1, pm, pg = _call_in(xs, p, layer, nseq)
        oc, nk, nv, nc, nh, ns = _call_mix_sample(pm, ck, cv, conv, state_lru, sret, p, layer)
        xs = _call_out(x1, pg, oc, p, layer, nseq, final)
        for acc, o in zip(outs_s, (nk.reshape(nseq, buf, SWA_KV_HEADS, SWA_HEAD_DIM),
                                   nv.reshape(nseq, buf, SWA_KV_HEADS, SWA_HEAD_DIM),
                                   nc.reshape(nseq, CONV_WIDTH - 1, LRU_WIDTH), nh,
                                   ns.reshape(nseq, RET_HEADS, RET_DK, RET_DV))):
            acc.append(o)

    yp = xp.reshape(bsz, tp, D_MODEL)[:, pad + N_META:]
    ys = xs.reshape(nseq, 1, D_MODEL)
    return (yp, ys) + tuple(jnp.stack(a) for a in outs_p) + tuple(jnp.stack(a) for a in outs_s)
```

```python
import functools
import math

import jax
import jax.numpy as jnp
from jax import lax
from jax.experimental import pallas as pl
from jax.experimental.pallas import tpu as pltpu

F32 = jnp.float32
BF16 = jnp.bfloat16

D_MODEL = 1024
D_FF = 2048
N_META = 16
EPS = 1e-6
LRU_WIDTH = 512
LRU_BLOCKS = 8
LRU_BW = 64
CONV_WIDTH = 4
LRU_C = 8.0
SWA_HEAD_DIM = 64
SWA_HEADS = 8
SWA_KV_HEADS = 2
SWA_GROUP = 4
WINDOW = 128
RET_DK = 64
RET_DV = 128
RET_HEADS = 4
ROPE_BASE = 10000.0
GN_EPS = 1e-5
PAST_LEN = 8192

BLK = 128
N_MIX = 3328
N_GATE = 3 * D_MODEL
C_XA, C_YA, C_QS, C_KS, C_VS, C_QR, C_KR, C_VR, C_GR = 0, 512, 1024, 1536, 1664, 1792, 2048, 2304, 2816
N_OC = 3 * LRU_WIDTH
LOG_G = tuple(math.log1p(-(2.0 ** (-5.0 - h))) for h in range(RET_HEADS))

TM_DENSE = 256
G_SEQ = 16
VMEM_LIMIT = 56 * 1024 * 1024


def _dot(a, b):
    return jnp.dot(a, b, preferred_element_type=F32)


def _dot_nt(a, b):
    return lax.dot_general(a, b, (((1,), (1,)), ((), ())), preferred_element_type=F32)


def _dot_tn(a, b):
    return lax.dot_general(a, b, (((0,), (0,)), ((), ())), preferred_element_type=F32)


def _rms(x, g):
    return x * lax.rsqrt(jnp.mean(x * x, axis=-1, keepdims=True) + EPS) * g


def _softplus(x):
    return jnp.maximum(x, 0.0) + jnp.log1p(jnp.exp(-jnp.abs(x)))


def _swiglu(u, wgu_ref, wd_ref):
    gu = _dot(u, wgu_ref[...])
    act = (jax.nn.silu(gu[:, :D_FF]) * gu[:, D_FF:]).astype(BF16)
    return _dot(act, wd_ref[...])


def _in_body(x_ref, n1_ref, wgu_ref, wd_ref, n2_ref, wm_ref, wg_ref, x1_ref, pm_ref, pg_ref):
    x = x_ref[...]
    x1 = x + 0.5 * _swiglu(_rms(x, n1_ref[...]).astype(BF16), wgu_ref, wd_ref)
    x1_ref[...] = x1
    u2 = _rms(x1, n2_ref[...]).astype(BF16)
    pm_ref[...] = _dot(u2, wm_ref[...])
    pg_ref[...] = _dot(u2, wg_ref[...])


def _merge_out(x1, pg, oc, wa_ref, wb_ref, wc_ref, wo_ref, n_ref, wgu_ref, wd_ref, fn_ref, final):
    g = jax.nn.sigmoid(pg)
    merged = (g[:, :D_MODEL] * _dot(oc[:, :LRU_WIDTH], wa_ref[...])
              + g[:, D_MODEL:2 * D_MODEL] * _dot(oc[:, LRU_WIDTH:2 * LRU_WIDTH], wb_ref[...])
              + g[:, 2 * D_MODEL:] * _dot(oc[:, 2 * LRU_WIDTH:], wc_ref[...]))
    x2 = x1 + _dot(merged.astype(BF16), wo_ref[...])
    x3 = x2 + 0.5 * _swiglu(_rms(x2, n_ref[...]).astype(BF16), wgu_ref, wd_ref)
    if final:
        x3 = _rms(x3, fn_ref[...])
    return x3


FF_CHUNK = 512


def _merge_out_stages(x1_ref, pg_ref, oc, wa_ref, wb_ref, wc_ref, wo_ref, n_ref, wgu_ref, wd_ref, fn_ref, out_ref,
                      final):
    st = {}

    def merge():
        g = jax.nn.sigmoid(pg_ref[...])
        st['merged'] = (g[:, :D_MODEL] * _dot(oc[:, :LRU_WIDTH], wa_ref[...])
                        + g[:, D_MODEL:2 * D_MODEL] * _dot(oc[:, LRU_WIDTH:2 * LRU_WIDTH], wb_ref[...])
                        + g[:, 2 * D_MODEL:] * _dot(oc[:, 2 * LRU_WIDTH:], wc_ref[...])).astype(BF16)

    def project():
        st['x2'] = x1_ref[...] + _dot(st['merged'], wo_ref[...])
        st['u'] = _rms(st['x2'], n_ref[...]).astype(BF16)

    def ffn_chunk(c):
        def run():
            lo, hi = c * FF_CHUNK, (c + 1) * FF_CHUNK
            gate = _dot(st['u'], wgu_ref[:, lo:hi])
            up = _dot(st['u'], wgu_ref[:, D_FF + lo:D_FF + hi])
            y = _dot((jax.nn.silu(gate) * up).astype(BF16), wd_ref[lo:hi, :])
            st['y'] = y if c == 0 else st['y'] + y
        return run

    def store():
        x3 = st['x2'] + 0.5 * st['y']
        out_ref[...] = _rms(x3, fn_ref[...]) if final else x3

    return [merge, project] + [ffn_chunk(c) for c in range(D_FF // FF_CHUNK)] + [store]


def _out_body(x1_ref, pg_ref, oc_ref, wa_ref, wb_ref, wc_ref, wo_ref, n_ref, wgu_ref, wd_ref, fn_ref, out_ref,
              *, final):
    out_ref[...] = _merge_out(x1_ref[...], pg_ref[...], oc_ref[...], wa_ref, wb_ref, wc_ref, wo_ref, n_ref,
                              wgu_ref, wd_ref, fn_ref, final)


def _layer_spec(shape, layer):
    nd = len(shape)
    return pl.BlockSpec((None,) + tuple(shape[1:]), lambda *_: (layer,) + (0,) * (nd - 1),
                        pipeline_mode=pl.Buffered(1))


def _row_spec(tm, width):
    return pl.BlockSpec((tm, width), lambda i: (i, 0))


def _dense_params():
    return pltpu.CompilerParams(dimension_semantics=("arbitrary",), vmem_limit_bytes=VMEM_LIMIT)


def _call_in(x, p, layer, tm):
    n = x.shape[0]
    return pl.pallas_call(
        _in_body,
        grid=(n // tm,),
        in_specs=[_row_spec(tm, D_MODEL),
                  _layer_spec(p['ffn1_norm'].shape, layer), _layer_spec(p['ffn1_w_gu'].shape, layer),
                  _layer_spec(p['ffn1_w_down'].shape, layer), _layer_spec(p['mix_norm'].shape, layer),
                  _layer_spec(p['w_mix'].shape, layer), _layer_spec(p['w_gate'].shape, layer)],
        out_specs=[_row_spec(tm, D_MODEL), _row_spec(tm, N_MIX), _row_spec(tm, N_GATE)],
        out_shape=[jax.ShapeDtypeStruct((n, D_MODEL), F32), jax.ShapeDtypeStruct((n, N_MIX), F32),
                   jax.ShapeDtypeStruct((n, N_GATE), F32)],
        compiler_params=_dense_params(),
        name='layer_in',
    )(x, p['ffn1_norm'], p['ffn1_w_gu'], p['ffn1_w_down'], p['mix_norm'], p['w_mix'], p['w_gate'])


_OUT_WEIGHTS = ('w_branch_a', 'w_branch_b', 'w_branch_c', 'w_out', 'ffn2_norm', 'ffn2_w_gu', 'ffn2_w_down')


def _call_out(x1, pg, oc, p, layer, tm, final):
    n = x1.shape[0]
    return pl.pallas_call(
        functools.partial(_out_body, final=final),
        grid=(n // tm,),
        in_specs=[_row_spec(tm, D_MODEL), _row_spec(tm, N_GATE), _row_spec(tm, N_OC)]
        + [_layer_spec(p[k].shape, layer) for k in _OUT_WEIGHTS]
        + [pl.BlockSpec((1, D_MODEL), lambda i: (0, 0))],
        out_specs=_row_spec(tm, D_MODEL),
        out_shape=jax.ShapeDtypeStruct((n, D_MODEL), F32),
        compiler_params=_dense_params(),
        name='layer_out',
    )(x1, pg, oc, *[p[k] for k in _OUT_WEIGHTS], p['final_norm'])


def _lru_gates(xc, wa_ref, wx_ref, ba_ref, bx_ref, lam_ref):
    xcb = xc.astype(BF16)
    half = LRU_WIDTH // 2
    rpre = jnp.concatenate([_dot(xcb[:, :half], wa_ref[0]), _dot(xcb[:, half:], wa_ref[1])], axis=1)
    ipre = jnp.concatenate([_dot(xcb[:, :half], wx_ref[0]), _dot(xcb[:, half:], wx_ref[1])], axis=1)
    r = jax.nn.sigmoid(rpre + ba_ref[...])
    i = jax.nn.sigmoid(ipre + bx_ref[...])
    log_a = -LRU_C * r * _softplus(-lam_ref[...])
    a = jnp.exp(log_a)
    return a, jnp.sqrt(-jnp.tanh(log_a) * (a * a + 1.0)) * (i * xc)


def _rope_operands(cos, sin):
    lane = lax.broadcasted_iota(jnp.int32, (1, 4 * RET_DK), 1)
    first_half = (lane & (RET_DK - 1)) < RET_DK // 2
    cos2 = jnp.concatenate([cos, cos], axis=1)
    sin2 = jnp.concatenate([sin, sin], axis=1)
    return cos2, jnp.where(first_half, -sin2, sin2), first_half


def _rope(x, cos2, sin_signed, first_half):
    swapped = jnp.where(first_half, pltpu.roll(x, 4 * RET_DK - RET_DK // 2, 1), pltpu.roll(x, RET_DK // 2, 1))
    return x * cos2 + swapped * sin_signed


def _group_norm_gate(o, gain, gr):
    mu = jnp.mean(o, axis=-1, keepdims=True)
    d = o - mu
    var = jnp.mean(d * d, axis=-1, keepdims=True)
    return d * lax.rsqrt(var + GN_EPS) * gain * jax.nn.silu(gr)


def _ropetab_body(inv_ref, cos_ref, sin_ref, *, rows_per_step, pad):
    rows = lax.broadcasted_iota(jnp.int32, (rows_per_step, 1), 0)
    pos = (pl.program_id(0) * rows_per_step + rows - pad).astype(F32)
    ang = pos * inv_ref[...]
    cos_ref[...] = jnp.cos(ang)
    sin_ref[...] = jnp.sin(ang)


def _call_ropetab(inv, tp, pad):
    nblk = tp // BLK
    rps = BLK * max(d for d in (8, 5, 4, 2, 1) if nblk % d == 0)
    spec = pl.BlockSpec((rps, 128), lambda i: (i, 0))
    return pl.pallas_call(
        functools.partial(_ropetab_body, rows_per_step=rps, pad=pad),
        grid=(tp // rps,),
        in_specs=[pl.BlockSpec((1, 128), lambda i: (0, 0))],
        out_specs=[spec, spec],
        out_shape=[jax.ShapeDtypeStruct((tp, 128), F32), jax.ShapeDtypeStruct((tp, 128), F32)],
        name='rope_tables',
    )(inv)


def _block_masks(j, pad):
    lo = lax.broadcasted_iota(jnp.int32, (1, 128), 1) < 64
    rows = lax.broadcasted_iota(jnp.int32, (BLK, 1), 0)
    return lo, (j * BLK + rows) >= pad


def _mix_lru(pm_ref, cw_ref, cb_ref, wa_ref, wx_ref, ba_ref, bx_ref, lam_ref, xext, ocs, hcar, *, pad, j, cur, r0):
    rs = slice(r0, r0 + BLK)
    fresh = j == 0
    _, valid = _block_masks(j, pad)
    xa = jnp.where(valid, pm_ref[rs,C_XA:C_XA + LRU_WIDTH], 0.0)
    xext[8:8 + BLK, :] = xa
    xext[0:8, :] = jnp.where(fresh, 0.0, xext[0:8, :])
    xc = cb_ref[...] + xext[5:5 + BLK, :] * cw_ref[0:1, :]
    xc = xc + xext[6:6 + BLK, :] * cw_ref[1:2, :]
    xc = xc + xext[7:7 + BLK, :] * cw_ref[2:3, :]
    xc = xc + xa * cw_ref[3:4, :]
    xext[0:8, :] = xext[BLK:BLK + 8, :]
    a, bt = _lru_gates(xc, wa_ref, wx_ref, ba_ref, bx_ref, lam_ref)
    bt = jnp.where(valid, bt, 0.0)
    row8 = lax.broadcasted_iota(jnp.int32, (8, 1), 0)
    h_in = jnp.where(fresh, 0.0, hcar[7:8, :])
    hs = []
    for g in range(BLK // 8):
        ag = a[g * 8:(g + 1) * 8, :]
        bg = bt[g * 8:(g + 1) * 8, :]
        for k in (1, 2, 4):
            keep = row8 >= k
            a_sh = jnp.where(keep, pltpu.roll(ag, k, 0), 1.0)
            b_sh = jnp.where(keep, pltpu.roll(bg, k, 0), 0.0)
            bg = bg + ag * b_sh
            ag = ag * a_sh
        hg = ag * h_in + bg
        h_in = hg[7:8, :]
        hs.append(hg)
    hcar[...] = hs[-1]
    o_a = jnp.concatenate(hs, axis=0) * jax.nn.gelu(pm_ref[rs,C_YA:C_YA + LRU_WIDTH])
    ocs[cur, rs,0:LRU_WIDTH] = o_a.astype(BF16)


def _mix_swa(sinks_ref, pm_ref, ocs, kprev, vprev, *, layer, pad, j, cur, r0):
    rs = slice(r0, r0 + BLK)
    lo, _ = _block_masks(j, pad)
    k = pm_ref[rs,C_KS:C_KS + 128]
    v = pm_ref[rs,C_VS:C_VS + 128]
    k_sw = pltpu.roll(k, 64, 1)
    v_sw = pltpu.roll(v, 64, 1)
    kdup = (jnp.where(lo, k, k_sw).astype(BF16), jnp.where(lo, k_sw, k).astype(BF16))
    vdup = (jnp.where(lo, v, v_sw).astype(BF16), jnp.where(lo, v_sw, v).astype(BF16))
    row4 = lax.broadcasted_iota(jnp.int32, (SWA_GROUP * BLK, 1), 0)
    t4 = row4 & (BLK - 1)
    col = lax.broadcasted_iota(jnp.int32, (1, 2 * BLK), 1)
    ok = ((j - 1) * BLK + col >= pad) & (col > t4) & (col <= t4 + BLK)
    for h in range(SWA_KV_HEADS):
        parts = []
        for g in range(SWA_GROUP):
            head = SWA_GROUP * h + g
            slab = pm_ref[rs,C_QS + (head // 2) * 128:C_QS + (head // 2 + 1) * 128] * SWA_HEAD_DIM ** -0.5
            parts.append(jnp.where(lo if head % 2 == 0 else jnp.logical_not(lo), slab, 0.0))
        qst = jnp.concatenate(parts, axis=0).astype(BF16)
        kcat = jnp.concatenate([kprev[h], kdup[h]], axis=0)
        vcat = jnp.concatenate([vprev[h], vdup[h]], axis=0)
        sc = jnp.where(ok, _dot_nt(qst, kcat), -jnp.inf)
        sk = jnp.full((SWA_GROUP * BLK, 1), sinks_ref[layer, SWA_GROUP * h], F32)
        for g in range(1, SWA_GROUP):
            sk = jnp.where(row4 >= g * BLK, sinks_ref[layer, SWA_GROUP * h + g], sk)
        m = jnp.maximum(jnp.max(sc, axis=-1, keepdims=True), sk)
        e = jnp.exp(sc - m)
        den = jnp.sum(e, axis=-1, keepdims=True) + jnp.exp(sk - m)
        o = _dot(e.astype(BF16), vcat) / den
        for sl in range(2):
            ge = 2 * sl
            slab = jnp.where(lo, o[ge * BLK:(ge + 1) * BLK, :], o[(ge + 1) * BLK:(ge + 2) * BLK, :])
            c0 = LRU_WIDTH + (2 * h + sl) * 128
            ocs[cur, rs,c0:c0 + 128] = slab.astype(BF16)
        kprev[h] = kdup[h]
        vprev[h] = vdup[h]


def _mix_ret(pm_ref, cos_ref, sin_ref, gn_ref, ocs, state, dec_t, cross_t, kdec_t, *, pad, j, cur, r0):
    rs = slice(r0, r0 + BLK)
    fresh = j == 0
    lo, valid = _block_masks(j, pad)
    cos2, sin_signed, first_half = _rope_operands(cos_ref[...], sin_ref[...])
    qc = _rope(pm_ref[rs,C_QR:C_QR + 256], cos2, sin_signed, first_half)
    kc = _rope(pm_ref[rs,C_KR:C_KR + 256], cos2, sin_signed, first_half) * RET_DK ** -0.5
    kc = jnp.where(valid, kc, 0.0)
    st = [jnp.where(fresh, 0.0, state[sl * 128:(sl + 1) * 128, :]) for sl in range(2)]
    upd = [None, None]
    for h in range(RET_HEADS):
        sl = h // 2
        half = lo if h % 2 == 0 else jnp.logical_not(lo)
        qm = jnp.where(half, qc[:, sl * 128:(sl + 1) * 128], 0.0).astype(BF16)
        kslab = kc[:, sl * 128:(sl + 1) * 128]
        vh = jnp.where(valid, pm_ref[rs,C_VR + h * RET_DV:C_VR + (h + 1) * RET_DV], 0.0).astype(BF16)
        o = _dot((_dot_nt(qm, kslab.astype(BF16)) * dec_t[h]).astype(BF16), vh)
        o = o + _dot(qm, st[sl].astype(BF16)) * cross_t[h]
        c0 = 2 * LRU_WIDTH + h * RET_DV
        oc = _group_norm_gate(o, gn_ref[:, h * RET_DV:(h + 1) * RET_DV],
                              pm_ref[rs,C_GR + h * RET_DV:C_GR + (h + 1) * RET_DV])
        ocs[cur, rs,c0:c0 + RET_DV] = oc.astype(BF16)
        km = jnp.where(half, kslab * kdec_t[sl], 0.0).astype(BF16)
        u = _dot_tn(km, vh)
        upd[sl] = u if upd[sl] is None else upd[sl] + u
    srow = lax.broadcasted_iota(jnp.int32, (128, 1), 0)
    for sl in range(2):
        gcol = jnp.where(srow < RET_DK, math.exp(BLK * LOG_G[2 * sl]), math.exp(BLK * LOG_G[2 * sl + 1]))
        state[sl * 128:(sl + 1) * 128, :] = gcol * st[sl] + upd[sl]


def _mixout_body(sinks_ref, pm_ref, cos0_ref, sin0_ref, cos1_ref, sin1_ref, x1_ref, pg_ref,
                 cw_ref, cb_ref, wa_ref, wx_ref, ba_ref, bx_ref, lam_ref, gn_ref,
                 wba_ref, wbb_ref, wbc_ref, wo_ref, n2_ref, wgu_ref, wd_ref, fn_ref,
                 out_ref, klast_ref, vlast_ref, conv_ref, hlast_ref, sret_ref,
                 xext, ocs, hcar, kprev, vprev, state, dec_t, cross_t, kdec_t, snap_x, snap_h, snap_s,
                 *, layer, pad, nblk, final):
    s = pl.program_id(0)

    @pl.when(s == 0)
    def _():
        lo = lax.broadcasted_iota(jnp.int32, (1, 128), 1) < 64
        ti = lax.broadcasted_iota(jnp.int32, (BLK, 1), 0).astype(F32)
        tj = lax.broadcasted_iota(jnp.int32, (1, BLK), 1).astype(F32)
        diff = ti - tj
        for h in range(RET_HEADS):
            dec_t[h] = jnp.exp(jnp.where(diff >= 0, diff * LOG_G[h], -jnp.inf))
            cross_t[h] = jnp.broadcast_to(jnp.exp((ti + 1.0) * LOG_G[h]), (BLK, 128))
        for sl in range(2):
            kdec_t[sl] = jnp.where(lo, jnp.exp((BLK - 1.0 - ti) * LOG_G[2 * sl]),
                                   jnp.exp((BLK - 1.0 - ti) * LOG_G[2 * sl + 1]))
        for ref in (ocs, xext, hcar, kprev, vprev, state):
            ref[...] = jnp.zeros_like(ref)

    cur = lax.rem(s, 2)
    j0 = lax.rem(2 * s, nblk)
    j1 = lax.rem(2 * s + 1, nblk)
    dense = _merge_out_stages(x1_ref, pg_ref, ocs[lax.rem(s + 1, 2)], wba_ref, wbb_ref, wbc_ref, wo_ref, n2_ref,
                              wgu_ref, wd_ref, fn_ref, out_ref, final)

    def lru(j, r0):
        _mix_lru(pm_ref, cw_ref, cb_ref, wa_ref, wx_ref, ba_ref, bx_ref, lam_ref, xext, ocs, hcar,
                 pad=pad, j=j, cur=cur, r0=r0)

    def swa(j, r0):
        _mix_swa(sinks_ref, pm_ref, ocs, kprev, vprev, layer=layer, pad=pad, j=j, cur=cur, r0=r0)

    def ret(j, r0, cos_ref, sin_ref):
        _mix_ret(pm_ref, cos_ref, sin_ref, gn_ref, ocs, state, dec_t, cross_t, kdec_t, pad=pad, j=j, cur=cur, r0=r0)

    dense[0]()
    lru(j0, 0)
    snap_x[...] = xext[0:8, :]
    snap_h[...] = hcar[...]
    dense[1]()
    swa(j0, 0)
    dense[2]()
    ret(j0, 0, cos0_ref, sin0_ref)
    snap_s[...] = state[...]
    dense[3]()
    lru(j1, BLK)
    dense[4]()
    swa(j1, BLK)
    dense[5]()
    ret(j1, BLK, cos1_ref, sin1_ref)
    dense[6]()

    def write_state(r0, conv, hl, st):
        klast_ref[0] = pm_ref[r0:r0 + BLK, C_KS:C_KS + 128]
        vlast_ref[0] = pm_ref[r0:r0 + BLK, C_VS:C_VS + 128]
        conv_ref[0] = conv[...]
        hlast_ref[0] = hl[...]
        sret_ref[0] = st[...]

    @pl.when(j0 == nblk - 1)
    def _():
        write_state(0, snap_x, snap_h, snap_s)

    @pl.when(j1 == nblk - 1)
    def _():
        write_state(BLK, xext.at[0:8, :], hcar, state)


def _call_mix_out(pm, cos, sin, x1, pg, p, layer, bsz, pad, final):
    n = pm.shape[0]
    nb = n // BLK
    nblk = nb // bsz
    assert nb % 2 == 0 and nblk >= 2
    steps = nb // 2
    rows = 2 * BLK

    def lspec(shape):
        nd = len(shape)
        return pl.BlockSpec((None,) + tuple(shape[1:]), lambda s: (layer,) + (0,) * (nd - 1))

    def cur(width):
        return pl.BlockSpec((rows, width), lambda s: (jnp.minimum(s, steps - 1), 0))

    def prev(width):
        return pl.BlockSpec((rows, width), lambda s: (jnp.maximum(s - 1, 0), 0))

    def last(shape):
        return pl.BlockSpec((1,) + shape,
                            lambda s: (jnp.minimum(2 * s, nb - 1) // nblk,) + (0,) * len(shape))

    def tab(half):
        return pl.BlockSpec((BLK, 128), lambda s: (lax.rem(2 * s + half, nblk), 0))

    mix_params = ('conv_w', 'conv_b', 'lru_wa_bd', 'lru_wx_bd', 'lru_b_a', 'lru_b_x', 'lru_lambda', 'ret_norm')
    return pl.pallas_call(
        functools.partial(_mixout_body, layer=layer, pad=pad, nblk=nblk, final=final),
        grid=(steps + 1,),
        in_specs=[pl.BlockSpec(memory_space=pltpu.SMEM), cur(N_MIX), tab(0), tab(0), tab(1), tab(1),
                  prev(D_MODEL), prev(N_GATE)]
        + [lspec(p[k].shape) for k in mix_params]
        + [_layer_spec(p[k].shape, layer) for k in _OUT_WEIGHTS]
        + [pl.BlockSpec((1, D_MODEL), lambda s: (0, 0))],
        out_specs=[prev(D_MODEL), last((BLK, 128)), last((BLK, 128)), last((8, LRU_WIDTH)), last((8, LRU_WIDTH)),
                   last((RET_HEADS * RET_DK, RET_DV))],
        out_shape=[jax.ShapeDtypeStruct((n, D_MODEL), F32),
                   jax.ShapeDtypeStruct((bsz, BLK, 128), F32), jax.ShapeDtypeStruct((bsz, BLK, 128), F32),
                   jax.ShapeDtypeStruct((bsz, 8, LRU_WIDTH), F32), jax.ShapeDtypeStruct((bsz, 8, LRU_WIDTH), F32),
                   jax.ShapeDtypeStruct((bsz, RET_HEADS * RET_DK, RET_DV), F32)],
        scratch_shapes=[pltpu.VMEM((BLK + 8, LRU_WIDTH), F32), pltpu.VMEM((2, rows, N_OC), BF16),
                        pltpu.VMEM((8, LRU_WIDTH), F32),
                        pltpu.VMEM((SWA_KV_HEADS, BLK, 128), BF16), pltpu.VMEM((SWA_KV_HEADS, BLK, 128), BF16),
                        pltpu.VMEM((RET_HEADS * RET_DK, RET_DV), F32),
                        pltpu.VMEM((RET_HEADS, BLK, BLK), F32), pltpu.VMEM((RET_HEADS, BLK, 128), F32),
                        pltpu.VMEM((2, BLK, 128), F32),
                        pltpu.VMEM((8, LRU_WIDTH), F32), pltpu.VMEM((8, LRU_WIDTH), F32),
                        pltpu.VMEM((RET_HEADS * RET_DK, RET_DV), F32)],
        compiler_params=pltpu.CompilerParams(dimension_semantics=("arbitrary",), vmem_limit_bytes=VMEM_LIMIT),
        name='mix_out',
    )(p['swa_sinks'], pm, cos, sin, cos, sin, x1, pg, *[p[k] for k in mix_params],
      *[p[k] for k in _OUT_WEIGHTS], p['final_norm'])


def _mixs_body(pm_ref, ck_ref, cv_ref, conv_ref, h0_ref, s_ref, cw_ref, cb_ref, wa_ref, wx_ref, ba_ref, bx_ref,
               lam_ref, inv_ref, gn_ref, sk_ref,
               oc_ref, nk_ref, nv_ref, nconv_ref, nh_ref, ns_ref,
               qb, o8, qr_s, kr_s, v4_s, o2_s):
    gsz = G_SEQ
    nrow = 16
    rs = slice(None)

    @pl.when(pl.program_id(0) == 0)
    def _():
        qb[...] = jnp.zeros_like(qb)
        qr_s[...] = jnp.zeros_like(qr_s)
        kr_s[...] = jnp.zeros_like(kr_s)
        v4_s[...] = jnp.zeros_like(v4_s)

    lane = lax.broadcasted_iota(jnp.int32, (1, 128), 1)
    lo = lane < 64

    xa = pm_ref[rs,C_XA:C_XA + LRU_WIDTH]
    h1 = conv_ref[:, LRU_WIDTH:2 * LRU_WIDTH]
    h2 = conv_ref[:, 2 * LRU_WIDTH:3 * LRU_WIDTH]
    xc = cb_ref[...] + conv_ref[:, 0:LRU_WIDTH] * cw_ref[0:1, :]
    xc = xc + h1 * cw_ref[1:2, :]
    xc = xc + h2 * cw_ref[2:3, :]
    xc = xc + xa * cw_ref[3:4, :]
    nconv_ref[:, 0:LRU_WIDTH] = h1
    nconv_ref[:, LRU_WIDTH:2 * LRU_WIDTH] = h2
    nconv_ref[:, 2 * LRU_WIDTH:3 * LRU_WIDTH] = xa
    a, bt = _lru_gates(xc, wa_ref, wx_ref, ba_ref, bx_ref, lam_ref)
    hn = bt + a * h0_ref[...]
    nh_ref[...] = hn
    oc_ref[:, 0:LRU_WIDTH] = (hn * jax.nn.gelu(pm_ref[rs,C_YA:C_YA + LRU_WIDTH])).astype(BF16)

    nk_ref[:, 0:WINDOW - 1, :] = ck_ref[:, 1:WINDOW, :]
    nv_ref[:, 0:WINDOW - 1, :] = cv_ref[:, 1:WINDOW, :]
    for b in range(gsz):
        nk_ref[b, WINDOW - 1:WINDOW, :] = pm_ref[b:b + 1, C_KS:C_KS + 128]
        nv_ref[b, WINDOW - 1:WINDOW, :] = pm_ref[b:b + 1, C_VS:C_VS + 128]

    for r in range(SWA_HEADS):
        h = r // SWA_GROUP
        slab = pm_ref[rs,C_QS + (r // 2) * 128:C_QS + (r // 2 + 1) * 128]
        if r % 2 != h:
            slab = pltpu.roll(slab, 64, 1)
        qb[r * gsz:(r + 1) * gsz, :] = jnp.where(lo if h == 0 else jnp.logical_not(lo), slab, 0.0)

    ang = float(PAST_LEN) * inv_ref[...]
    cos2, sin_signed, first_half = _rope_operands(jnp.cos(ang), jnp.sin(ang))
    qc = _rope(pm_ref[rs,C_QR:C_QR + 256], cos2, sin_signed, first_half)
    kc = _rope(pm_ref[rs,C_KR:C_KR + 256], cos2, sin_signed, first_half) * RET_DK ** -0.5
    lane256 = lax.broadcasted_iota(jnp.int32, (1, RET_HEADS * RET_DK), 1)
    for r in range(RET_HEADS):
        hm = (lane256 >= r * RET_DK) & (lane256 < (r + 1) * RET_DK)
        qm = jnp.where(hm, qc, 0.0)
        km = jnp.where(hm, kc, 0.0)
        for c in range(2):
            qr_s[c, r * gsz:(r + 1) * gsz, :] = qm[:, c * 128:(c + 1) * 128]
            kr_s[c, r * gsz:(r + 1) * gsz, :] = km[:, c * 128:(c + 1) * 128]
        v4_s[r * gsz:(r + 1) * gsz, :] = pm_ref[rs,C_VR + r * RET_DV:C_VR + (r + 1) * RET_DV]

    srow = lax.broadcasted_iota(jnp.int32, (RET_HEADS * RET_DK, 1), 0)
    gcol = jnp.full((RET_HEADS * RET_DK, 1), math.exp(LOG_G[0]), F32)
    for r in range(1, RET_HEADS):
        gcol = jnp.where(srow >= r * RET_DK, math.exp(LOG_G[r]), gcol)
    sk = sk_ref[:, 0:1]

    def per_seq(b, carry):
        rows = pl.ds(b, nrow, stride=gsz)
        s = _dot_nt(qb[rows, :].astype(BF16), nk_ref[b].astype(BF16)) * SWA_HEAD_DIM ** -0.5
        m = jnp.maximum(jnp.max(s, axis=-1, keepdims=True), sk)
        e = jnp.exp(s - m)
        den = jnp.sum(e, axis=-1, keepdims=True) + jnp.exp(sk - m)
        o8[rows, :] = _dot(e.astype(BF16), nv_ref[b].astype(BF16)) / den
        sb = s_ref[b]
        sbb = sb.astype(BF16)
        o2_s[rows, :] = (_dot(qr_s[0, rows, :].astype(BF16), sbb[0:128, :])
                         + _dot(qr_s[1, rows, :].astype(BF16), sbb[128:256, :]))
        v4 = v4_s[rows, :].astype(BF16)
        kv = jnp.concatenate([_dot_tn(kr_s[0, rows, :].astype(BF16), v4),
                              _dot_tn(kr_s[1, rows, :].astype(BF16), v4)], axis=0)
        ns_ref[b] = gcol * sb + kv
        return carry

    lax.fori_loop(0, gsz, per_seq, 0)

    for sl in range(4):
        h = sl // 2
        ev = o8[(2 * sl) * gsz:(2 * sl + 1) * gsz, :]
        od = o8[(2 * sl + 1) * gsz:(2 * sl + 2) * gsz, :]
        if h != 0:
            ev = pltpu.roll(ev, 64, 1)
        if h != 1:
            od = pltpu.roll(od, 64, 1)
        oc_ref[:, LRU_WIDTH + sl * 128:LRU_WIDTH + (sl + 1) * 128] = jnp.where(lo, ev, od).astype(BF16)

    prod = qc * kc
    p_hi = prod.astype(BF16)
    p_lo = (prod - p_hi.astype(F32)).astype(BF16)
    er = lax.broadcasted_iota(jnp.int32, (RET_HEADS * RET_DK, RET_HEADS * RET_DV), 0) // RET_DK
    ec = lax.broadcasted_iota(jnp.int32, (RET_HEADS * RET_DK, RET_HEADS * RET_DV), 1) // RET_DV
    expand = jnp.where(er == ec, 1.0, 0.0).astype(BF16)
    qk = _dot(p_hi, expand) + _dot(p_lo, expand)
    for r in range(RET_HEADS):
        cs = slice(r * RET_DV, (r + 1) * RET_DV)
        o = qk[:, cs] * pm_ref[rs,C_VR + r * RET_DV:C_VR + (r + 1) * RET_DV]
        o = o + o2_s[r * gsz:(r + 1) * gsz, :] * math.exp(LOG_G[r])
        oc = _group_norm_gate(o, gn_ref[:, cs], pm_ref[rs,C_GR + r * RET_DV:C_GR + (r + 1) * RET_DV])
        oc_ref[:, 2 * LRU_WIDTH + r * RET_DV:2 * LRU_WIDTH + (r + 1) * RET_DV] = oc.astype(BF16)


def _call_mix_sample(pm, ck, cv, conv, h0, sret, p, layer):
    nseq = pm.shape[0]
    gsz = G_SEQ

    def lspec(shape):
        nd = len(shape)
        return pl.BlockSpec((None,) + tuple(shape[1:]), lambda i: (layer,) + (0,) * (nd - 1))

    def seq2(width):
        return pl.BlockSpec((gsz, width), lambda i: (i, 0))

    def seq3(layered, d1, d2):
        if layered:
            return pl.BlockSpec((None, gsz, d1, d2), lambda i: (layer, i, 0, 0))
        return pl.BlockSpec((gsz, d1, d2), lambda i: (i, 0, 0))

    sdim = RET_HEADS * RET_DK
    return pl.pallas_call(
        _mixs_body,
        grid=(nseq // gsz,),
        in_specs=[seq2(N_MIX), seq3(True, WINDOW, 128), seq3(True, WINDOW, 128),
                  pl.BlockSpec((None, gsz, 3 * LRU_WIDTH), lambda i: (layer, i, 0)),
                  pl.BlockSpec((None, gsz, LRU_WIDTH), lambda i: (layer, i, 0)),
                  seq3(True, sdim, RET_DV),
                  lspec(p['conv_w'].shape), lspec(p['conv_b'].shape), lspec(p['lru_wa_bd'].shape),
                  lspec(p['lru_wx_bd'].shape), lspec(p['lru_b_a'].shape), lspec(p['lru_b_x'].shape),
                  lspec(p['lru_lambda'].shape), pl.BlockSpec((1, 128), lambda i: (0, 0)),
                  lspec(p['ret_norm'].shape), lspec(p['sinks16'].shape)],
        out_specs=[seq2(N_OC), seq3(False, WINDOW, 128), seq3(False, WINDOW, 128), seq2(3 * LRU_WIDTH),
                   seq2(LRU_WIDTH), seq3(False, sdim, RET_DV)],
        out_shape=[jax.ShapeDtypeStruct((nseq, N_OC), BF16),
                   jax.ShapeDtypeStruct((nseq, WINDOW, 128), F32), jax.ShapeDtypeStruct((nseq, WINDOW, 128), F32),
                   jax.ShapeDtypeStruct((nseq, 3 * LRU_WIDTH), F32), jax.ShapeDtypeStruct((nseq, LRU_WIDTH), F32),
                   jax.ShapeDtypeStruct((nseq, sdim, RET_DV), F32)],
        scratch_shapes=[pltpu.VMEM((16 * gsz, 128), F32), pltpu.VMEM((16 * gsz, 128), F32),
                        pltpu.VMEM((2, 16 * gsz, 128), F32), pltpu.VMEM((2, 16 * gsz, 128), F32),
                        pltpu.VMEM((16 * gsz, 128), F32), pltpu.VMEM((16 * gsz, 128), F32)],
        compiler_params=pltpu.CompilerParams(dimension_semantics=("arbitrary",), vmem_limit_bytes=VMEM_LIMIT),
        name='mix_sample',
    )(pm, ck, cv, conv, h0, sret, p['conv_w'], p['conv_b'], p['lru_wa_bd'], p['lru_wx_bd'], p['lru_b_a'],
      p['lru_b_x'], p['lru_lambda'], p['rope_inv'], p['ret_norm'], p['sinks16'])


def _block_diag(w):
    depth = w.shape[0]
    w = w.reshape(depth, 2, 4, LRU_BW, LRU_BW)
    eye = jnp.eye(4, dtype=w.dtype)
    return jnp.einsum('lsncd,nm->lsncmd', w, eye).reshape(depth, 2, 4 * LRU_BW, 4 * LRU_BW)


def kernel(x_prompt, x_sample, cache_swa_k, cache_swa_v, state_conv, state_lru, state_ret, meta_tokens, ffn1_norm,
           ffn1_w_gu, ffn1_w_down, mix_norm, w_in, conv_w, conv_b, lru_w_a, lru_b_a, lru_w_x, lru_b_x, lru_lambda,
           swa_sinks, ret_norm, w_branch_a, w_branch_b, w_branch_c, w_out, ffn2_norm, ffn2_w_gu, ffn2_w_down,
           final_norm):
    depth = w_in.shape[0]
    bsz, seq, _ = x_prompt.shape
    nseq = x_sample.shape[0]
    buf = cache_swa_k.shape[2]
    assert buf == WINDOW == BLK and x_sample.shape[1] == 1 and nseq % G_SEQ == 0
    t = seq + N_META
    pad = (-t) % BLK
    tp = t + pad
    assert (bsz * tp) % TM_DENSE == 0

    def row(v):
        return v.reshape(depth, 1, -1).astype(F32)

    half = jnp.arange(128) % (RET_DK // 2)
    p = {
        'ffn1_norm': row(ffn1_norm), 'ffn1_w_gu': ffn1_w_gu.astype(BF16), 'ffn1_w_down': ffn1_w_down.astype(BF16),
        'mix_norm': row(mix_norm), 'w_mix': w_in[:, :, :N_MIX].astype(BF16), 'w_gate': w_in[:, :, N_MIX:].astype(BF16),
        'conv_w': conv_w.astype(F32), 'conv_b': row(conv_b),
        'lru_wa_bd': _block_diag(lru_w_a).astype(BF16), 'lru_wx_bd': _block_diag(lru_w_x).astype(BF16),
        'lru_b_a': row(lru_b_a), 'lru_b_x': row(lru_b_x), 'lru_lambda': row(lru_lambda),
        'swa_sinks': swa_sinks.astype(F32),
        'sinks16': jnp.pad(jnp.broadcast_to(swa_sinks.astype(F32)[:, :, None], (depth, SWA_HEADS, 128)),
                           ((0, 0), (0, 16 - SWA_HEADS), (0, 0))),
        'ret_norm': row(ret_norm),
        'w_branch_a': w_branch_a.astype(BF16), 'w_branch_b': w_branch_b.astype(BF16),
        'w_branch_c': w_branch_c.astype(BF16), 'w_out': w_out.astype(BF16),
        'ffn2_norm': row(ffn2_norm), 'ffn2_w_gu': ffn2_w_gu.astype(BF16), 'ffn2_w_down': ffn2_w_down.astype(BF16),
        'final_norm': final_norm.reshape(1, D_MODEL).astype(F32),
        'rope_inv': (ROPE_BASE ** (-half.astype(F32) / (RET_DK // 2))).reshape(1, 128),
    }

    meta = jnp.broadcast_to(meta_tokens.astype(F32)[None], (bsz, N_META, D_MODEL))
    xp = jnp.concatenate([jnp.zeros((bsz, pad, D_MODEL), F32), meta, x_prompt], axis=1).reshape(bsz * tp, D_MODEL)
    xs = x_sample.reshape(nseq, D_MODEL)
    ck = cache_swa_k.reshape(depth, nseq, buf, 128)
    cv = cache_swa_v.reshape(depth, nseq, buf, 128)
    conv = state_conv.reshape(depth, nseq, 3 * LRU_WIDTH)
    sret = state_ret.reshape(depth, nseq, RET_HEADS * RET_DK, RET_DV)
    cos, sin = _call_ropetab(p['rope_inv'], tp, pad)

    outs_p = [[] for _ in range(5)]
    outs_s = [[] for _ in range(5)]
    for layer in range(depth):
        final = layer == depth - 1
        x1, pm, pg = _call_in(xp, p, layer, TM_DENSE)
        xp, kl, vl, cl, hl, sl = _call_mix_out(pm, cos, sin, x1, pg, p, layer, bsz, pad, final)
        for acc, o in zip(outs_p, (kl.reshape(bsz, buf, SWA_KV_HEADS, SWA_HEAD_DIM),
                                   vl.reshape(bsz, buf, SWA_KV_HEADS, SWA_HEAD_DIM),
                                   cl[:, 8 - (CONV_WIDTH - 1):, :], hl[:, 7, :],
                                   sl.reshape(bsz, RET_HEADS, RET_DK, RET_DV))):
            acc.append(o)

        x1, pm, pg = _call_in(xs, p, layer, nseq)
        oc, nk, nv, nc, nh, ns = _call_mix_sample(pm, ck, cv, conv, state_lru, sret, p, layer)
        xs = _call_out(x1, pg, oc, p, layer, nseq, final)
        for acc, o in zip(outs_s, (nk.reshape(nseq, buf, SWA_KV_HEADS, SWA_HEAD_DIM),
                                   nv.reshape(nseq, buf, SWA_KV_HEADS, SWA_HEAD_DIM),
                                   nc.reshape(nseq, CONV_WIDTH - 1, LRU_WIDTH), nh,
                                   ns.reshape(nseq, RET_HEADS, RET_DK, RET_DV))):
            acc.append(o)

    yp = xp.reshape(bsz, tp, D_MODEL)[:, pad + N_META:]
    ys = xs.reshape(nseq, 1, D_MODEL)
    return (yp, ys) + tuple(jnp.stack(a) for a in outs_p) + tuple(jnp.stack(a) for a in outs_s)
```

```python
import functools
import math

import jax
import jax.numpy as jnp
from jax import lax
from jax.experimental import pallas as pl
from jax.experimental.pallas import tpu as pltpu

F32 = jnp.float32
BF16 = jnp.bfloat16

D_MODEL = 1024
D_FF = 2048
N_META = 16
EPS = 1e-6
LRU_WIDTH = 512
LRU_BLOCKS = 8
LRU_BW = 64
CONV_WIDTH = 4
LRU_C = 8.0
SWA_HEAD_DIM = 64
SWA_HEADS = 8
SWA_KV_HEADS = 2
SWA_GROUP = 4
WINDOW = 128
RET_DK = 64
RET_DV = 128
RET_HEADS = 4
ROPE_BASE = 10000.0
GN_EPS = 1e-5
PAST_LEN = 8192

BLK = 128
N_MIX = 3328
N_GATE = 3 * D_MODEL
C_XA, C_YA, C_QS, C_KS, C_VS, C_QR, C_KR, C_VR, C_GR = 0, 512, 1024, 1536, 1664, 1792, 2048, 2304, 2816
N_OC = 3 * LRU_WIDTH
LOG_G = tuple(math.log1p(-(2.0 ** (-5.0 - h))) for h in range(RET_HEADS))

TM_DENSE = 256
G_SEQ = 16
VMEM_LIMIT = 56 * 1024 * 1024


def _dot(a, b):
    return jnp.dot(a, b, preferred_element_type=F32)


def _dot_nt(a, b):
    return lax.dot_general(a, b, (((1,), (1,)), ((), ())), preferred_element_type=F32)


def _dot_tn(a, b):
    return lax.dot_general(a, b, (((0,), (0,)), ((), ())), preferred_element_type=F32)


def _rms(x, g):
    return x * lax.rsqrt(jnp.mean(x * x, axis=-1, keepdims=True) + EPS) * g


def _softplus(x):
    return jnp.maximum(x, 0.0) + jnp.log1p(jnp.exp(-jnp.abs(x)))


def _wt(ref, rows=None, cols=None):
    r = slice(None) if rows is None else slice(rows[0] // 2, rows[1] // 2)
    c = slice(None) if cols is None else slice(*cols)
    return pltpu.bitcast(ref[r, c], BF16)


def _swiglu(u, wgu_ref, wd_ref):
    gu = _dot(u, _wt(wgu_ref))
    act = (jax.nn.silu(gu[:, :D_FF]) * gu[:, D_FF:]).astype(BF16)
    return _dot(act, _wt(wd_ref))


def _in_body(x_ref, n1_ref, wgu_ref, wd_ref, n2_ref, wm_ref, wg_ref, x1_ref, pm_ref, pg_ref):
    x = x_ref[...]
    x1 = x + 0.5 * _swiglu(_rms(x, n1_ref[...]).astype(BF16), wgu_ref, wd_ref)
    x1_ref[...] = x1
    u2 = _rms(x1, n2_ref[...]).astype(BF16)
    pm_ref[...] = _dot(u2, _wt(wm_ref))
    pg_ref[...] = _dot(u2, _wt(wg_ref))


def _merge_out(x1, pg, oc, wa_ref, wb_ref, wc_ref, wo_ref, n_ref, wgu_ref, wd_ref, fn_ref, final):
    g = jax.nn.sigmoid(pg)
    merged = (g[:, :D_MODEL] * _dot(oc[:, :LRU_WIDTH], _wt(wa_ref))
              + g[:, D_MODEL:2 * D_MODEL] * _dot(oc[:, LRU_WIDTH:2 * LRU_WIDTH], _wt(wb_ref))
              + g[:, 2 * D_MODEL:] * _dot(oc[:, 2 * LRU_WIDTH:], _wt(wc_ref)))
    x2 = x1 + _dot(merged.astype(BF16), _wt(wo_ref))
    x3 = x2 + 0.5 * _swiglu(_rms(x2, n_ref[...]).astype(BF16), wgu_ref, wd_ref)
    if final:
        x3 = _rms(x3, fn_ref[...])
    return x3


FF_CHUNK = 256


def _merge_out_pieces(x1_ref, pg_ref, oc, wa_ref, wb_ref, wc_ref, wo_ref, n_ref, wgu_ref, wd_ref, fn_ref, out_ref,
                      final):
    merged = None
    for b, w_ref in enumerate((wa_ref, wb_ref, wc_ref)):
        for c in range(0, D_MODEL, 2 * FF_CHUNK):
            g = jax.nn.sigmoid(pg_ref[:, b * D_MODEL + c:b * D_MODEL + c + 2 * FF_CHUNK])
            part = g * _dot(oc[:, b * LRU_WIDTH:(b + 1) * LRU_WIDTH], _wt(w_ref, cols=(c, c + 2 * FF_CHUNK)))
            if b == 0:
                merged = [part] if c == 0 else merged + [part]
            else:
                merged[c // (2 * FF_CHUNK)] = merged[c // (2 * FF_CHUNK)] + part
            yield
    merged = jnp.concatenate(merged, axis=1).astype(BF16)
    x2 = []
    for c in range(0, D_MODEL, FF_CHUNK):
        x2.append(x1_ref[:, c:c + FF_CHUNK] + _dot(merged, _wt(wo_ref, cols=(c, c + FF_CHUNK))))
        yield
    x2 = jnp.concatenate(x2, axis=1)
    u = _rms(x2, n_ref[...]).astype(BF16)
    yield
    y = None
    for c in range(D_FF // FF_CHUNK):
        lo, hi = c * FF_CHUNK, (c + 1) * FF_CHUNK
        gate = _dot(u, _wt(wgu_ref, cols=(lo, hi)))
        yield
        up = _dot(u, _wt(wgu_ref, cols=(D_FF + lo, D_FF + hi)))
        yield
        part = _dot((jax.nn.silu(gate) * up).astype(BF16), _wt(wd_ref, rows=(lo, hi)))
        y = part if y is None else y + part
        yield
    x3 = x2 + 0.5 * y
    out_ref[...] = _rms(x3, fn_ref[...]) if final else x3
    yield


def _alternate(first, second, ratio):
    live = [True, True]
    while any(live):
        for idx, (gen, count) in enumerate(((first, 1), (second, ratio))):
            for _ in range(count):
                if live[idx] and next(gen, StopIteration) is StopIteration:
                    live[idx] = False


def _out_body(x1_ref, pg_ref, oc_ref, wa_ref, wb_ref, wc_ref, wo_ref, n_ref, wgu_ref, wd_ref, fn_ref, out_ref,
              *, final):
    out_ref[...] = _merge_out(x1_ref[...], pg_ref[...], oc_ref[...], wa_ref, wb_ref, wc_ref, wo_ref, n_ref,
                              wgu_ref, wd_ref, fn_ref, final)


def _layer_spec(shape, layer):
    nd = len(shape)
    return pl.BlockSpec((None,) + tuple(shape[1:]), lambda *_: (layer,) + (0,) * (nd - 1),
                        pipeline_mode=pl.Buffered(1))


def _row_spec(tm, width):
    return pl.BlockSpec((tm, width), lambda i: (i, 0))


def _dense_params():
    return pltpu.CompilerParams(dimension_semantics=("arbitrary",), vmem_limit_bytes=VMEM_LIMIT)


def _call_in(x, p, layer, tm):
    n = x.shape[0]
    return pl.pallas_call(
        _in_body,
        grid=(n // tm,),
        in_specs=[_row_spec(tm, D_MODEL),
                  _layer_spec(p['ffn1_norm'].shape, layer), _layer_spec(p['ffn1_w_gu'].shape, layer),
                  _layer_spec(p['ffn1_w_down'].shape, layer), _layer_spec(p['mix_norm'].shape, layer),
                  _layer_spec(p['w_mix'].shape, layer), _layer_spec(p['w_gate'].shape, layer)],
        out_specs=[_row_spec(tm, D_MODEL), _row_spec(tm, N_MIX), _row_spec(tm, N_GATE)],
        out_shape=[jax.ShapeDtypeStruct((n, D_MODEL), F32), jax.ShapeDtypeStruct((n, N_MIX), F32),
                   jax.ShapeDtypeStruct((n, N_GATE), F32)],
        compiler_params=_dense_params(),
        name='layer_in',
    )(x, p['ffn1_norm'], p['ffn1_w_gu'], p['ffn1_w_down'], p['mix_norm'], p['w_mix'], p['w_gate'])


_OUT_WEIGHTS = ('w_branch_a', 'w_branch_b', 'w_branch_c', 'w_out', 'ffn2_norm', 'ffn2_w_gu', 'ffn2_w_down')


def _call_out(x1, pg, oc, p, layer, tm, final):
    n = x1.shape[0]
    return pl.pallas_call(
        functools.partial(_out_body, final=final),
        grid=(n // tm,),
        in_specs=[_row_spec(tm, D_MODEL), _row_spec(tm, N_GATE), _row_spec(tm, N_OC)]
        + [_layer_spec(p[k].shape, layer) for k in _OUT_WEIGHTS]
        + [pl.BlockSpec((1, D_MODEL), lambda i: (0, 0))],
        out_specs=_row_spec(tm, D_MODEL),
        out_shape=jax.ShapeDtypeStruct((n, D_MODEL), F32),
        compiler_params=_dense_params(),
        name='layer_out',
    )(x1, pg, oc, *[p[k] for k in _OUT_WEIGHTS], p['final_norm'])


def _drain(pieces):
    while True:
        try:
            next(pieces)
        except StopIteration as done:
            return done.value


def _lru_gate_pieces(xc, wa_ref, wx_ref, ba_ref, bx_ref, lam_ref):
    xcb = xc.astype(BF16)
    half = LRU_WIDTH // 2
    rpre = jnp.concatenate([_dot(xcb[:, :half], wa_ref[0]), _dot(xcb[:, half:], wa_ref[1])], axis=1)
    yield
    ipre = jnp.concatenate([_dot(xcb[:, :half], wx_ref[0]), _dot(xcb[:, half:], wx_ref[1])], axis=1)
    yield
    r = jax.nn.sigmoid(rpre + ba_ref[...])
    i = jax.nn.sigmoid(ipre + bx_ref[...])
    yield
    log_a = -LRU_C * r * _softplus(-lam_ref[...])
    a = jnp.exp(log_a)
    yield
    return a, jnp.sqrt(-jnp.tanh(log_a) * (a * a + 1.0)) * (i * xc)


def _lru_gates(*args):
    return _drain(_lru_gate_pieces(*args))


def _rope_operands(cos, sin):
    lane = lax.broadcasted_iota(jnp.int32, (1, 4 * RET_DK), 1)
    first_half = (lane & (RET_DK - 1)) < RET_DK // 2
    cos2 = jnp.concatenate([cos, cos], axis=1)
    sin2 = jnp.concatenate([sin, sin], axis=1)
    return cos2, jnp.where(first_half, -sin2, sin2), first_half


def _rope(x, cos2, sin_signed, first_half):
    swapped = jnp.where(first_half, pltpu.roll(x, 4 * RET_DK - RET_DK // 2, 1), pltpu.roll(x, RET_DK // 2, 1))
    return x * cos2 + swapped * sin_signed


def _group_norm_gate(o, gain, gr):
    mu = jnp.mean(o, axis=-1, keepdims=True)
    d = o - mu
    var = jnp.mean(d * d, axis=-1, keepdims=True)
    return d * lax.rsqrt(var + GN_EPS) * gain * jax.nn.silu(gr)


def _ropetab_body(inv_ref, cos_ref, sin_ref, *, rows_per_step, pad):
    rows = lax.broadcasted_iota(jnp.int32, (rows_per_step, 1), 0)
    pos = (pl.program_id(0) * rows_per_step + rows - pad).astype(F32)
    ang = pos * inv_ref[...]
    cos_ref[...] = jnp.cos(ang)
    sin_ref[...] = jnp.sin(ang)


def _call_ropetab(inv, tp, pad):
    nblk = tp // BLK
    rps = BLK * max(d for d in (8, 5, 4, 2, 1) if nblk % d == 0)
    spec = pl.BlockSpec((rps, 128), lambda i: (i, 0))
    return pl.pallas_call(
        functools.partial(_ropetab_body, rows_per_step=rps, pad=pad),
        grid=(tp // rps,),
        in_specs=[pl.BlockSpec((1, 128), lambda i: (0, 0))],
        out_specs=[spec, spec],
        out_shape=[jax.ShapeDtypeStruct((tp, 128), F32), jax.ShapeDtypeStruct((tp, 128), F32)],
        name='rope_tables',
    )(inv)


def _block_masks(j, pad):
    lo = lax.broadcasted_iota(jnp.int32, (1, 128), 1) < 64
    rows = lax.broadcasted_iota(jnp.int32, (BLK, 1), 0)
    return lo, (j * BLK + rows) >= pad


def _mix_lru(pm_ref, cw_ref, cb_ref, wa_ref, wx_ref, ba_ref, bx_ref, lam_ref, xext, ocs, hcar, *, pad, j, cur, r0):
    rs = slice(r0, r0 + BLK)
    fresh = j == 0
    _, valid = _block_masks(j, pad)
    xa = jnp.where(valid, pm_ref[rs,C_XA:C_XA + LRU_WIDTH], 0.0)
    xext[8:8 + BLK, :] = xa
    xext[0:8, :] = jnp.where(fresh, 0.0, xext[0:8, :])
    yield
    xc = cb_ref[...] + xext[5:5 + BLK, :] * cw_ref[0:1, :]
    xc = xc + xext[6:6 + BLK, :] * cw_ref[1:2, :]
    yield
    xc = xc + xext[7:7 + BLK, :] * cw_ref[2:3, :]
    xc = xc + xa * cw_ref[3:4, :]
    xext[0:8, :] = xext[BLK:BLK + 8, :]
    yield
    a, bt = yield from _lru_gate_pieces(xc, wa_ref, wx_ref, ba_ref, bx_ref, lam_ref)
    bt = jnp.where(valid, bt, 0.0)
    yield
    row8 = lax.broadcasted_iota(jnp.int32, (8, 1), 0)
    h_in = jnp.where(fresh, 0.0, hcar[7:8, :])
    hs = []
    for g in range(BLK // 8):
        ag = a[g * 8:(g + 1) * 8, :]
        bg = bt[g * 8:(g + 1) * 8, :]
        for k in (1, 2, 4):
            keep = row8 >= k
            a_sh = jnp.where(keep, pltpu.roll(ag, k, 0), 1.0)
            b_sh = jnp.where(keep, pltpu.roll(bg, k, 0), 0.0)
            bg = bg + ag * b_sh
            ag = ag * a_sh
        hg = ag * h_in + bg
        h_in = hg[7:8, :]
        hs.append(hg)
        if g % 2 == 1:
            yield
    hcar[...] = hs[-1]
    o_a = jnp.concatenate(hs, axis=0) * jax.nn.gelu(pm_ref[rs,C_YA:C_YA + LRU_WIDTH])
    ocs[cur, rs,0:LRU_WIDTH] = o_a.astype(BF16)
    yield


def _mix_swa(sinks_ref, pm_ref, ocs, kprev, vprev, *, layer, pad, j, cur, r0):
    rs = slice(r0, r0 + BLK)
    lo, _ = _block_masks(j, pad)
    k = pm_ref[rs,C_KS:C_KS + 128]
    v = pm_ref[rs,C_VS:C_VS + 128]
    k_sw = pltpu.roll(k, 64, 1)
    v_sw = pltpu.roll(v, 64, 1)
    kdup = (jnp.where(lo, k, k_sw).astype(BF16), jnp.where(lo, k_sw, k).astype(BF16))
    vdup = (jnp.where(lo, v, v_sw).astype(BF16), jnp.where(lo, v_sw, v).astype(BF16))
    row4 = lax.broadcasted_iota(jnp.int32, (SWA_GROUP * BLK, 1), 0)
    t4 = row4 & (BLK - 1)
    col = lax.broadcasted_iota(jnp.int32, (1, 2 * BLK), 1)
    ok = ((j - 1) * BLK + col >= pad) & (col > t4) & (col <= t4 + BLK)
    for h in range(SWA_KV_HEADS):
        parts = []
        for g in range(SWA_GROUP):
            head = SWA_GROUP * h + g
            slab = pm_ref[rs,C_QS + (head // 2) * 128:C_QS + (head // 2 + 1) * 128] * SWA_HEAD_DIM ** -0.5
            parts.append(jnp.where(lo if head % 2 == 0 else jnp.logical_not(lo), slab, 0.0))
        qst = jnp.concatenate(parts, axis=0).astype(BF16)
        kcat = jnp.concatenate([kprev[h], kdup[h]], axis=0)
        vcat = jnp.concatenate([vprev[h], vdup[h]], axis=0)
        yield
        sc = jnp.where(ok, _dot_nt(qst, kcat), -jnp.inf)
        yield
        sk = jnp.full((SWA_GROUP * BLK, 1), sinks_ref[layer, SWA_GROUP * h], F32)
        for g in range(1, SWA_GROUP):
            sk = jnp.where(row4 >= g * BLK, sinks_ref[layer, SWA_GROUP * h + g], sk)
        m = jnp.maximum(jnp.max(sc, axis=-1, keepdims=True), sk)
        yield
        e = jnp.exp(sc - m)
        yield
        den = jnp.sum(e, axis=-1, keepdims=True) + jnp.exp(sk - m)
        yield
        o = _dot(e.astype(BF16), vcat) / den
        yield
        for sl in range(2):
            ge = 2 * sl
            slab = jnp.where(lo, o[ge * BLK:(ge + 1) * BLK, :], o[(ge + 1) * BLK:(ge + 2) * BLK, :])
            c0 = LRU_WIDTH + (2 * h + sl) * 128
            ocs[cur, rs,c0:c0 + 128] = slab.astype(BF16)
        kprev[h] = kdup[h]
        vprev[h] = vdup[h]
        yield


def _mix_ret(pm_ref, cos_ref, sin_ref, gn_ref, ocs, state, dec_t, cross_t, kdec_t, *, pad, j, cur, r0):
    rs = slice(r0, r0 + BLK)
    fresh = j == 0
    lo, valid = _block_masks(j, pad)
    cos2, sin_signed, first_half = _rope_operands(cos_ref[...], sin_ref[...])
    qc = _rope(pm_ref[rs,C_QR:C_QR + 256], cos2, sin_signed, first_half)
    yield
    kc = _rope(pm_ref[rs,C_KR:C_KR + 256], cos2, sin_signed, first_half) * RET_DK ** -0.5
    kc = jnp.where(valid, kc, 0.0)
    yield
    st = [jnp.where(fresh, 0.0, state[sl * 128:(sl + 1) * 128, :]) for sl in range(2)]
    yield
    upd = [None, None]
    for h in range(RET_HEADS):
        sl = h // 2
        half = lo if h % 2 == 0 else jnp.logical_not(lo)
        qm = jnp.where(half, qc[:, sl * 128:(sl + 1) * 128], 0.0).astype(BF16)
        kslab = kc[:, sl * 128:(sl + 1) * 128]
        vh = jnp.where(valid, pm_ref[rs,C_VR + h * RET_DV:C_VR + (h + 1) * RET_DV], 0.0).astype(BF16)
        sc = (_dot_nt(qm, kslab.astype(BF16)) * dec_t[h]).astype(BF16)
        yield
        o = _dot(sc, vh)
        o = o + _dot(qm, st[sl].astype(BF16)) * cross_t[h]
        yield
        c0 = 2 * LRU_WIDTH + h * RET_DV
        oc = _group_norm_gate(o, gn_ref[:, h * RET_DV:(h + 1) * RET_DV],
                              pm_ref[rs,C_GR + h * RET_DV:C_GR + (h + 1) * RET_DV])
        ocs[cur, rs,c0:c0 + RET_DV] = oc.astype(BF16)
        yield
        km = jnp.where(half, kslab * kdec_t[sl], 0.0).astype(BF16)
        u = _dot_tn(km, vh)
        upd[sl] = u if upd[sl] is None else upd[sl] + u
        yield
    srow = lax.broadcasted_iota(jnp.int32, (128, 1), 0)
    for sl in range(2):
        gcol = jnp.where(srow < RET_DK, math.exp(BLK * LOG_G[2 * sl]), math.exp(BLK * LOG_G[2 * sl + 1]))
        state[sl * 128:(sl + 1) * 128, :] = gcol * st[sl] + upd[sl]


def _mixout_body(sinks_ref, pm_ref, cos0_ref, sin0_ref, cos1_ref, sin1_ref, x1_ref, pg_ref,
                 cw_ref, cb_ref, wa_ref, wx_ref, ba_ref, bx_ref, lam_ref, gn_ref,
                 wba_ref, wbb_ref, wbc_ref, wo_ref, n2_ref, wgu_ref, wd_ref, fn_ref,
                 out_ref, klast_ref, vlast_ref, conv_ref, hlast_ref, sret_ref,
                 xext, ocs, hcar, kprev, vprev, state, dec_t, cross_t, kdec_t, snap_x, snap_h, snap_s,
                 *, layer, pad, nblk, final):
    s = pl.program_id(0)

    @pl.when(s == 0)
    def _():
        lo = lax.broadcasted_iota(jnp.int32, (1, 128), 1) < 64
        ti = lax.broadcasted_iota(jnp.int32, (BLK, 1), 0).astype(F32)
        tj = lax.broadcasted_iota(jnp.int32, (1, BLK), 1).astype(F32)
        diff = ti - tj
        for h in range(RET_HEADS):
            dec_t[h] = jnp.exp(jnp.where(diff >= 0, diff * LOG_G[h], -jnp.inf))
            cross_t[h] = jnp.broadcast_to(jnp.exp((ti + 1.0) * LOG_G[h]), (BLK, 128))
        for sl in range(2):
            kdec_t[sl] = jnp.where(lo, jnp.exp((BLK - 1.0 - ti) * LOG_G[2 * sl]),
                                   jnp.exp((BLK - 1.0 - ti) * LOG_G[2 * sl + 1]))
        for ref in (ocs, xext, hcar, kprev, vprev, state):
            ref[...] = jnp.zeros_like(ref)

    cur = lax.rem(s, 2)
    j0 = lax.rem(2 * s, nblk)
    j1 = lax.rem(2 * s + 1, nblk)
    dense = _merge_out_pieces(x1_ref, pg_ref, ocs[lax.rem(s + 1, 2)], wba_ref, wbb_ref, wbc_ref, wo_ref, n2_ref,
                              wgu_ref, wd_ref, fn_ref, out_ref, final)

    def mixers():
        for j, r0, cos_ref, sin_ref in ((j0, 0, cos0_ref, sin0_ref), (j1, BLK, cos1_ref, sin1_ref)):
            yield from _mix_lru(pm_ref, cw_ref, cb_ref, wa_ref, wx_ref, ba_ref, bx_ref, lam_ref, xext, ocs, hcar,
                                pad=pad, j=j, cur=cur, r0=r0)
            yield from _mix_swa(sinks_ref, pm_ref, ocs, kprev, vprev, layer=layer, pad=pad, j=j, cur=cur, r0=r0)
            yield from _mix_ret(pm_ref, cos_ref, sin_ref, gn_ref, ocs, state, dec_t, cross_t, kdec_t,
                                pad=pad, j=j, cur=cur, r0=r0)
            if r0 == 0:
                snap_x[...] = xext[0:8, :]
                snap_h[...] = hcar[...]
                snap_s[...] = state[...]

    _alternate(dense, mixers(), 3)

    def write_state(r0, conv, hl, st):
        klast_ref[0] = pm_ref[r0:r0 + BLK, C_KS:C_KS + 128]
        vlast_ref[0] = pm_ref[r0:r0 + BLK, C_VS:C_VS + 128]
        conv_ref[0] = conv[...]
        hlast_ref[0] = hl[...]
        sret_ref[0] = st[...]

    @pl.when(j0 == nblk - 1)
    def _():
        write_state(0, snap_x, snap_h, snap_s)

    @pl.when(j1 == nblk - 1)
    def _():
        write_state(BLK, xext.at[0:8, :], hcar, state)


def _call_mix_out(pm, cos, sin, x1, pg, p, layer, bsz, pad, final):
    n = pm.shape[0]
    nb = n // BLK
    nblk = nb // bsz
    assert nb % 2 == 0 and nblk >= 2
    steps = nb // 2
    rows = 2 * BLK

    def lspec(shape):
        nd = len(shape)
        return pl.BlockSpec((None,) + tuple(shape[1:]), lambda s: (layer,) + (0,) * (nd - 1))

    def cur(width):
        return pl.BlockSpec((rows, width), lambda s: (jnp.minimum(s, steps - 1), 0))

    def prev(width):
        return pl.BlockSpec((rows, width), lambda s: (jnp.maximum(s - 1, 0), 0))

    def last(shape):
        return pl.BlockSpec((1,) + shape,
                            lambda s: (jnp.minimum(2 * s, nb - 1) // nblk,) + (0,) * len(shape))

    def tab(half):
        return pl.BlockSpec((BLK, 128), lambda s: (lax.rem(2 * s + half, nblk), 0))

    mix_params = ('conv_w', 'conv_b', 'lru_wa_bd', 'lru_wx_bd', 'lru_b_a', 'lru_b_x', 'lru_lambda', 'ret_norm')
    return pl.pallas_call(
        functools.partial(_mixout_body, layer=layer, pad=pad, nblk=nblk, final=final),
        grid=(steps + 1,),
        in_specs=[pl.BlockSpec(memory_space=pltpu.SMEM), cur(N_MIX), tab(0), tab(0), tab(1), tab(1),
                  prev(D_MODEL), prev(N_GATE)]
        + [lspec(p[k].shape) for k in mix_params]
        + [_layer_spec(p[k].shape, layer) for k in _OUT_WEIGHTS]
        + [pl.BlockSpec((1, D_MODEL), lambda s: (0, 0))],
        out_specs=[prev(D_MODEL), last((BLK, 128)), last((BLK, 128)), last((8, LRU_WIDTH)), last((8, LRU_WIDTH)),
                   last((RET_HEADS * RET_DK, RET_DV))],
        out_shape=[jax.ShapeDtypeStruct((n, D_MODEL), F32),
                   jax.ShapeDtypeStruct((bsz, BLK, 128), F32), jax.ShapeDtypeStruct((bsz, BLK, 128), F32),
                   jax.ShapeDtypeStruct((bsz, 8, LRU_WIDTH), F32), jax.ShapeDtypeStruct((bsz, 8, LRU_WIDTH), F32),
                   jax.ShapeDtypeStruct((bsz, RET_HEADS * RET_DK, RET_DV), F32)],
        scratch_shapes=[pltpu.VMEM((BLK + 8, LRU_WIDTH), F32), pltpu.VMEM((2, rows, N_OC), BF16),
                        pltpu.VMEM((8, LRU_WIDTH), F32),
                        pltpu.VMEM((SWA_KV_HEADS, BLK, 128), BF16), pltpu.VMEM((SWA_KV_HEADS, BLK, 128), BF16),
                        pltpu.VMEM((RET_HEADS * RET_DK, RET_DV), F32),
                        pltpu.VMEM((RET_HEADS, BLK, BLK), F32), pltpu.VMEM((RET_HEADS, BLK, 128), F32),
                        pltpu.VMEM((2, BLK, 128), F32),
                        pltpu.VMEM((8, LRU_WIDTH), F32), pltpu.VMEM((8, LRU_WIDTH), F32),
                        pltpu.VMEM((RET_HEADS * RET_DK, RET_DV), F32)],
        compiler_params=pltpu.CompilerParams(dimension_semantics=("arbitrary",), vmem_limit_bytes=VMEM_LIMIT),
        name='mix_out',
    )(p['swa_sinks'], pm, cos, sin, cos, sin, x1, pg, *[p[k] for k in mix_params],
      *[p[k] for k in _OUT_WEIGHTS], p['final_norm'])


def _mixs_body(pm_ref, ck_ref, cv_ref, conv_ref, h0_ref, s_ref, cw_ref, cb_ref, wa_ref, wx_ref, ba_ref, bx_ref,
               lam_ref, inv_ref, gn_ref, sk_ref, nk_all, nv_all, ns_all,
               oc_ref, nk_ref, nv_ref, nconv_ref, nh_ref, ns_ref,
               qb, o8, qr_s, kr_s, v4_s, o2_s):
    gsz = G_SEQ
    nrow = 16
    rs = slice(None)

    @pl.when(pl.program_id(0) == 0)
    def _():
        qb[...] = jnp.zeros_like(qb)
        qr_s[...] = jnp.zeros_like(qr_s)
        kr_s[...] = jnp.zeros_like(kr_s)
        v4_s[...] = jnp.zeros_like(v4_s)

    lane = lax.broadcasted_iota(jnp.int32, (1, 128), 1)
    lo = lane < 64

    xa = pm_ref[rs,C_XA:C_XA + LRU_WIDTH]
    h1 = conv_ref[:, LRU_WIDTH:2 * LRU_WIDTH]
    h2 = conv_ref[:, 2 * LRU_WIDTH:3 * LRU_WIDTH]
    xc = cb_ref[...] + conv_ref[:, 0:LRU_WIDTH] * cw_ref[0:1, :]
    xc = xc + h1 * cw_ref[1:2, :]
    xc = xc + h2 * cw_ref[2:3, :]
    xc = xc + xa * cw_ref[3:4, :]
    nconv_ref[:, 0:LRU_WIDTH] = h1
    nconv_ref[:, LRU_WIDTH:2 * LRU_WIDTH] = h2
    nconv_ref[:, 2 * LRU_WIDTH:3 * LRU_WIDTH] = xa
    a, bt = _lru_gates(xc, wa_ref, wx_ref, ba_ref, bx_ref, lam_ref)
    hn = bt + a * h0_ref[...]
    nh_ref[...] = hn
    oc_ref[:, 0:LRU_WIDTH] = (hn * jax.nn.gelu(pm_ref[rs,C_YA:C_YA + LRU_WIDTH])).astype(BF16)

    nk_ref[:, 0:WINDOW - 1, :] = ck_ref[:, 1:WINDOW, :]
    nv_ref[:, 0:WINDOW - 1, :] = cv_ref[:, 1:WINDOW, :]
    for b in range(gsz):
        nk_ref[b, WINDOW - 1:WINDOW, :] = pm_ref[b:b + 1, C_KS:C_KS + 128]
        nv_ref[b, WINDOW - 1:WINDOW, :] = pm_ref[b:b + 1, C_VS:C_VS + 128]

    for r in range(SWA_HEADS):
        h = r // SWA_GROUP
        slab = pm_ref[rs,C_QS + (r // 2) * 128:C_QS + (r // 2 + 1) * 128]
        if r % 2 != h:
            slab = pltpu.roll(slab, 64, 1)
        qb[r * gsz:(r + 1) * gsz, :] = jnp.where(lo if h == 0 else jnp.logical_not(lo), slab, 0.0)

    ang = float(PAST_LEN) * inv_ref[...]
    cos2, sin_signed, first_half = _rope_operands(jnp.cos(ang), jnp.sin(ang))
    qc = _rope(pm_ref[rs,C_QR:C_QR + 256], cos2, sin_signed, first_half)
    kc = _rope(pm_ref[rs,C_KR:C_KR + 256], cos2, sin_signed, first_half) * RET_DK ** -0.5
    lane256 = lax.broadcasted_iota(jnp.int32, (1, RET_HEADS * RET_DK), 1)
    for r in range(RET_HEADS):
        hm = (lane256 >= r * RET_DK) & (lane256 < (r + 1) * RET_DK)
        qm = jnp.where(hm, qc, 0.0)
        km = jnp.where(hm, kc, 0.0)
        for c in range(2):
            qr_s[c, r * gsz:(r + 1) * gsz, :] = qm[:, c * 128:(c + 1) * 128]
            kr_s[c, r * gsz:(r + 1) * gsz, :] = km[:, c * 128:(c + 1) * 128]
        v4_s[r * gsz:(r + 1) * gsz, :] = pm_ref[rs,C_VR + r * RET_DV:C_VR + (r + 1) * RET_DV]

    srow = lax.broadcasted_iota(jnp.int32, (RET_HEADS * RET_DK, 1), 0)
    gcol = jnp.full((RET_HEADS * RET_DK, 1), math.exp(LOG_G[0]), F32)
    for r in range(1, RET_HEADS):
        gcol = jnp.where(srow >= r * RET_DK, math.exp(LOG_G[r]), gcol)
    sk = sk_ref[:, 0:1]

    def per_seq(b, carry):
        rows = pl.ds(b, nrow, stride=gsz)
        s = _dot_nt(qb[rows, :].astype(BF16), nk_ref[b].astype(BF16)) * SWA_HEAD_DIM ** -0.5
        m = jnp.maximum(jnp.max(s, axis=-1, keepdims=True), sk)
        e = jnp.exp(s - m)
        den = jnp.sum(e, axis=-1, keepdims=True) + jnp.exp(sk - m)
        o8[rows, :] = _dot(e.astype(BF16), nv_ref[b].astype(BF16)) / den
        sb = s_ref[b]
        sbb = sb.astype(BF16)
        o2_s[rows, :] = (_dot(qr_s[0, rows, :].astype(BF16), sbb[0:128, :])
                         + _dot(qr_s[1, rows, :].astype(BF16), sbb[128:256, :]))
        v4 = v4_s[rows, :].astype(BF16)
        kv = jnp.concatenate([_dot_tn(kr_s[0, rows, :].astype(BF16), v4),
                              _dot_tn(kr_s[1, rows, :].astype(BF16), v4)], axis=0)
        ns_ref[b] = gcol * sb + kv
        return carry

    lax.fori_loop(0, gsz, per_seq, 0)

    for sl in range(4):
        h = sl // 2
        ev = o8[(2 * sl) * gsz:(2 * sl + 1) * gsz, :]
        od = o8[(2 * sl + 1) * gsz:(2 * sl + 2) * gsz, :]
        if h != 0:
            ev = pltpu.roll(ev, 64, 1)
        if h != 1:
            od = pltpu.roll(od, 64, 1)
        oc_ref[:, LRU_WIDTH + sl * 128:LRU_WIDTH + (sl + 1) * 128] = jnp.where(lo, ev, od).astype(BF16)

    prod = qc * kc
    p_hi = prod.astype(BF16)
    p_lo = (prod - p_hi.astype(F32)).astype(BF16)
    er = lax.broadcasted_iota(jnp.int32, (RET_HEADS * RET_DK, RET_HEADS * RET_DV), 0) // RET_DK
    ec = lax.broadcasted_iota(jnp.int32, (RET_HEADS * RET_DK, RET_HEADS * RET_DV), 1) // RET_DV
    expand = jnp.where(er == ec, 1.0, 0.0).astype(BF16)
    qk = _dot(p_hi, expand) + _dot(p_lo, expand)
    for r in range(RET_HEADS):
        cs = slice(r * RET_DV, (r + 1) * RET_DV)
        o = qk[:, cs] * pm_ref[rs,C_VR + r * RET_DV:C_VR + (r + 1) * RET_DV]
        o = o + o2_s[r * gsz:(r + 1) * gsz, :] * math.exp(LOG_G[r])
        oc = _group_norm_gate(o, gn_ref[:, cs], pm_ref[rs,C_GR + r * RET_DV:C_GR + (r + 1) * RET_DV])
        oc_ref[:, 2 * LRU_WIDTH + r * RET_DV:2 * LRU_WIDTH + (r + 1) * RET_DV] = oc.astype(BF16)


def _call_mix_sample(pm, ck, cv, conv, h0, sret, stacked, p, layer):
    nseq = pm.shape[0]
    depth = ck.shape[0]
    whole = pl.BlockSpec(memory_space=pl.ANY)
    gsz = G_SEQ

    def lspec(shape):
        nd = len(shape)
        return pl.BlockSpec((None,) + tuple(shape[1:]), lambda i: (layer,) + (0,) * (nd - 1))

    def seq2(width):
        return pl.BlockSpec((gsz, width), lambda i: (i, 0))

    def seq3(layered, d1, d2):
        if layered:
            return pl.BlockSpec((None, gsz, d1, d2), lambda i: (layer, i, 0, 0))
        return pl.BlockSpec((gsz, d1, d2), lambda i: (i, 0, 0))

    sdim = RET_HEADS * RET_DK
    return pl.pallas_call(
        _mixs_body,
        grid=(nseq // gsz,),
        in_specs=[seq2(N_MIX), seq3(True, WINDOW, 128), seq3(True, WINDOW, 128),
                  pl.BlockSpec((None, gsz, 3 * LRU_WIDTH), lambda i: (layer, i, 0)),
                  pl.BlockSpec((None, gsz, LRU_WIDTH), lambda i: (layer, i, 0)),
                  seq3(True, sdim, RET_DV),
                  lspec(p['conv_w'].shape), lspec(p['conv_b'].shape), lspec(p['lru_wa_bd'].shape),
                  lspec(p['lru_wx_bd'].shape), lspec(p['lru_b_a'].shape), lspec(p['lru_b_x'].shape),
                  lspec(p['lru_lambda'].shape), pl.BlockSpec((1, 128), lambda i: (0, 0)),
                  lspec(p['ret_norm'].shape), lspec(p['sinks16'].shape), whole, whole, whole],
        out_specs=[seq2(N_OC), seq3(True, WINDOW, 128), seq3(True, WINDOW, 128), seq2(3 * LRU_WIDTH),
                   seq2(LRU_WIDTH), seq3(True, sdim, RET_DV)],
        out_shape=[jax.ShapeDtypeStruct((nseq, N_OC), BF16),
                   jax.ShapeDtypeStruct((depth, nseq, WINDOW, 128), F32),
                   jax.ShapeDtypeStruct((depth, nseq, WINDOW, 128), F32),
                   jax.ShapeDtypeStruct((nseq, 3 * LRU_WIDTH), F32), jax.ShapeDtypeStruct((nseq, LRU_WIDTH), F32),
                   jax.ShapeDtypeStruct((depth, nseq, sdim, RET_DV), F32)],
        input_output_aliases={16: 1, 17: 2, 18: 5},
        scratch_shapes=[pltpu.VMEM((16 * gsz, 128), F32), pltpu.VMEM((16 * gsz, 128), F32),
                        pltpu.VMEM((2, 16 * gsz, 128), F32), pltpu.VMEM((2, 16 * gsz, 128), F32),
                        pltpu.VMEM((16 * gsz, 128), F32), pltpu.VMEM((16 * gsz, 128), F32)],
        compiler_params=pltpu.CompilerParams(dimension_semantics=("arbitrary",), vmem_limit_bytes=VMEM_LIMIT),
        name='mix_sample',
    )(pm, ck, cv, conv, h0, sret, p['conv_w'], p['conv_b'], p['lru_wa_bd'], p['lru_wx_bd'], p['lru_b_a'],
      p['lru_b_x'], p['lru_lambda'], p['rope_inv'], p['ret_norm'], p['sinks16'], *stacked)


def _block_diag(w):
    depth = w.shape[0]
    w = w.reshape(depth, 2, 4, LRU_BW, LRU_BW)
    eye = jnp.eye(4, dtype=w.dtype)
    return jnp.einsum('lsncd,nm->lsncmd', w, eye).reshape(depth, 2, 4 * LRU_BW, 4 * LRU_BW)


def kernel(x_prompt, x_sample, cache_swa_k, cache_swa_v, state_conv, state_lru, state_ret, meta_tokens, ffn1_norm,
           ffn1_w_gu, ffn1_w_down, mix_norm, w_in, conv_w, conv_b, lru_w_a, lru_b_a, lru_w_x, lru_b_x, lru_lambda,
           swa_sinks, ret_norm, w_branch_a, w_branch_b, w_branch_c, w_out, ffn2_norm, ffn2_w_gu, ffn2_w_down,
           final_norm):
    depth = w_in.shape[0]
    bsz, seq, _ = x_prompt.shape
    nseq = x_sample.shape[0]
    buf = cache_swa_k.shape[2]
    assert buf == WINDOW == BLK and x_sample.shape[1] == 1 and nseq % G_SEQ == 0
    t = seq + N_META
    pad = (-t) % BLK
    tp = t + pad
    assert (bsz * tp) % TM_DENSE == 0

    def row(v):
        return v.reshape(depth, 1, -1).astype(F32)

    def packed(w):
        wb = lax.bitcast_convert_type(w.astype(BF16), jnp.uint16).astype(jnp.uint32)
        return wb[..., 0::2, :] | (wb[..., 1::2, :] << 16)

    half = jnp.arange(128) % (RET_DK // 2)
    p = {
        'ffn1_norm': row(ffn1_norm), 'ffn1_w_gu': packed(ffn1_w_gu), 'ffn1_w_down': packed(ffn1_w_down),
        'mix_norm': row(mix_norm), 'w_mix': packed(w_in[:, :, :N_MIX]), 'w_gate': packed(w_in[:, :, N_MIX:]),
        'conv_w': conv_w.astype(F32), 'conv_b': row(conv_b),
        'lru_wa_bd': _block_diag(lru_w_a).astype(BF16), 'lru_wx_bd': _block_diag(lru_w_x).astype(BF16),
        'lru_b_a': row(lru_b_a), 'lru_b_x': row(lru_b_x), 'lru_lambda': row(lru_lambda),
        'swa_sinks': swa_sinks.astype(F32),
        'sinks16': jnp.pad(jnp.broadcast_to(swa_sinks.astype(F32)[:, :, None], (depth, SWA_HEADS, 128)),
                           ((0, 0), (0, 16 - SWA_HEADS), (0, 0))),
        'ret_norm': row(ret_norm),
        'w_branch_a': packed(w_branch_a), 'w_branch_b': packed(w_branch_b),
        'w_branch_c': packed(w_branch_c), 'w_out': packed(w_out),
        'ffn2_norm': row(ffn2_norm), 'ffn2_w_gu': packed(ffn2_w_gu), 'ffn2_w_down': packed(ffn2_w_down),
        'final_norm': final_norm.reshape(1, D_MODEL).astype(F32),
        'rope_inv': (ROPE_BASE ** (-half.astype(F32) / (RET_DK // 2))).reshape(1, 128),
    }

    meta = jnp.broadcast_to(meta_tokens.astype(F32)[None], (bsz, N_META, D_MODEL))
    xp = jnp.concatenate([jnp.zeros((bsz, pad, D_MODEL), F32), meta, x_prompt], axis=1).reshape(bsz * tp, D_MODEL)
    xs = x_sample.reshape(nseq, D_MODEL)
    ck = cache_swa_k.reshape(depth, nseq, buf, 128)
    cv = cache_swa_v.reshape(depth, nseq, buf, 128)
    conv = state_conv.reshape(depth, nseq, 3 * LRU_WIDTH)
    sret = state_ret.reshape(depth, nseq, RET_HEADS * RET_DK, RET_DV)
    cos, sin = _call_ropetab(p['rope_inv'], tp, pad)

    outs_p = [[] for _ in range(5)]
    outs_s = [[], []]
    stacked = (jnp.zeros(ck.shape, F32), jnp.zeros(cv.shape, F32), jnp.zeros(sret.shape, F32))
    for layer in range(depth):
        final = layer == depth - 1
        x1, pm, pg = _call_in(xp, p, layer, TM_DENSE)
        xp, kl, vl, cl, hl, sl = _call_mix_out(pm, cos, sin, x1, pg, p, layer, bsz, pad, final)
        for acc, o in zip(outs_p, (kl.reshape(bsz, buf, SWA_KV_HEADS, SWA_HEAD_DIM),
                                   vl.reshape(bsz, buf, SWA_KV_HEADS, SWA_HEAD_DIM),
                                   cl[:, 8 - (CONV_WIDTH - 1):, :], hl[:, 7, :],
                                   sl.reshape(bsz, RET_HEADS, RET_DK, RET_DV))):
            acc.append(o)

        x1, pm, pg = _call_in(xs, p, layer, nseq)
        oc, nk, nv, nc, nh, ns = _call_mix_sample(pm, ck, cv, conv, state_lru, sret, stacked, p, layer)
        stacked = (nk, nv, ns)
        xs = _call_out(x1, pg, oc, p, layer, nseq, final)
        outs_s[0].append(nc.reshape(nseq, CONV_WIDTH - 1, LRU_WIDTH))
        outs_s[1].append(nh)

    yp = xp.reshape(bsz, tp, D_MODEL)[:, pad + N_META:]
    ys = xs.reshape(nseq, 1, D_MODEL)
    nk, nv, ns = stacked
    return ((yp, ys) + tuple(jnp.stack(a) for a in outs_p)
            + (nk.reshape(depth, nseq, buf, SWA_KV_HEADS, SWA_HEAD_DIM),
               nv.reshape(depth, nseq, buf, SWA_KV_HEADS, SWA_HEAD_DIM),
               jnp.stack(outs_s[0]), jnp.stack(outs_s[1]),
               ns.reshape(depth, nseq, RET_HEADS, RET_DK, RET_DV)))
```

```python
import functools
import math

import jax
import jax.numpy as jnp
from jax import lax
from jax.experimental import pallas as pl
from jax.experimental.pallas import tpu as pltpu

F32 = jnp.float32
BF16 = jnp.bfloat16

D_MODEL = 1024
D_FF = 2048
N_META = 16
EPS = 1e-6
LRU_WIDTH = 512
LRU_BLOCKS = 8
LRU_BW = 64
CONV_WIDTH = 4
LRU_C = 8.0
SWA_HEAD_DIM = 64
SWA_HEADS = 8
SWA_KV_HEADS = 2
SWA_GROUP = 4
WINDOW = 128
RET_DK = 64
RET_DV = 128
RET_HEADS = 4
ROPE_BASE = 10000.0
GN_EPS = 1e-5
PAST_LEN = 8192

BLK = 128
N_MIX = 3328
N_GATE = 3 * D_MODEL
C_XA, C_YA, C_QS, C_KS, C_VS, C_QR, C_KR, C_VR, C_GR = 0, 512, 1024, 1536, 1664, 1792, 2048, 2304, 2816
N_OC = 3 * LRU_WIDTH
LOG_G = tuple(math.log1p(-(2.0 ** (-5.0 - h))) for h in range(RET_HEADS))

TM_DENSE = 256
G_SEQ = 16
VMEM_LIMIT = 56 * 1024 * 1024


def _dot(a, b):
    return jnp.dot(a, b, preferred_element_type=F32)


def _dot_nt(a, b):
    return lax.dot_general(a, b, (((1,), (1,)), ((), ())), preferred_element_type=F32)


def _dot_tn(a, b):
    return lax.dot_general(a, b, (((0,), (0,)), ((), ())), preferred_element_type=F32)


def _rms(x, g):
    return x * lax.rsqrt(jnp.mean(x * x, axis=-1, keepdims=True) + EPS) * g


def _softplus(x):
    return jnp.maximum(x, 0.0) + jnp.log1p(jnp.exp(-jnp.abs(x)))


def _wt(ref, rows=None, cols=None):
    r = slice(None) if rows is None else slice(*rows)
    c = slice(None) if cols is None else slice(*cols)
    return ref[r, c]


def _swiglu(u, wgu_ref, wd_ref):
    gu = _dot(u, _wt(wgu_ref))
    act = (jax.nn.silu(gu[:, :D_FF]) * gu[:, D_FF:]).astype(BF16)
    return _dot(act, _wt(wd_ref))


def _in_body(x_ref, n1_ref, wgu_ref, wd_ref, n2_ref, wm_ref, wg_ref, x1_ref, pm_ref, pg_ref):
    x = x_ref[...]
    x1 = x + 0.5 * _swiglu(_rms(x, n1_ref[...]).astype(BF16), wgu_ref, wd_ref)
    x1_ref[...] = x1
    u2 = _rms(x1, n2_ref[...]).astype(BF16)
    pm_ref[...] = _dot(u2, _wt(wm_ref))
    pg_ref[...] = _dot(u2, _wt(wg_ref))


def _merge_out(x1, pg, oc, wa_ref, wb_ref, wc_ref, wo_ref, n_ref, wgu_ref, wd_ref, fn_ref, final):
    g = jax.nn.sigmoid(pg)
    merged = (g[:, :D_MODEL] * _dot(oc[:, :LRU_WIDTH], _wt(wa_ref))
              + g[:, D_MODEL:2 * D_MODEL] * _dot(oc[:, LRU_WIDTH:2 * LRU_WIDTH], _wt(wb_ref))
              + g[:, 2 * D_MODEL:] * _dot(oc[:, 2 * LRU_WIDTH:], _wt(wc_ref)))
    x2 = x1 + _dot(merged.astype(BF16), _wt(wo_ref))
    x3 = x2 + 0.5 * _swiglu(_rms(x2, n_ref[...]).astype(BF16), wgu_ref, wd_ref)
    if final:
        x3 = _rms(x3, fn_ref[...])
    return x3


FF_CHUNK = 256


def _merge_out_pieces(x1_ref, pg_ref, oc, wa_ref, wb_ref, wc_ref, wo_ref, n_ref, wgu_ref, wd_ref, fn_ref, out_ref,
                      final):
    merged = None
    for b, w_ref in enumerate((wa_ref, wb_ref, wc_ref)):
        for c in range(0, D_MODEL, 2 * FF_CHUNK):
            g = jax.nn.sigmoid(pg_ref[:, b * D_MODEL + c:b * D_MODEL + c + 2 * FF_CHUNK])
            part = g * _dot(oc[:, b * LRU_WIDTH:(b + 1) * LRU_WIDTH], _wt(w_ref, cols=(c, c + 2 * FF_CHUNK)))
            if b == 0:
                merged = [part] if c == 0 else merged + [part]
            else:
                merged[c // (2 * FF_CHUNK)] = merged[c // (2 * FF_CHUNK)] + part
            yield
    merged = jnp.concatenate(merged, axis=1).astype(BF16)
    x2 = []
    for c in range(0, D_MODEL, FF_CHUNK):
        x2.append(x1_ref[:, c:c + FF_CHUNK] + _dot(merged, _wt(wo_ref, cols=(c, c + FF_CHUNK))))
        yield
    x2 = jnp.concatenate(x2, axis=1)
    u = _rms(x2, n_ref[...]).astype(BF16)
    yield
    y = None
    for c in range(D_FF // FF_CHUNK):
        lo, hi = c * FF_CHUNK, (c + 1) * FF_CHUNK
        gate = _dot(u, _wt(wgu_ref, cols=(lo, hi)))
        yield
        up = _dot(u, _wt(wgu_ref, cols=(D_FF + lo, D_FF + hi)))
        yield
        part = _dot((jax.nn.silu(gate) * up).astype(BF16), _wt(wd_ref, rows=(lo, hi)))
        y = part if y is None else y + part
        yield
    x3 = x2 + 0.5 * y
    out_ref[...] = _rms(x3, fn_ref[...]) if final else x3
    yield


def _alternate(first, second, ratio):
    live = [True, True]
    while any(live):
        for idx, (gen, count) in enumerate(((first, 1), (second, ratio))):
            for _ in range(count):
                if live[idx] and next(gen, StopIteration) is StopIteration:
                    live[idx] = False


def _out_body(x1_ref, pg_ref, oc_ref, wa_ref, wb_ref, wc_ref, wo_ref, n_ref, wgu_ref, wd_ref, fn_ref, out_ref,
              *, final):
    out_ref[...] = _merge_out(x1_ref[...], pg_ref[...], oc_ref[...], wa_ref, wb_ref, wc_ref, wo_ref, n_ref,
                              wgu_ref, wd_ref, fn_ref, final)


def _layer_spec(shape, layer):
    nd = len(shape)
    return pl.BlockSpec((None,) + tuple(shape[1:]), lambda *_: (layer,) + (0,) * (nd - 1),
                        pipeline_mode=pl.Buffered(1))


def _row_spec(tm, width):
    return pl.BlockSpec((tm, width), lambda i: (i, 0))


def _dense_params():
    return pltpu.CompilerParams(dimension_semantics=("arbitrary",), vmem_limit_bytes=VMEM_LIMIT)


def _call_in(x, p, layer, tm):
    n = x.shape[0]
    return pl.pallas_call(
        _in_body,
        grid=(n // tm,),
        in_specs=[_row_spec(tm, D_MODEL),
                  _layer_spec(p['ffn1_norm'].shape, layer), _layer_spec(p['ffn1_w_gu'].shape, layer),
                  _layer_spec(p['ffn1_w_down'].shape, layer), _layer_spec(p['mix_norm'].shape, layer),
                  _layer_spec(p['w_mix'].shape, layer), _layer_spec(p['w_gate'].shape, layer)],
        out_specs=[_row_spec(tm, D_MODEL), _row_spec(tm, N_MIX), _row_spec(tm, N_GATE)],
        out_shape=[jax.ShapeDtypeStruct((n, D_MODEL), F32), jax.ShapeDtypeStruct((n, N_MIX), F32),
                   jax.ShapeDtypeStruct((n, N_GATE), F32)],
        compiler_params=_dense_params(),
        name='layer_in',
    )(x, p['ffn1_norm'], p['ffn1_w_gu'], p['ffn1_w_down'], p['mix_norm'], p['w_mix'], p['w_gate'])


_OUT_WEIGHTS = ('w_branch_a', 'w_branch_b', 'w_branch_c', 'w_out', 'ffn2_norm', 'ffn2_w_gu', 'ffn2_w_down')


def _call_out(x1, pg, oc, p, layer, tm, final):
    n = x1.shape[0]
    return pl.pallas_call(
        functools.partial(_out_body, final=final),
        grid=(n // tm,),
        in_specs=[_row_spec(tm, D_MODEL), _row_spec(tm, N_GATE), _row_spec(tm, N_OC)]
        + [_layer_spec(p[k].shape, layer) for k in _OUT_WEIGHTS]
        + [pl.BlockSpec((1, D_MODEL), lambda i: (0, 0))],
        out_specs=_row_spec(tm, D_MODEL),
        out_shape=jax.ShapeDtypeStruct((n, D_MODEL), F32),
        compiler_params=_dense_params(),
        name='layer_out',
    )(x1, pg, oc, *[p[k] for k in _OUT_WEIGHTS], p['final_norm'])


def _drain(pieces):
    while True:
        try:
            next(pieces)
        except StopIteration as done:
            return done.value


def _lru_gate_pieces(xc, wa_ref, wx_ref, ba_ref, bx_ref, lam_ref):
    xcb = xc.astype(BF16)
    half = LRU_WIDTH // 2
    rpre = jnp.concatenate([_dot(xcb[:, :half], wa_ref[0]), _dot(xcb[:, half:], wa_ref[1])], axis=1)
    yield
    ipre = jnp.concatenate([_dot(xcb[:, :half], wx_ref[0]), _dot(xcb[:, half:], wx_ref[1])], axis=1)
    yield
    r = jax.nn.sigmoid(rpre + ba_ref[...])
    i = jax.nn.sigmoid(ipre + bx_ref[...])
    yield
    log_a = -LRU_C * r * _softplus(-lam_ref[...])
    a = jnp.exp(log_a)
    yield
    z = -jnp.tanh(log_a) * (a * a + 1.0)
    return a, jnp.where(z > 0.0, z * lax.rsqrt(z), 0.0) * (i * xc)


def _lru_gates(*args):
    return _drain(_lru_gate_pieces(*args))


def _rope_operands(cos, sin):
    lane = lax.broadcasted_iota(jnp.int32, (1, 4 * RET_DK), 1)
    first_half = (lane & (RET_DK - 1)) < RET_DK // 2
    cos2 = jnp.concatenate([cos, cos], axis=1)
    sin2 = jnp.concatenate([sin, sin], axis=1)
    return cos2, jnp.where(first_half, -sin2, sin2), first_half


def _rope(x, cos2, sin_signed, first_half):
    swapped = jnp.where(first_half, pltpu.roll(x, 4 * RET_DK - RET_DK // 2, 1), pltpu.roll(x, RET_DK // 2, 1))
    return x * cos2 + swapped * sin_signed


def _group_norm_gate(o, gain, gr):
    mu = jnp.mean(o, axis=-1, keepdims=True)
    d = o - mu
    var = jnp.mean(d * d, axis=-1, keepdims=True)
    return d * lax.rsqrt(var + GN_EPS) * gain * jax.nn.silu(gr)


def _ropetab_body(inv_ref, cos_ref, sin_ref, *, rows_per_step, pad):
    rows = lax.broadcasted_iota(jnp.int32, (rows_per_step, 1), 0)
    pos = (pl.program_id(0) * rows_per_step + rows - pad).astype(F32)
    ang = pos * inv_ref[...]
    cos_ref[...] = jnp.cos(ang)
    sin_ref[...] = jnp.sin(ang)


def _call_ropetab(inv, tp, pad):
    nblk = tp // BLK
    rps = BLK * max(d for d in (8, 5, 4, 2, 1) if nblk % d == 0)
    spec = pl.BlockSpec((rps, 128), lambda i: (i, 0))
    return pl.pallas_call(
        functools.partial(_ropetab_body, rows_per_step=rps, pad=pad),
        grid=(tp // rps,),
        in_specs=[pl.BlockSpec((1, 128), lambda i: (0, 0))],
        out_specs=[spec, spec],
        out_shape=[jax.ShapeDtypeStruct((tp, 128), F32), jax.ShapeDtypeStruct((tp, 128), F32)],
        name='rope_tables',
    )(inv)


def _block_masks(j, pad):
    lo = lax.broadcasted_iota(jnp.int32, (1, 128), 1) < 64
    rows = lax.broadcasted_iota(jnp.int32, (BLK, 1), 0)
    return lo, (j * BLK + rows) >= pad


def _mix_lru(pm_ref, cw_ref, cb_ref, wa_ref, wx_ref, ba_ref, bx_ref, lam_ref, xext, ocs, hcar, a_s, b_s, h_s,
             *, pad, j, cur, r0):
    rs = slice(r0, r0 + BLK)
    fresh = j == 0
    _, valid = _block_masks(j, pad)
    xa = jnp.where(valid, pm_ref[rs,C_XA:C_XA + LRU_WIDTH], 0.0)
    hist = jnp.where(fresh, 0.0, xext[...])
    row8 = lax.broadcasted_iota(jnp.int32, (8, 1), 0)
    xc = cb_ref[...]
    for tap in range(CONV_WIDTH - 1):
        k = CONV_WIDTH - 1 - tap
        sh = pltpu.roll(xa, k, 0)
        top = jnp.where(row8 < k, pltpu.roll(hist, k, 0), sh[0:8, :])
        xc = xc + jnp.concatenate([top, sh[8:, :]], axis=0) * cw_ref[tap:tap + 1, :]
        yield
    xc = xc + xa * cw_ref[CONV_WIDTH - 1:CONV_WIDTH, :]
    xext[...] = xa[BLK - 8:BLK, :]
    yield
    a, bt = yield from _lru_gate_pieces(xc, wa_ref, wx_ref, ba_ref, bx_ref, lam_ref)
    bt = jnp.where(valid, bt, 0.0)
    yield
    ngrp = BLK // 8
    for c in range(LRU_WIDTH // 128):
        a_s[c] = a[:, c * 128:(c + 1) * 128]
        b_s[c] = bt[:, c * 128:(c + 1) * 128]
    yield
    h_in = jnp.where(fresh, 0.0, hcar[7:8, :])
    for c in range(LRU_WIDTH // 128):
        prods, sums = [], []
        for r in range(8):
            ar = a_s[c, pl.ds(r, ngrp, stride=8), :]
            br = b_s[c, pl.ds(r, ngrp, stride=8), :]
            prods.append(ar if r == 0 else ar * prods[-1])
            sums.append(br if r == 0 else ar * sums[-1] + br)
        carry = h_in[:, c * 128:(c + 1) * 128]
        carries = []
        for g in range(ngrp):
            carries.append(carry)
            carry = prods[7][g:g + 1, :] * carry + sums[7][g:g + 1, :]
        carries = jnp.concatenate(carries, axis=0)
        for r in range(8):
            h_s[c, pl.ds(r, ngrp, stride=8), :] = prods[r] * carries + sums[r]
        yield
    h = jnp.concatenate([h_s[c] for c in range(LRU_WIDTH // 128)], axis=1)
    hcar[...] = h[BLK - 8:BLK, :]
    o_a = h * jax.nn.gelu(pm_ref[rs,C_YA:C_YA + LRU_WIDTH])
    ocs[cur, rs,0:LRU_WIDTH] = o_a.astype(BF16)
    yield


def _mix_swa(sinks_ref, pm_ref, ocs, kprev, vprev, *, layer, pad, j, cur, r0):
    rs = slice(r0, r0 + BLK)
    lo, _ = _block_masks(j, pad)
    k = pm_ref[rs,C_KS:C_KS + 128]
    v = pm_ref[rs,C_VS:C_VS + 128]
    k_sw = pltpu.roll(k, 64, 1)
    v_sw = pltpu.roll(v, 64, 1)
    kdup = (jnp.where(lo, k, k_sw).astype(BF16), jnp.where(lo, k_sw, k).astype(BF16))
    vdup = (jnp.where(lo, v, v_sw).astype(BF16), jnp.where(lo, v_sw, v).astype(BF16))
    row4 = lax.broadcasted_iota(jnp.int32, (SWA_GROUP * BLK, 1), 0)
    t4 = row4 & (BLK - 1)
    col = lax.broadcasted_iota(jnp.int32, (1, 2 * BLK), 1)
    ok = ((j - 1) * BLK + col >= pad) & (col > t4) & (col <= t4 + BLK)
    for h in range(SWA_KV_HEADS):
        parts = []
        for g in range(SWA_GROUP):
            head = SWA_GROUP * h + g
            slab = pm_ref[rs,C_QS + (head // 2) * 128:C_QS + (head // 2 + 1) * 128] * SWA_HEAD_DIM ** -0.5
            parts.append(jnp.where(lo if head % 2 == 0 else jnp.logical_not(lo), slab, 0.0))
        qst = jnp.concatenate(parts, axis=0).astype(BF16)
        kcat = jnp.concatenate([kprev[h], kdup[h]], axis=0)
        vcat = jnp.concatenate([vprev[h], vdup[h]], axis=0)
        yield
        sc = jnp.where(ok, _dot_nt(qst, kcat), -jnp.inf)
        yield
        sk = jnp.full((SWA_GROUP * BLK, 1), sinks_ref[layer, SWA_GROUP * h], F32)
        for g in range(1, SWA_GROUP):
            sk = jnp.where(row4 >= g * BLK, sinks_ref[layer, SWA_GROUP * h + g], sk)
        m = jnp.maximum(jnp.max(sc, axis=-1, keepdims=True), sk)
        yield
        e = jnp.exp(sc - m)
        yield
        den = jnp.sum(e, axis=-1, keepdims=True) + jnp.exp(sk - m)
        yield
        o = _dot(e.astype(BF16), vcat) / den
        yield
        for sl in range(2):
            ge = 2 * sl
            slab = jnp.where(lo, o[ge * BLK:(ge + 1) * BLK, :], o[(ge + 1) * BLK:(ge + 2) * BLK, :])
            c0 = LRU_WIDTH + (2 * h + sl) * 128
            ocs[cur, rs,c0:c0 + 128] = slab.astype(BF16)
        kprev[h] = kdup[h]
        vprev[h] = vdup[h]
        yield


def _mix_ret(pm_ref, cos_ref, sin_ref, gn_ref, ocs, state, dec_t, cross_t, kdec_t, *, pad, j, cur, r0):
    rs = slice(r0, r0 + BLK)
    fresh = j == 0
    lo, valid = _block_masks(j, pad)
    cos2, sin_signed, first_half = _rope_operands(cos_ref[...], sin_ref[...])
    qc = _rope(pm_ref[rs,C_QR:C_QR + 256], cos2, sin_signed, first_half)
    yield
    kc = _rope(pm_ref[rs,C_KR:C_KR + 256], cos2, sin_signed, first_half) * RET_DK ** -0.5
    kc = jnp.where(valid, kc, 0.0)
    yield
    st = [jnp.where(fresh, 0.0, state[sl * 128:(sl + 1) * 128, :]) for sl in range(2)]
    yield
    upd = [None, None]
    for h in range(RET_HEADS):
        sl = h // 2
        half = lo if h % 2 == 0 else jnp.logical_not(lo)
        qm = jnp.where(half, qc[:, sl * 128:(sl + 1) * 128], 0.0).astype(BF16)
        kslab = kc[:, sl * 128:(sl + 1) * 128]
        vh = jnp.where(valid, pm_ref[rs,C_VR + h * RET_DV:C_VR + (h + 1) * RET_DV], 0.0).astype(BF16)
        sc = (_dot_nt(qm, kslab.astype(BF16)) * dec_t[h]).astype(BF16)
        yield
        o = _dot(sc, vh)
        o = o + _dot(qm, st[sl].astype(BF16)) * cross_t[h]
        yield
        c0 = 2 * LRU_WIDTH + h * RET_DV
        oc = _group_norm_gate(o, gn_ref[:, h * RET_DV:(h + 1) * RET_DV],
                              pm_ref[rs,C_GR + h * RET_DV:C_GR + (h + 1) * RET_DV])
        ocs[cur, rs,c0:c0 + RET_DV] = oc.astype(BF16)
        yield
        km = jnp.where(half, kslab * kdec_t[sl], 0.0).astype(BF16)
        u = _dot_tn(km, vh)
        upd[sl] = u if upd[sl] is None else upd[sl] + u
        yield
    srow = lax.broadcasted_iota(jnp.int32, (128, 1), 0)
    for sl in range(2):
        gcol = jnp.where(srow < RET_DK, math.exp(BLK * LOG_G[2 * sl]), math.exp(BLK * LOG_G[2 * sl + 1]))
        state[sl * 128:(sl + 1) * 128, :] = gcol * st[sl] + upd[sl]


def _mixout_body(sinks_ref, pm_ref, cos0_ref, sin0_ref, cos1_ref, sin1_ref, x1_ref, pg_ref,
                 cw_ref, cb_ref, wa_ref, wx_ref, ba_ref, bx_ref, lam_ref, gn_ref,
                 wba_ref, wbb_ref, wbc_ref, wo_ref, n2_ref, wgu_ref, wd_ref, fn_ref,
                 out_ref, klast_ref, vlast_ref, conv_ref, hlast_ref, sret_ref,
                 xext, ocs, hcar, kprev, vprev, state, dec_t, cross_t, kdec_t, snap_x, snap_h, snap_s, a_s, b_s, h_s,
                 *, layer, pad, nblk, final):
    s = pl.program_id(0)

    @pl.when(s == 0)
    def _():
        lo = lax.broadcasted_iota(jnp.int32, (1, 128), 1) < 64
        ti = lax.broadcasted_iota(jnp.int32, (BLK, 1), 0).astype(F32)
        tj = lax.broadcasted_iota(jnp.int32, (1, BLK), 1).astype(F32)
        diff = ti - tj
        for h in range(RET_HEADS):
            dec_t[h] = jnp.exp(jnp.where(diff >= 0, diff * LOG_G[h], -jnp.inf))
            cross_t[h] = jnp.broadcast_to(jnp.exp((ti + 1.0) * LOG_G[h]), (BLK, 128))
        for sl in range(2):
            kdec_t[sl] = jnp.where(lo, jnp.exp((BLK - 1.0 - ti) * LOG_G[2 * sl]),
                                   jnp.exp((BLK - 1.0 - ti) * LOG_G[2 * sl + 1]))
        for ref in (ocs, xext, hcar, kprev, vprev, state):
            ref[...] = jnp.zeros_like(ref)

    cur = lax.rem(s, 2)
    j0 = lax.rem(2 * s, nblk)
    j1 = lax.rem(2 * s + 1, nblk)
    dense = _merge_out_pieces(x1_ref, pg_ref, ocs[lax.rem(s + 1, 2)], wba_ref, wbb_ref, wbc_ref, wo_ref, n2_ref,
                              wgu_ref, wd_ref, fn_ref, out_ref, final)

    def lru(j, r0):
        return _mix_lru(pm_ref, cw_ref, cb_ref, wa_ref, wx_ref, ba_ref, bx_ref, lam_ref, xext, ocs, hcar,
                        a_s, b_s, h_s, pad=pad, j=j, cur=cur, r0=r0)

    def swa(j, r0):
        return _mix_swa(sinks_ref, pm_ref, ocs, kprev, vprev, layer=layer, pad=pad, j=j, cur=cur, r0=r0)

    def ret(j, r0, cos_ref, sin_ref):
        return _mix_ret(pm_ref, cos_ref, sin_ref, gn_ref, ocs, state, dec_t, cross_t, kdec_t,
                        pad=pad, j=j, cur=cur, r0=r0)

    def mixers():
        yield from lru(j0, 0)
        yield from swa(j0, 0)
        yield from ret(j0, 0, cos0_ref, sin0_ref)
        snap_x[...] = xext[...]
        snap_h[...] = hcar[...]
        snap_s[...] = state[...]
        yield from lru(j1, BLK)
        yield from swa(j1, BLK)
        yield from ret(j1, BLK, cos1_ref, sin1_ref)

    _alternate(dense, mixers(), 3)

    def write_state(r0, conv, hl, st):
        klast_ref[0] = pm_ref[r0:r0 + BLK, C_KS:C_KS + 128]
        vlast_ref[0] = pm_ref[r0:r0 + BLK, C_VS:C_VS + 128]
        conv_ref[0] = conv[...]
        hlast_ref[0] = hl[...]
        sret_ref[0] = st[...]

    @pl.when(j0 == nblk - 1)
    def _():
        write_state(0, snap_x, snap_h, snap_s)

    @pl.when(j1 == nblk - 1)
    def _():
        write_state(BLK, xext, hcar, state)


def _call_mix_out(pm, cos, sin, x1, pg, p, layer, bsz, pad, final):
    n = pm.shape[0]
    nb = n // BLK
    nblk = nb // bsz
    assert nb % 2 == 0 and nblk >= 2
    steps = nb // 2
    rows = 2 * BLK

    def lspec(shape):
        nd = len(shape)
        return pl.BlockSpec((None,) + tuple(shape[1:]), lambda s: (layer,) + (0,) * (nd - 1))

    def cur(width):
        return pl.BlockSpec((rows, width), lambda s: (jnp.minimum(s, steps - 1), 0))

    def prev(width):
        return pl.BlockSpec((rows, width), lambda s: (jnp.maximum(s - 1, 0), 0))

    def last(shape):
        return pl.BlockSpec((1,) + shape,
                            lambda s: (jnp.minimum(2 * s, nb - 1) // nblk,) + (0,) * len(shape))

    def tab(half):
        return pl.BlockSpec((BLK, 128), lambda s: (lax.rem(2 * s + half, nblk), 0))

    mix_params = ('conv_w', 'conv_b', 'lru_wa_bd', 'lru_wx_bd', 'lru_b_a', 'lru_b_x', 'lru_lambda', 'ret_norm')
    return pl.pallas_call(
        functools.partial(_mixout_body, layer=layer, pad=pad, nblk=nblk, final=final),
        grid=(steps + 1,),
        in_specs=[pl.BlockSpec(memory_space=pltpu.SMEM), cur(N_MIX), tab(0), tab(0), tab(1), tab(1),
                  prev(D_MODEL), prev(N_GATE)]
        + [lspec(p[k].shape) for k in mix_params]
        + [_layer_spec(p[k].shape, layer) for k in _OUT_WEIGHTS]
        + [pl.BlockSpec((1, D_MODEL), lambda s: (0, 0))],
        out_specs=[prev(D_MODEL), last((BLK, 128)), last((BLK, 128)), last((8, LRU_WIDTH)), last((8, LRU_WIDTH)),
                   last((RET_HEADS * RET_DK, RET_DV))],
        out_shape=[jax.ShapeDtypeStruct((n, D_MODEL), F32),
                   jax.ShapeDtypeStruct((bsz, BLK, 128), F32), jax.ShapeDtypeStruct((bsz, BLK, 128), F32),
                   jax.ShapeDtypeStruct((bsz, 8, LRU_WIDTH), F32), jax.ShapeDtypeStruct((bsz, 8, LRU_WIDTH), F32),
                   jax.ShapeDtypeStruct((bsz, RET_HEADS * RET_DK, RET_DV), F32)],
        scratch_shapes=[pltpu.VMEM((8, LRU_WIDTH), F32), pltpu.VMEM((2, rows, N_OC), BF16),
                        pltpu.VMEM((8, LRU_WIDTH), F32),
                        pltpu.VMEM((SWA_KV_HEADS, BLK, 128), BF16), pltpu.VMEM((SWA_KV_HEADS, BLK, 128), BF16),
                        pltpu.VMEM((RET_HEADS * RET_DK, RET_DV), F32),
                        pltpu.VMEM((RET_HEADS, BLK, BLK), F32), pltpu.VMEM((RET_HEADS, BLK, 128), F32),
                        pltpu.VMEM((2, BLK, 128), F32),
                        pltpu.VMEM((8, LRU_WIDTH), F32), pltpu.VMEM((8, LRU_WIDTH), F32),
                        pltpu.VMEM((RET_HEADS * RET_DK, RET_DV), F32)]
        + [pltpu.VMEM((LRU_WIDTH // 128, BLK, 128), F32)] * 3,
        compiler_params=pltpu.CompilerParams(dimension_semantics=("arbitrary",), vmem_limit_bytes=VMEM_LIMIT),
        name='mix_out',
    )(p['swa_sinks'], pm, cos, sin, cos, sin, x1, pg, *[p[k] for k in mix_params],
      *[p[k] for k in _OUT_WEIGHTS], p['final_norm'])


def _mixs_body(pm_ref, ck_ref, cv_ref, conv_ref, h0_ref, s_ref, cw_ref, cb_ref, wa_ref, wx_ref, ba_ref, bx_ref,
               lam_ref, inv_ref, gn_ref, sk_ref, nk_all, nv_all, ns_all,
               oc_ref, nk_ref, nv_ref, nconv_ref, nh_ref, ns_ref,
               qb, o8, qr_s, kr_s, v4_s, o2_s):
    gsz = G_SEQ
    nrow = 16
    rs = slice(None)

    @pl.when(pl.program_id(0) == 0)
    def _():
        qb[...] = jnp.zeros_like(qb)
        qr_s[...] = jnp.zeros_like(qr_s)
        kr_s[...] = jnp.zeros_like(kr_s)
        v4_s[...] = jnp.zeros_like(v4_s)

    lane = lax.broadcasted_iota(jnp.int32, (1, 128), 1)
    lo = lane < 64

    xa = pm_ref[rs,C_XA:C_XA + LRU_WIDTH]
    h1 = conv_ref[:, LRU_WIDTH:2 * LRU_WIDTH]
    h2 = conv_ref[:, 2 * LRU_WIDTH:3 * LRU_WIDTH]
    xc = cb_ref[...] + conv_ref[:, 0:LRU_WIDTH] * cw_ref[0:1, :]
    xc = xc + h1 * cw_ref[1:2, :]
    xc = xc + h2 * cw_ref[2:3, :]
    xc = xc + xa * cw_ref[3:4, :]
    nconv_ref[:, 0:LRU_WIDTH] = h1
    nconv_ref[:, LRU_WIDTH:2 * LRU_WIDTH] = h2
    nconv_ref[:, 2 * LRU_WIDTH:3 * LRU_WIDTH] = xa
    a, bt = _lru_gates(xc, wa_ref, wx_ref, ba_ref, bx_ref, lam_ref)
    hn = bt + a * h0_ref[...]
    nh_ref[...] = hn
    oc_ref[:, 0:LRU_WIDTH] = (hn * jax.nn.gelu(pm_ref[rs,C_YA:C_YA + LRU_WIDTH])).astype(BF16)

    nk_ref[:, 0:WINDOW - 1, :] = ck_ref[:, 1:WINDOW, :]
    nv_ref[:, 0:WINDOW - 1, :] = cv_ref[:, 1:WINDOW, :]
    for b in range(gsz):
        nk_ref[b, WINDOW - 1:WINDOW, :] = pm_ref[b:b + 1, C_KS:C_KS + 128]
        nv_ref[b, WINDOW - 1:WINDOW, :] = pm_ref[b:b + 1, C_VS:C_VS + 128]

    for r in range(SWA_HEADS):
        h = r // SWA_GROUP
        slab = pm_ref[rs,C_QS + (r // 2) * 128:C_QS + (r // 2 + 1) * 128]
        if r % 2 != h:
            slab = pltpu.roll(slab, 64, 1)
        qb[r * gsz:(r + 1) * gsz, :] = jnp.where(lo if h == 0 else jnp.logical_not(lo), slab, 0.0)

    ang = float(PAST_LEN) * inv_ref[...]
    cos2, sin_signed, first_half = _rope_operands(jnp.cos(ang), jnp.sin(ang))
    qc = _rope(pm_ref[rs,C_QR:C_QR + 256], cos2, sin_signed, first_half)
    kc = _rope(pm_ref[rs,C_KR:C_KR + 256], cos2, sin_signed, first_half) * RET_DK ** -0.5
    lane256 = lax.broadcasted_iota(jnp.int32, (1, RET_HEADS * RET_DK), 1)
    for r in range(RET_HEADS):
        hm = (lane256 >= r * RET_DK) & (lane256 < (r + 1) * RET_DK)
        qm = jnp.where(hm, qc, 0.0)
        km = jnp.where(hm, kc, 0.0)
        for c in range(2):
            qr_s[c, r * gsz:(r + 1) * gsz, :] = qm[:, c * 128:(c + 1) * 128]
            kr_s[c, r * gsz:(r + 1) * gsz, :] = km[:, c * 128:(c + 1) * 128]
        v4_s[r * gsz:(r + 1) * gsz, :] = pm_ref[rs,C_VR + r * RET_DV:C_VR + (r + 1) * RET_DV]

    srow = lax.broadcasted_iota(jnp.int32, (RET_HEADS * RET_DK, 1), 0)
    gcol = jnp.full((RET_HEADS * RET_DK, 1), math.exp(LOG_G[0]), F32)
    for r in range(1, RET_HEADS):
        gcol = jnp.where(srow >= r * RET_DK, math.exp(LOG_G[r]), gcol)
    sk = sk_ref[:, 0:1]

    lanes = 4

    def per_group(i, carry):
        seqs = [i * lanes + q for q in range(lanes)]
        rows = [pl.ds(b, nrow, stride=gsz) for b in seqs]
        s = [_dot_nt(qb[r, :].astype(BF16), nk_ref[b].astype(BF16)) * SWA_HEAD_DIM ** -0.5
             for b, r in zip(seqs, rows)]
        sb = [s_ref[b] for b in seqs]
        o2 = [_dot(qr_s[0, r, :].astype(BF16), x[0:128, :].astype(BF16))
              + _dot(qr_s[1, r, :].astype(BF16), x[128:256, :].astype(BF16)) for r, x in zip(rows, sb)]
        v4 = [v4_s[r, :].astype(BF16) for r in rows]
        kv = [jnp.concatenate([_dot_tn(kr_s[0, r, :].astype(BF16), v), _dot_tn(kr_s[1, r, :].astype(BF16), v)],
                              axis=0) for r, v in zip(rows, v4)]
        m = [jnp.maximum(jnp.max(x, axis=-1, keepdims=True), sk) for x in s]
        e = [jnp.exp(x - y) for x, y in zip(s, m)]
        den = [jnp.sum(x, axis=-1, keepdims=True) + jnp.exp(sk - y) for x, y in zip(e, m)]
        o = [_dot(x.astype(BF16), nv_ref[b].astype(BF16)) / d for x, b, d in zip(e, seqs, den)]
        for q, (b, r) in enumerate(zip(seqs, rows)):
            o2_s[r, :] = o2[q]
            ns_ref[b] = gcol * sb[q] + kv[q]
            o8[r, :] = o[q]
        return carry

    lax.fori_loop(0, gsz // lanes, per_group, 0)

    for sl in range(4):
        h = sl // 2
        ev = o8[(2 * sl) * gsz:(2 * sl + 1) * gsz, :]
        od = o8[(2 * sl + 1) * gsz:(2 * sl + 2) * gsz, :]
        if h != 0:
            ev = pltpu.roll(ev, 64, 1)
        if h != 1:
            od = pltpu.roll(od, 64, 1)
        oc_ref[:, LRU_WIDTH + sl * 128:LRU_WIDTH + (sl + 1) * 128] = jnp.where(lo, ev, od).astype(BF16)

    prod = qc * kc
    p_hi = prod.astype(BF16)
    p_lo = (prod - p_hi.astype(F32)).astype(BF16)
    er = lax.broadcasted_iota(jnp.int32, (RET_HEADS * RET_DK, RET_HEADS * RET_DV), 0) // RET_DK
    ec = lax.broadcasted_iota(jnp.int32, (RET_HEADS * RET_DK, RET_HEADS * RET_DV), 1) // RET_DV
    expand = jnp.where(er == ec, 1.0, 0.0).astype(BF16)
    qk = _dot(p_hi, expand) + _dot(p_lo, expand)
    for r in range(RET_HEADS):
        cs = slice(r * RET_DV, (r + 1) * RET_DV)
        o = qk[:, cs] * pm_ref[rs,C_VR + r * RET_DV:C_VR + (r + 1) * RET_DV]
        o = o + o2_s[r * gsz:(r + 1) * gsz, :] * math.exp(LOG_G[r])
        oc = _group_norm_gate(o, gn_ref[:, cs], pm_ref[rs,C_GR + r * RET_DV:C_GR + (r + 1) * RET_DV])
        oc_ref[:, 2 * LRU_WIDTH + r * RET_DV:2 * LRU_WIDTH + (r + 1) * RET_DV] = oc.astype(BF16)


def _call_mix_sample(pm, ck, cv, conv, h0, sret, stacked, p, layer):
    nseq = pm.shape[0]
    depth = ck.shape[0]
    whole = pl.BlockSpec(memory_space=pl.ANY)
    gsz = G_SEQ

    def lspec(shape):
        nd = len(shape)
        return pl.BlockSpec((None,) + tuple(shape[1:]), lambda i: (layer,) + (0,) * (nd - 1))

    def seq2(width):
        return pl.BlockSpec((gsz, width), lambda i: (i, 0))

    def seq3(layered, d1, d2):
        if layered:
            return pl.BlockSpec((None, gsz, d1, d2), lambda i: (layer, i, 0, 0))
        return pl.BlockSpec((gsz, d1, d2), lambda i: (i, 0, 0))

    sdim = RET_HEADS * RET_DK
    return pl.pallas_call(
        _mixs_body,
        grid=(nseq // gsz,),
        in_specs=[seq2(N_MIX), seq3(True, WINDOW, 128), seq3(True, WINDOW, 128),
                  pl.BlockSpec((None, gsz, 3 * LRU_WIDTH), lambda i: (layer, i, 0)),
                  pl.BlockSpec((None, gsz, LRU_WIDTH), lambda i: (layer, i, 0)),
                  seq3(True, sdim, RET_DV),
                  lspec(p['conv_w'].shape), lspec(p['conv_b'].shape), lspec(p['lru_wa_bd'].shape),
                  lspec(p['lru_wx_bd'].shape), lspec(p['lru_b_a'].shape), lspec(p['lru_b_x'].shape),
                  lspec(p['lru_lambda'].shape), pl.BlockSpec((1, 128), lambda i: (0, 0)),
                  lspec(p['ret_norm'].shape), lspec(p['sinks16'].shape), whole, whole, whole],
        out_specs=[seq2(N_OC), seq3(True, WINDOW, 128), seq3(True, WINDOW, 128), seq2(3 * LRU_WIDTH),
                   seq2(LRU_WIDTH), seq3(True, sdim, RET_DV)],
        out_shape=[jax.ShapeDtypeStruct((nseq, N_OC), BF16),
                   jax.ShapeDtypeStruct((depth, nseq, WINDOW, 128), F32),
                   jax.ShapeDtypeStruct((depth, nseq, WINDOW, 128), F32),
                   jax.ShapeDtypeStruct((nseq, 3 * LRU_WIDTH), F32), jax.ShapeDtypeStruct((nseq, LRU_WIDTH), F32),
                   jax.ShapeDtypeStruct((depth, nseq, sdim, RET_DV), F32)],
        input_output_aliases={16: 1, 17: 2, 18: 5},
        scratch_shapes=[pltpu.VMEM((16 * gsz, 128), F32), pltpu.VMEM((16 * gsz, 128), F32),
                        pltpu.VMEM((2, 16 * gsz, 128), F32), pltpu.VMEM((2, 16 * gsz, 128), F32),
                        pltpu.VMEM((16 * gsz, 128), F32), pltpu.VMEM((16 * gsz, 128), F32)],
        compiler_params=pltpu.CompilerParams(dimension_semantics=("arbitrary",), vmem_limit_bytes=VMEM_LIMIT),
        name='mix_sample',
    )(pm, ck, cv, conv, h0, sret, p['conv_w'], p['conv_b'], p['lru_wa_bd'], p['lru_wx_bd'], p['lru_b_a'],
      p['lru_b_x'], p['lru_lambda'], p['rope_inv'], p['ret_norm'], p['sinks16'], *stacked)


def _block_diag(w):
    depth = w.shape[0]
    w = w.reshape(depth, 2, 4, LRU_BW, LRU_BW)
    eye = jnp.eye(4, dtype=w.dtype)
    return jnp.einsum('lsncd,nm->lsncmd', w, eye).reshape(depth, 2, 4 * LRU_BW, 4 * LRU_BW)


def kernel(x_prompt, x_sample, cache_swa_k, cache_swa_v, state_conv, state_lru, state_ret, meta_tokens, ffn1_norm,
           ffn1_w_gu, ffn1_w_down, mix_norm, w_in, conv_w, conv_b, lru_w_a, lru_b_a, lru_w_x, lru_b_x, lru_lambda,
           swa_sinks, ret_norm, w_branch_a, w_branch_b, w_branch_c, w_out, ffn2_norm, ffn2_w_gu, ffn2_w_down,
           final_norm):
    depth = w_in.shape[0]
    bsz, seq, _ = x_prompt.shape
    nseq = x_sample.shape[0]
    buf = cache_swa_k.shape[2]
    assert buf == WINDOW == BLK and x_sample.shape[1] == 1 and nseq % G_SEQ == 0
    t = seq + N_META
    pad = (-t) % BLK
    tp = t + pad
    assert (bsz * tp) % TM_DENSE == 0

    def row(v):
        return v.reshape(depth, 1, -1).astype(F32)

    def bf16(w):
        return w.astype(BF16)

    half = jnp.arange(128) % (RET_DK // 2)
    p = {
        'ffn1_norm': row(ffn1_norm), 'ffn1_w_gu': bf16(ffn1_w_gu), 'ffn1_w_down': bf16(ffn1_w_down),
        'mix_norm': row(mix_norm), 'w_mix': bf16(w_in[:, :, :N_MIX]), 'w_gate': bf16(w_in[:, :, N_MIX:]),
        'conv_w': conv_w.astype(F32), 'conv_b': row(conv_b),
        'lru_wa_bd': _block_diag(lru_w_a).astype(BF16), 'lru_wx_bd': _block_diag(lru_w_x).astype(BF16),
        'lru_b_a': row(lru_b_a), 'lru_b_x': row(lru_b_x), 'lru_lambda': row(lru_lambda),
        'swa_sinks': swa_sinks.astype(F32),
        'sinks16': jnp.pad(jnp.broadcast_to(swa_sinks.astype(F32)[:, :, None], (depth, SWA_HEADS, 128)),
                           ((0, 0), (0, 16 - SWA_HEADS), (0, 0))),
        'ret_norm': row(ret_norm),
        'w_branch_a': bf16(w_branch_a), 'w_branch_b': bf16(w_branch_b),
        'w_branch_c': bf16(w_branch_c), 'w_out': bf16(w_out),
        'ffn2_norm': row(ffn2_norm), 'ffn2_w_gu': bf16(ffn2_w_gu), 'ffn2_w_down': bf16(ffn2_w_down),
        'final_norm': final_norm.reshape(1, D_MODEL).astype(F32),
        'rope_inv': (ROPE_BASE ** (-half.astype(F32) / (RET_DK // 2))).reshape(1, 128),
    }

    meta = jnp.broadcast_to(meta_tokens.astype(F32)[None], (bsz, N_META, D_MODEL))
    xp = jnp.concatenate([jnp.zeros((bsz, pad, D_MODEL), F32), meta, x_prompt], axis=1).reshape(bsz * tp, D_MODEL)
    xs = x_sample.reshape(nseq, D_MODEL)
    ck = cache_swa_k.reshape(depth, nseq, buf, 128)
    cv = cache_swa_v.reshape(depth, nseq, buf, 128)
    conv = state_conv.reshape(depth, nseq, 3 * LRU_WIDTH)
    sret = state_ret.reshape(depth, nseq, RET_HEADS * RET_DK, RET_DV)
    cos, sin = _call_ropetab(p['rope_inv'], tp, pad)

    outs_p = [[] for _ in range(5)]
    outs_s = [[], []]
    stacked = (jnp.zeros(ck.shape, F32), jnp.zeros(cv.shape, F32), jnp.zeros(sret.shape, F32))
    for layer in range(depth):
        final = layer == depth - 1
        x1, pm, pg = _call_in(xp, p, layer, TM_DENSE)
        xp, kl, vl, cl, hl, sl = _call_mix_out(pm, cos, sin, x1, pg, p, layer, bsz, pad, final)
        for acc, o in zip(outs_p, (kl.reshape(bsz, buf, SWA_KV_HEADS, SWA_HEAD_DIM),
                                   vl.reshape(bsz, buf, SWA_KV_HEADS, SWA_HEAD_DIM),
                                   cl[:, 8 - (CONV_WIDTH - 1):, :], hl[:, 7, :],
                                   sl.reshape(bsz, RET_HEADS, RET_DK, RET_DV))):
            acc.append(o)

        x1, pm, pg = _call_in(xs, p, layer, nseq)
        oc, nk, nv, nc, nh, ns = _call_mix_sample(pm, ck, cv, conv, state_lru, sret, stacked, p, layer)
        stacked = (nk, nv, ns)
        xs = _call_out(x1, pg, oc, p, layer, nseq, final)
        outs_s[0].append(nc.reshape(nseq, CONV_WIDTH - 1, LRU_WIDTH))
        outs_s[1].append(nh)

    yp = xp.reshape(bsz, tp, D_MODEL)[:, pad + N_META:]
    ys = xs.reshape(nseq, 1, D_MODEL)
    nk, nv, ns = stacked
    return ((yp, ys) + tuple(jnp.stack(a) for a in outs_p)
            + (nk.reshape(depth, nseq, buf, SWA_KV_HEADS, SWA_HEAD_DIM),
               nv.reshape(depth, nseq, buf, SWA_KV_HEADS, SWA_HEAD_DIM),
               jnp.stack(outs_s[0]), jnp.stack(outs_s[1]),
               ns.reshape(depth, nseq, RET_HEADS, RET_DK, RET_DV)))
```

```python
import functools
import math

import jax
import jax.numpy as jnp
from jax import lax
from jax.experimental import pallas as pl
from jax.experimental.pallas import tpu as pltpu

F32 = jnp.float32
BF16 = jnp.bfloat16

D_MODEL = 1024
D_FF = 2048
N_META = 16
EPS = 1e-6
LRU_WIDTH = 512
LRU_BLOCKS = 8
LRU_BW = 64
CONV_WIDTH = 4
LRU_C = 8.0
SWA_HEAD_DIM = 64
SWA_HEADS = 8
SWA_KV_HEADS = 2
SWA_GROUP = 4
WINDOW = 128
RET_DK = 64
RET_DV = 128
RET_HEADS = 4
ROPE_BASE = 10000.0
GN_EPS = 1e-5
PAST_LEN = 8192

BLK = 128
N_MIX = 3328
N_GATE = 3 * D_MODEL
C_XA, C_YA, C_QS, C_KS, C_VS, C_QR, C_KR, C_VR, C_GR = 0, 512, 1024, 1536, 1664, 1792, 2048, 2304, 2816
N_OC = 3 * LRU_WIDTH
LOG_G = tuple(math.log1p(-(2.0 ** (-5.0 - h))) for h in range(RET_HEADS))

TM_DENSE = 256
G_SEQ = 16
VMEM_LIMIT = 56 * 1024 * 1024


def _dot(a, b):
    return jnp.dot(a, b, preferred_element_type=F32)


def _dot_nt(a, b):
    return lax.dot_general(a, b, (((1,), (1,)), ((), ())), preferred_element_type=F32)


def _dot_tn(a, b):
    return lax.dot_general(a, b, (((0,), (0,)), ((), ())), preferred_element_type=F32)


def _rms(x, g):
    return x * lax.rsqrt(jnp.mean(x * x, axis=-1, keepdims=True) + EPS) * g


def _softplus(x):
    return jnp.maximum(x, 0.0) + jnp.log1p(jnp.exp(-jnp.abs(x)))


def _wt(ref, rows=None, cols=None):
    r = slice(None) if rows is None else slice(*rows)
    c = slice(None) if cols is None else slice(*cols)
    return ref[r, c]


def _swiglu(u, wgu_ref, wd_ref):
    gu = _dot(u, _wt(wgu_ref))
    act = (jax.nn.silu(gu[:, :D_FF]) * gu[:, D_FF:]).astype(BF16)
    return _dot(act, _wt(wd_ref))


def _in_body(*refs, nblk=None):
    if nblk is None:
        x_ref, n1_ref, wgu_ref, wd_ref, n2_ref, wm_ref, wg_ref, x1_ref, pm_ref, pg_ref = refs
        x = x_ref[...]
    else:
        xa_ref, xb_ref, head_ref, n1_ref, wgu_ref, wd_ref, n2_ref, wm_ref, wg_ref, x1_ref, pm_ref, pg_ref = refs
        first = 2 * pl.program_id(0)
        x = jnp.concatenate([jnp.where(lax.rem(first + h, nblk) == 0, head_ref[...], ref[...])
                             for h, ref in enumerate((xa_ref, xb_ref))], axis=0)
    x1 = x + 0.5 * _swiglu(_rms(x, n1_ref[...]).astype(BF16), wgu_ref, wd_ref)
    x1_ref[...] = x1
    u2 = _rms(x1, n2_ref[...]).astype(BF16)
    pm_ref[...] = _dot(u2, _wt(wm_ref))
    pm_ref[:, C_YA:C_YA + LRU_WIDTH] = jax.nn.gelu(pm_ref[:, C_YA:C_YA + LRU_WIDTH])
    pm_ref[:, C_GR:N_MIX] = jax.nn.silu(pm_ref[:, C_GR:N_MIX])
    pg_ref[...] = jax.nn.sigmoid(_dot(u2, _wt(wg_ref)))


def _merge_out(x1, pg, oc, wa_ref, wb_ref, wc_ref, wo_ref, n_ref, wgu_ref, wd_ref, fn_ref, final):
    g = pg
    merged = (g[:, :D_MODEL] * _dot(oc[:, :LRU_WIDTH], _wt(wa_ref))
              + g[:, D_MODEL:2 * D_MODEL] * _dot(oc[:, LRU_WIDTH:2 * LRU_WIDTH], _wt(wb_ref))
              + g[:, 2 * D_MODEL:] * _dot(oc[:, 2 * LRU_WIDTH:], _wt(wc_ref)))
    x2 = x1 + _dot(merged.astype(BF16), _wt(wo_ref))
    x3 = x2 + 0.5 * _swiglu(_rms(x2, n_ref[...]).astype(BF16), wgu_ref, wd_ref)
    if final:
        x3 = _rms(x3, fn_ref[...])
    return x3


FF_CHUNK = 256


def _merge_out_pieces(x1_ref, pg_ref, oc, wa_ref, wb_ref, wc_ref, wo_ref, n_ref, wgu_ref, wd_ref, fn_ref, out_ref,
                      final):
    merged = None
    for b, w_ref in enumerate((wa_ref, wb_ref, wc_ref)):
        for c in range(0, D_MODEL, 2 * FF_CHUNK):
            g = pg_ref[:, b * D_MODEL + c:b * D_MODEL + c + 2 * FF_CHUNK]
            part = g * _dot(oc[:, b * LRU_WIDTH:(b + 1) * LRU_WIDTH], _wt(w_ref, cols=(c, c + 2 * FF_CHUNK)))
            if b == 0:
                merged = [part] if c == 0 else merged + [part]
            else:
                merged[c // (2 * FF_CHUNK)] = merged[c // (2 * FF_CHUNK)] + part
            yield
    merged = jnp.concatenate(merged, axis=1).astype(BF16)
    x2 = []
    for c in range(0, D_MODEL, FF_CHUNK):
        x2.append(x1_ref[:, c:c + FF_CHUNK] + _dot(merged, _wt(wo_ref, cols=(c, c + FF_CHUNK))))
        yield
    x2 = jnp.concatenate(x2, axis=1)
    u = _rms(x2, n_ref[...]).astype(BF16)
    yield
    y = None
    for c in range(D_FF // FF_CHUNK):
        lo, hi = c * FF_CHUNK, (c + 1) * FF_CHUNK
        gate = _dot(u, _wt(wgu_ref, cols=(lo, hi)))
        yield
        up = _dot(u, _wt(wgu_ref, cols=(D_FF + lo, D_FF + hi)))
        yield
        part = _dot((jax.nn.silu(gate) * up).astype(BF16), _wt(wd_ref, rows=(lo, hi)))
        y = part if y is None else y + part
        yield
    x3 = x2 + 0.5 * y
    out_ref[...] = _rms(x3, fn_ref[...]) if final else x3
    yield


def _alternate(first, second, ratio):
    live = [True, True]
    while any(live):
        for idx, (gen, count) in enumerate(((first, 1), (second, ratio))):
            for _ in range(count):
                if live[idx] and next(gen, StopIteration) is StopIteration:
                    live[idx] = False


def _out_body(x1_ref, pg_ref, oc_ref, wa_ref, wb_ref, wc_ref, wo_ref, n_ref, wgu_ref, wd_ref, fn_ref, out_ref,
              *, final):
    out_ref[...] = _merge_out(x1_ref[...], pg_ref[...], oc_ref[...], wa_ref, wb_ref, wc_ref, wo_ref, n_ref,
                              wgu_ref, wd_ref, fn_ref, final)


def _layer_spec(shape, layer):
    nd = len(shape)
    return pl.BlockSpec((None,) + tuple(shape[1:]), lambda *_: (layer,) + (0,) * (nd - 1),
                        pipeline_mode=pl.Buffered(1))


def _row_spec(tm, width):
    return pl.BlockSpec((tm, width), lambda i: (i, 0))


def _dense_params():
    return pltpu.CompilerParams(dimension_semantics=("arbitrary",), vmem_limit_bytes=VMEM_LIMIT)


def _prompt_block(f, nblk):
    return f - f // nblk - 1


def _call_in(x, p, layer, tm, head=None, nblk=None):
    if head is None:
        n = x.shape[0]
        x_specs, x_args = [_row_spec(tm, D_MODEL)], (x,)
    else:
        assert tm == 2 * BLK
        n = (x.shape[0] // BLK + x.shape[0] // BLK // (nblk - 1)) * BLK
        last = x.shape[0] // BLK - 1

        def half(h):
            return pl.BlockSpec((BLK, D_MODEL),
                                lambda i: (jnp.clip(_prompt_block(2 * i + h, nblk), 0, last), 0))

        x_specs = [half(0), half(1), pl.BlockSpec((BLK, D_MODEL), lambda i: (0, 0))]
        x_args = (x, x, head)
    return pl.pallas_call(
        functools.partial(_in_body, nblk=None if head is None else nblk),
        grid=(n // tm,),
        in_specs=x_specs + [
                  _layer_spec(p['ffn1_norm'].shape, layer), _layer_spec(p['ffn1_w_gu'].shape, layer),
                  _layer_spec(p['ffn1_w_down'].shape, layer), _layer_spec(p['mix_norm'].shape, layer),
                  _layer_spec(p['w_mix'].shape, layer), _layer_spec(p['w_gate'].shape, layer)],
        out_specs=[_row_spec(tm, D_MODEL), _row_spec(tm, N_MIX), _row_spec(tm, N_GATE)],
        out_shape=[jax.ShapeDtypeStruct((n, D_MODEL), F32), jax.ShapeDtypeStruct((n, N_MIX), F32),
                   jax.ShapeDtypeStruct((n, N_GATE), F32)],
        compiler_params=_dense_params(),
        name='layer_in',
    )(*x_args, p['ffn1_norm'], p['ffn1_w_gu'], p['ffn1_w_down'], p['mix_norm'], p['w_mix'], p['w_gate'])


_OUT_WEIGHTS = ('w_branch_a', 'w_branch_b', 'w_branch_c', 'w_out', 'ffn2_norm', 'ffn2_w_gu', 'ffn2_w_down')


def _call_out(x1, pg, oc, p, layer, tm, final):
    n = x1.shape[0]
    return pl.pallas_call(
        functools.partial(_out_body, final=final),
        grid=(n // tm,),
        in_specs=[_row_spec(tm, D_MODEL), _row_spec(tm, N_GATE), _row_spec(tm, N_OC)]
        + [_layer_spec(p[k].shape, layer) for k in _OUT_WEIGHTS]
        + [pl.BlockSpec((1, D_MODEL), lambda i: (0, 0))],
        out_specs=_row_spec(tm, D_MODEL),
        out_shape=jax.ShapeDtypeStruct((n, D_MODEL), F32),
        compiler_params=_dense_params(),
        name='layer_out',
    )(x1, pg, oc, *[p[k] for k in _OUT_WEIGHTS], p['final_norm'])


def _drain(pieces):
    while True:
        try:
            next(pieces)
        except StopIteration as done:
            return done.value


def _lru_gate_pieces(xc, wa_ref, wx_ref, ba_ref, bx_ref, lam_ref):
    xcb = xc.astype(BF16)
    half = LRU_WIDTH // 2
    rpre = jnp.concatenate([_dot(xcb[:, :half], wa_ref[0]), _dot(xcb[:, half:], wa_ref[1])], axis=1)
    yield
    ipre = jnp.concatenate([_dot(xcb[:, :half], wx_ref[0]), _dot(xcb[:, half:], wx_ref[1])], axis=1)
    yield
    r = jax.nn.sigmoid(rpre + ba_ref[...])
    i = jax.nn.sigmoid(ipre + bx_ref[...])
    yield
    log_a = -LRU_C * r * _softplus(-lam_ref[...])
    a = jnp.exp(log_a)
    yield
    z = -jnp.tanh(log_a) * (a * a + 1.0)
    return a, jnp.where(z > 0.0, z * lax.rsqrt(z), 0.0) * (i * xc)


def _lru_gates(*args):
    return _drain(_lru_gate_pieces(*args))


def _rope_operands(cos, sin):
    lane = lax.broadcasted_iota(jnp.int32, (1, 4 * RET_DK), 1)
    first_half = (lane & (RET_DK - 1)) < RET_DK // 2
    cos2 = jnp.concatenate([cos, cos], axis=1)
    sin2 = jnp.concatenate([sin, sin], axis=1)
    return cos2, jnp.where(first_half, -sin2, sin2), first_half


def _rope(x, cos2, sin_signed, first_half):
    swapped = jnp.where(first_half, pltpu.roll(x, 4 * RET_DK - RET_DK // 2, 1), pltpu.roll(x, RET_DK // 2, 1))
    return x * cos2 + swapped * sin_signed


def _group_norm_gate(o, gain, gate):
    mu = jnp.mean(o, axis=-1, keepdims=True)
    d = o - mu
    var = jnp.mean(d * d, axis=-1, keepdims=True)
    return d * lax.rsqrt(var + GN_EPS) * gain * gate


def _ropetab_body(inv_ref, cos_ref, sin_ref, *, rows_per_step, pad):
    rows = lax.broadcasted_iota(jnp.int32, (rows_per_step, 1), 0)
    pos = (pl.program_id(0) * rows_per_step + rows - pad).astype(F32)
    ang = pos * inv_ref[...]
    cos_ref[...] = jnp.cos(ang)
    sin_ref[...] = jnp.sin(ang)


def _call_ropetab(inv, tp, pad):
    nblk = tp // BLK
    rps = BLK * max(d for d in (8, 5, 4, 2, 1) if nblk % d == 0)
    spec = pl.BlockSpec((rps, 128), lambda i: (i, 0))
    return pl.pallas_call(
        functools.partial(_ropetab_body, rows_per_step=rps, pad=pad),
        grid=(tp // rps,),
        in_specs=[pl.BlockSpec((1, 128), lambda i: (0, 0))],
        out_specs=[spec, spec],
        out_shape=[jax.ShapeDtypeStruct((tp, 128), F32), jax.ShapeDtypeStruct((tp, 128), F32)],
        name='rope_tables',
    )(inv)


def _block_masks(j, pad):
    lo = lax.broadcasted_iota(jnp.int32, (1, 128), 1) < 64
    rows = lax.broadcasted_iota(jnp.int32, (BLK, 1), 0)
    return lo, (j * BLK + rows) >= pad


def _mix_lru(pm_ref, cw_ref, cb_ref, wa_ref, wx_ref, ba_ref, bx_ref, lam_ref, xext, ocs, hcar, a_s, b_s, h_s,
             *, pad, j, cur, r0):
    rs = slice(r0, r0 + BLK)
    fresh = j == 0
    _, valid = _block_masks(j, pad)
    xa = jnp.where(valid, pm_ref[rs,C_XA:C_XA + LRU_WIDTH], 0.0)
    hist = jnp.where(fresh, 0.0, xext[...])
    row8 = lax.broadcasted_iota(jnp.int32, (8, 1), 0)
    xc = cb_ref[...]
    for tap in range(CONV_WIDTH - 1):
        k = CONV_WIDTH - 1 - tap
        sh = pltpu.roll(xa, k, 0)
        top = jnp.where(row8 < k, pltpu.roll(hist, k, 0), sh[0:8, :])
        xc = xc + jnp.concatenate([top, sh[8:, :]], axis=0) * cw_ref[tap:tap + 1, :]
        yield
    xc = xc + xa * cw_ref[CONV_WIDTH - 1:CONV_WIDTH, :]
    xext[...] = xa[BLK - 8:BLK, :]
    yield
    a, bt = yield from _lru_gate_pieces(xc, wa_ref, wx_ref, ba_ref, bx_ref, lam_ref)
    bt = jnp.where(valid, bt, 0.0)
    yield
    ngrp = BLK // 8
    for c in range(LRU_WIDTH // 128):
        a_s[c] = a[:, c * 128:(c + 1) * 128]
        b_s[c] = bt[:, c * 128:(c + 1) * 128]
    yield
    h_in = jnp.where(fresh, 0.0, hcar[7:8, :])
    for c in range(LRU_WIDTH // 128):
        prods, sums = [], []
        for r in range(8):
            ar = a_s[c, pl.ds(r, ngrp, stride=8), :]
            br = b_s[c, pl.ds(r, ngrp, stride=8), :]
            prods.append(ar if r == 0 else ar * prods[-1])
            sums.append(br if r == 0 else ar * sums[-1] + br)
        carry = h_in[:, c * 128:(c + 1) * 128]
        carries = []
        for g in range(ngrp):
            carries.append(carry)
            carry = prods[7][g:g + 1, :] * carry + sums[7][g:g + 1, :]
        carries = jnp.concatenate(carries, axis=0)
        for r in range(8):
            h_s[c, pl.ds(r, ngrp, stride=8), :] = prods[r] * carries + sums[r]
        yield
    h = jnp.concatenate([h_s[c] for c in range(LRU_WIDTH // 128)], axis=1)
    hcar[...] = h[BLK - 8:BLK, :]
    o_a = h * pm_ref[rs,C_YA:C_YA + LRU_WIDTH]
    ocs[cur, rs,0:LRU_WIDTH] = o_a.astype(BF16)
    yield


def _mix_swa(sinks_ref, pm_ref, ocs, kprev, vprev, *, layer, pad, j, cur, r0):
    rs = slice(r0, r0 + BLK)
    lo, _ = _block_masks(j, pad)
    k = pm_ref[rs,C_KS:C_KS + 128]
    v = pm_ref[rs,C_VS:C_VS + 128]
    k_sw = pltpu.roll(k, 64, 1)
    v_sw = pltpu.roll(v, 64, 1)
    kdup = (jnp.where(lo, k, k_sw).astype(BF16), jnp.where(lo, k_sw, k).astype(BF16))
    vdup = (jnp.where(lo, v, v_sw).astype(BF16), jnp.where(lo, v_sw, v).astype(BF16))
    row4 = lax.broadcasted_iota(jnp.int32, (SWA_GROUP * BLK, 1), 0)
    t4 = row4 & (BLK - 1)
    col = lax.broadcasted_iota(jnp.int32, (1, 2 * BLK), 1)
    ok = ((j - 1) * BLK + col >= pad) & (col > t4) & (col <= t4 + BLK)
    for h in range(SWA_KV_HEADS):
        parts = []
        for g in range(SWA_GROUP):
            head = SWA_GROUP * h + g
            slab = pm_ref[rs,C_QS + (head // 2) * 128:C_QS + (head // 2 + 1) * 128]
            parts.append(jnp.where(lo if head % 2 == 0 else jnp.logical_not(lo), slab, 0.0))
        qst = jnp.concatenate(parts, axis=0).astype(BF16)
        kcat = jnp.concatenate([kprev[h], kdup[h]], axis=0)
        vcat = jnp.concatenate([vprev[h], vdup[h]], axis=0)
        yield
        sc = jnp.where(ok, _dot_nt(qst, kcat), -jnp.inf)
        yield
        sk = jnp.full((SWA_GROUP * BLK, 1), sinks_ref[layer, SWA_GROUP * h], F32)
        for g in range(1, SWA_GROUP):
            sk = jnp.where(row4 >= g * BLK, sinks_ref[layer, SWA_GROUP * h + g], sk)
        m = jnp.maximum(jnp.max(sc, axis=-1, keepdims=True), sk)
        yield
        e = jnp.exp(sc - m)
        yield
        den = jnp.sum(e, axis=-1, keepdims=True) + jnp.exp(sk - m)
        yield
        o = _dot(e.astype(BF16), vcat) / den
        yield
        for sl in range(2):
            ge = 2 * sl
            slab = jnp.where(lo, o[ge * BLK:(ge + 1) * BLK, :], o[(ge + 1) * BLK:(ge + 2) * BLK, :])
            c0 = LRU_WIDTH + (2 * h + sl) * 128
            ocs[cur, rs,c0:c0 + 128] = slab.astype(BF16)
        kprev[h] = kdup[h]
        vprev[h] = vdup[h]
        yield


def _mix_ret(pm_ref, cos_ref, sin_ref, gn_ref, ocs, state, dec_t, cross_t, kdec_t, *, pad, j, cur, r0):
    rs = slice(r0, r0 + BLK)
    fresh = j == 0
    lo, valid = _block_masks(j, pad)
    cos2, sin_signed, first_half = _rope_operands(cos_ref[...], sin_ref[...])
    qc = _rope(pm_ref[rs,C_QR:C_QR + 256], cos2, sin_signed, first_half)
    yield
    kc = _rope(pm_ref[rs,C_KR:C_KR + 256], cos2, sin_signed, first_half)
    kc = jnp.where(valid, kc, 0.0)
    yield
    st = [jnp.where(fresh, 0.0, state[sl * 128:(sl + 1) * 128, :]) for sl in range(2)]
    yield
    upd = [None, None]
    for h in range(RET_HEADS):
        sl = h // 2
        half = lo if h % 2 == 0 else jnp.logical_not(lo)
        qm = jnp.where(half, qc[:, sl * 128:(sl + 1) * 128], 0.0).astype(BF16)
        kslab = kc[:, sl * 128:(sl + 1) * 128]
        vh = jnp.where(valid, pm_ref[rs,C_VR + h * RET_DV:C_VR + (h + 1) * RET_DV], 0.0).astype(BF16)
        sc = (_dot_nt(qm, kslab.astype(BF16)) * dec_t[h]).astype(BF16)
        yield
        o = _dot(sc, vh)
        o = o + _dot(qm, st[sl].astype(BF16)) * cross_t[h]
        yield
        c0 = 2 * LRU_WIDTH + h * RET_DV
        oc = _group_norm_gate(o, gn_ref[:, h * RET_DV:(h + 1) * RET_DV],
                              pm_ref[rs,C_GR + h * RET_DV:C_GR + (h + 1) * RET_DV])
        ocs[cur, rs,c0:c0 + RET_DV] = oc.astype(BF16)
        yield
        km = jnp.where(half, kslab * kdec_t[sl], 0.0).astype(BF16)
        u = _dot_tn(km, vh)
        upd[sl] = u if upd[sl] is None else upd[sl] + u
        yield
    srow = lax.broadcasted_iota(jnp.int32, (128, 1), 0)
    for sl in range(2):
        gcol = jnp.where(srow < RET_DK, math.exp(BLK * LOG_G[2 * sl]), math.exp(BLK * LOG_G[2 * sl + 1]))
        state[sl * 128:(sl + 1) * 128, :] = gcol * st[sl] + upd[sl]


def _mixout_body(sinks_ref, pm_ref, cos0_ref, sin0_ref, cos1_ref, sin1_ref, x1_ref, pg_ref,
                 cw_ref, cb_ref, wa_ref, wx_ref, ba_ref, bx_ref, lam_ref, gn_ref,
                 wba_ref, wbb_ref, wbc_ref, wo_ref, n2_ref, wgu_ref, wd_ref, fn_ref,
                 out_ref, klast_ref, vlast_ref, conv_ref, hlast_ref, sret_ref,
                 xext, ocs, hcar, kprev, vprev, state, dec_t, cross_t, kdec_t, snap_x, snap_h, snap_s, a_s, b_s, h_s,
                 *, layer, pad, nblk, final):
    s = pl.program_id(0)

    @pl.when(s == 0)
    def _():
        lo = lax.broadcasted_iota(jnp.int32, (1, 128), 1) < 64
        ti = lax.broadcasted_iota(jnp.int32, (BLK, 1), 0).astype(F32)
        tj = lax.broadcasted_iota(jnp.int32, (1, BLK), 1).astype(F32)
        diff = ti - tj
        for h in range(RET_HEADS):
            dec_t[h] = jnp.exp(jnp.where(diff >= 0, diff * LOG_G[h], -jnp.inf))
            cross_t[h] = jnp.broadcast_to(jnp.exp((ti + 1.0) * LOG_G[h]), (BLK, 128))
        for sl in range(2):
            kdec_t[sl] = jnp.where(lo, jnp.exp((BLK - 1.0 - ti) * LOG_G[2 * sl]),
                                   jnp.exp((BLK - 1.0 - ti) * LOG_G[2 * sl + 1]))
        for ref in (ocs, xext, hcar, kprev, vprev, state):
            ref[...] = jnp.zeros_like(ref)

    cur = lax.rem(s, 2)
    j0 = lax.rem(2 * s, nblk)
    j1 = lax.rem(2 * s + 1, nblk)
    dense = _merge_out_pieces(x1_ref, pg_ref, ocs[lax.rem(s + 1, 2)], wba_ref, wbb_ref, wbc_ref, wo_ref, n2_ref,
                              wgu_ref, wd_ref, fn_ref, out_ref, final)

    def lru(j, r0):
        return _mix_lru(pm_ref, cw_ref, cb_ref, wa_ref, wx_ref, ba_ref, bx_ref, lam_ref, xext, ocs, hcar,
                        a_s, b_s, h_s, pad=pad, j=j, cur=cur, r0=r0)

    def swa(j, r0):
        return _mix_swa(sinks_ref, pm_ref, ocs, kprev, vprev, layer=layer, pad=pad, j=j, cur=cur, r0=r0)

    def ret(j, r0, cos_ref, sin_ref):
        return _mix_ret(pm_ref, cos_ref, sin_ref, gn_ref, ocs, state, dec_t, cross_t, kdec_t,
                        pad=pad, j=j, cur=cur, r0=r0)

    def mixers():
        yield from lru(j0, 0)
        yield from swa(j0, 0)
        yield from ret(j0, 0, cos0_ref, sin0_ref)
        snap_x[...] = xext[...]
        snap_h[...] = hcar[...]
        snap_s[...] = state[...]
        yield from lru(j1, BLK)
        yield from swa(j1, BLK)
        yield from ret(j1, BLK, cos1_ref, sin1_ref)

    _alternate(dense, mixers(), 3)

    def write_state(r0, conv, hl, st):
        klast_ref[0] = pm_ref[r0:r0 + BLK, C_KS:C_KS + 128]
        vlast_ref[0] = pm_ref[r0:r0 + BLK, C_VS:C_VS + 128]
        conv_ref[0] = conv[...]
        hlast_ref[0] = hl[...]
        sret_ref[0] = st[...]

    @pl.when(j0 == nblk - 1)
    def _():
        write_state(0, snap_x, snap_h, snap_s)

    @pl.when(j1 == nblk - 1)
    def _():
        write_state(BLK, xext, hcar, state)


def _call_mix_out(pm, cos, sin, x1, pg, p, layer, bsz, pad, final):
    n = pm.shape[0]
    nb = n // BLK
    nblk = nb // bsz
    assert nb % 2 == 0 and nblk >= 2
    steps = nb // 2
    rows = 2 * BLK

    def lspec(shape):
        nd = len(shape)
        return pl.BlockSpec((None,) + tuple(shape[1:]), lambda s: (layer,) + (0,) * (nd - 1))

    def cur(width):
        return pl.BlockSpec((rows, width), lambda s: (jnp.minimum(s, steps - 1), 0))

    def prev(width):
        return pl.BlockSpec((rows, width), lambda s: (jnp.maximum(s - 1, 0), 0))

    def last(shape):
        return pl.BlockSpec((1,) + shape,
                            lambda s: (jnp.minimum(2 * s, nb - 1) // nblk,) + (0,) * len(shape))

    def tab(half):
        return pl.BlockSpec((BLK, 128), lambda s: (lax.rem(2 * s + half, nblk), 0))

    mix_params = ('conv_w', 'conv_b', 'lru_wa_bd', 'lru_wx_bd', 'lru_b_a', 'lru_b_x', 'lru_lambda', 'ret_norm')
    return pl.pallas_call(
        functools.partial(_mixout_body, layer=layer, pad=pad, nblk=nblk, final=final),
        grid=(steps + 1,),
        in_specs=[pl.BlockSpec(memory_space=pltpu.SMEM), cur(N_MIX), tab(0), tab(0), tab(1), tab(1),
                  prev(D_MODEL), prev(N_GATE)]
        + [lspec(p[k].shape) for k in mix_params]
        + [_layer_spec(p[k].shape, layer) for k in _OUT_WEIGHTS]
        + [pl.BlockSpec((1, D_MODEL), lambda s: (0, 0))],
        out_specs=[prev(D_MODEL), last((BLK, 128)), last((BLK, 128)), last((8, LRU_WIDTH)), last((8, LRU_WIDTH)),
                   last((RET_HEADS * RET_DK, RET_DV))],
        out_shape=[jax.ShapeDtypeStruct((n, D_MODEL), F32),
                   jax.ShapeDtypeStruct((bsz, BLK, 128), F32), jax.ShapeDtypeStruct((bsz, BLK, 128), F32),
                   jax.ShapeDtypeStruct((bsz, 8, LRU_WIDTH), F32), jax.ShapeDtypeStruct((bsz, 8, LRU_WIDTH), F32),
                   jax.ShapeDtypeStruct((bsz, RET_HEADS * RET_DK, RET_DV), F32)],
        scratch_shapes=[pltpu.VMEM((8, LRU_WIDTH), F32), pltpu.VMEM((2, rows, N_OC), BF16),
                        pltpu.VMEM((8, LRU_WIDTH), F32),
                        pltpu.VMEM((SWA_KV_HEADS, BLK, 128), BF16), pltpu.VMEM((SWA_KV_HEADS, BLK, 128), BF16),
                        pltpu.VMEM((RET_HEADS * RET_DK, RET_DV), F32),
                        pltpu.VMEM((RET_HEADS, BLK, BLK), F32), pltpu.VMEM((RET_HEADS, BLK, 128), F32),
                        pltpu.VMEM((2, BLK, 128), F32),
                        pltpu.VMEM((8, LRU_WIDTH), F32), pltpu.VMEM((8, LRU_WIDTH), F32),
                        pltpu.VMEM((RET_HEADS * RET_DK, RET_DV), F32)]
        + [pltpu.VMEM((LRU_WIDTH // 128, BLK, 128), F32)] * 3,
        compiler_params=pltpu.CompilerParams(dimension_semantics=("arbitrary",), vmem_limit_bytes=VMEM_LIMIT),
        name='mix_out',
    )(p['swa_sinks'], pm, cos, sin, cos, sin, x1, pg, *[p[k] for k in mix_params],
      *[p[k] for k in _OUT_WEIGHTS], p['final_norm'])


def _mixs_body(pm_ref, ck_ref, cv_ref, conv_ref, h0_ref, s_ref, cw_ref, cb_ref, wa_ref, wx_ref, ba_ref, bx_ref,
               lam_ref, inv_ref, gn_ref, sk_ref, nk_all, nv_all, ns_all,
               oc_ref, nk_ref, nv_ref, nconv_ref, nh_ref, ns_ref,
               qb, o8, qr_s, kr_s, v4_s, o2_s):
    gsz = G_SEQ
    nrow = 16
    rs = slice(None)

    @pl.when(pl.program_id(0) == 0)
    def _():
        qb[...] = jnp.zeros_like(qb)
        qr_s[...] = jnp.zeros_like(qr_s)
        kr_s[...] = jnp.zeros_like(kr_s)
        v4_s[...] = jnp.zeros_like(v4_s)

    lane = lax.broadcasted_iota(jnp.int32, (1, 128), 1)
    lo = lane < 64

    xa = pm_ref[rs,C_XA:C_XA + LRU_WIDTH]
    h1 = conv_ref[:, LRU_WIDTH:2 * LRU_WIDTH]
    h2 = conv_ref[:, 2 * LRU_WIDTH:3 * LRU_WIDTH]
    xc = cb_ref[...] + conv_ref[:, 0:LRU_WIDTH] * cw_ref[0:1, :]
    xc = xc + h1 * cw_ref[1:2, :]
    xc = xc + h2 * cw_ref[2:3, :]
    xc = xc + xa * cw_ref[3:4, :]
    nconv_ref[:, 0:LRU_WIDTH] = h1
    nconv_ref[:, LRU_WIDTH:2 * LRU_WIDTH] = h2
    nconv_ref[:, 2 * LRU_WIDTH:3 * LRU_WIDTH] = xa
    a, bt = _lru_gates(xc, wa_ref, wx_ref, ba_ref, bx_ref, lam_ref)
    hn = bt + a * h0_ref[...]
    nh_ref[...] = hn
    oc_ref[:, 0:LRU_WIDTH] = (hn * pm_ref[rs,C_YA:C_YA + LRU_WIDTH]).astype(BF16)

    nk_ref[:, 0:WINDOW - 1, :] = ck_ref[:, 1:WINDOW, :]
    nv_ref[:, 0:WINDOW - 1, :] = cv_ref[:, 1:WINDOW, :]
    for b in range(gsz):
        nk_ref[b, WINDOW - 1:WINDOW, :] = pm_ref[b:b + 1, C_KS:C_KS + 128]
        nv_ref[b, WINDOW - 1:WINDOW, :] = pm_ref[b:b + 1, C_VS:C_VS + 128]

    for r in range(SWA_HEADS):
        h = r // SWA_GROUP
        slab = pm_ref[rs,C_QS + (r // 2) * 128:C_QS + (r // 2 + 1) * 128]
        if r % 2 != h:
            slab = pltpu.roll(slab, 64, 1)
        qb[r * gsz:(r + 1) * gsz, :] = jnp.where(lo if h == 0 else jnp.logical_not(lo), slab, 0.0)

    ang = float(PAST_LEN) * inv_ref[...]
    cos2, sin_signed, first_half = _rope_operands(jnp.cos(ang), jnp.sin(ang))
    qc = _rope(pm_ref[rs,C_QR:C_QR + 256], cos2, sin_signed, first_half)
    kc = _rope(pm_ref[rs,C_KR:C_KR + 256], cos2, sin_signed, first_half)
    lane256 = lax.broadcasted_iota(jnp.int32, (1, RET_HEADS * RET_DK), 1)
    for r in range(RET_HEADS):
        hm = (lane256 >= r * RET_DK) & (lane256 < (r + 1) * RET_DK)
        qm = jnp.where(hm, qc, 0.0)
        km = jnp.where(hm, kc, 0.0)
        for c in range(2):
            qr_s[c, r * gsz:(r + 1) * gsz, :] = qm[:, c * 128:(c + 1) * 128]
            kr_s[c, r * gsz:(r + 1) * gsz, :] = km[:, c * 128:(c + 1) * 128]
        v4_s[r * gsz:(r + 1) * gsz, :] = pm_ref[rs,C_VR + r * RET_DV:C_VR + (r + 1) * RET_DV]

    srow = lax.broadcasted_iota(jnp.int32, (RET_HEADS * RET_DK, 1), 0)
    gcol = jnp.full((RET_HEADS * RET_DK, 1), math.exp(LOG_G[0]), F32)
    for r in range(1, RET_HEADS):
        gcol = jnp.where(srow >= r * RET_DK, math.exp(LOG_G[r]), gcol)
    sk = sk_ref[:, 0:1]

    lanes = 4

    def per_group(i, carry):
        seqs = [i * lanes + q for q in range(lanes)]
        rows = [pl.ds(b, nrow, stride=gsz) for b in seqs]
        s = [_dot_nt(qb[r, :].astype(BF16), nk_ref[b].astype(BF16)) for b, r in zip(seqs, rows)]
        sb = [s_ref[b] for b in seqs]
        o2 = [_dot(qr_s[0, r, :].astype(BF16), x[0:128, :].astype(BF16))
              + _dot(qr_s[1, r, :].astype(BF16), x[128:256, :].astype(BF16)) for r, x in zip(rows, sb)]
        v4 = [v4_s[r, :].astype(BF16) for r in rows]
        kv = [jnp.concatenate([_dot_tn(kr_s[0, r, :].astype(BF16), v), _dot_tn(kr_s[1, r, :].astype(BF16), v)],
                              axis=0) for r, v in zip(rows, v4)]
        m = [jnp.maximum(jnp.max(x, axis=-1, keepdims=True), sk) for x in s]
        e = [jnp.exp(x - y) for x, y in zip(s, m)]
        den = [jnp.sum(x, axis=-1, keepdims=True) + jnp.exp(sk - y) for x, y in zip(e, m)]
        o = [_dot(x.astype(BF16), nv_ref[b].astype(BF16)) / d for x, b, d in zip(e, seqs, den)]
        for q, (b, r) in enumerate(zip(seqs, rows)):
            o2_s[r, :] = o2[q]
            ns_ref[b] = gcol * sb[q] + kv[q]
            o8[r, :] = o[q]
        return carry

    lax.fori_loop(0, gsz // lanes, per_group, 0)

    for sl in range(4):
        h = sl // 2
        ev = o8[(2 * sl) * gsz:(2 * sl + 1) * gsz, :]
        od = o8[(2 * sl + 1) * gsz:(2 * sl + 2) * gsz, :]
        if h != 0:
            ev = pltpu.roll(ev, 64, 1)
        if h != 1:
            od = pltpu.roll(od, 64, 1)
        oc_ref[:, LRU_WIDTH + sl * 128:LRU_WIDTH + (sl + 1) * 128] = jnp.where(lo, ev, od).astype(BF16)

    prod = qc * kc
    p_hi = prod.astype(BF16)
    p_lo = (prod - p_hi.astype(F32)).astype(BF16)
    er = lax.broadcasted_iota(jnp.int32, (RET_HEADS * RET_DK, RET_HEADS * RET_DV), 0) // RET_DK
    ec = lax.broadcasted_iota(jnp.int32, (RET_HEADS * RET_DK, RET_HEADS * RET_DV), 1) // RET_DV
    expand = jnp.where(er == ec, 1.0, 0.0).astype(BF16)
    qk = _dot(p_hi, expand) + _dot(p_lo, expand)
    for r in range(RET_HEADS):
        cs = slice(r * RET_DV, (r + 1) * RET_DV)
        o = qk[:, cs] * pm_ref[rs,C_VR + r * RET_DV:C_VR + (r + 1) * RET_DV]
        o = o + o2_s[r * gsz:(r + 1) * gsz, :] * math.exp(LOG_G[r])
        oc = _group_norm_gate(o, gn_ref[:, cs], pm_ref[rs,C_GR + r * RET_DV:C_GR + (r + 1) * RET_DV])
        oc_ref[:, 2 * LRU_WIDTH + r * RET_DV:2 * LRU_WIDTH + (r + 1) * RET_DV] = oc.astype(BF16)


def _call_mix_sample(pm, ck, cv, conv, h0, sret, stacked, p, layer):
    nseq = pm.shape[0]
    depth = ck.shape[0]
    whole = pl.BlockSpec(memory_space=pl.ANY)
    gsz = G_SEQ

    def lspec(shape):
        nd = len(shape)
        return pl.BlockSpec((None,) + tuple(shape[1:]), lambda i: (layer,) + (0,) * (nd - 1))

    def seq2(width):
        return pl.BlockSpec((gsz, width), lambda i: (i, 0))

    def seq3(layered, d1, d2):
        if layered:
            return pl.BlockSpec((None, gsz, d1, d2), lambda i: (layer, i, 0, 0))
        return pl.BlockSpec((gsz, d1, d2), lambda i: (i, 0, 0))

    sdim = RET_HEADS * RET_DK
    return pl.pallas_call(
        _mixs_body,
        grid=(nseq // gsz,),
        in_specs=[seq2(N_MIX), seq3(True, WINDOW, 128), seq3(True, WINDOW, 128),
                  pl.BlockSpec((None, gsz, 3 * LRU_WIDTH), lambda i: (layer, i, 0)),
                  pl.BlockSpec((None, gsz, LRU_WIDTH), lambda i: (layer, i, 0)),
                  seq3(True, sdim, RET_DV),
                  lspec(p['conv_w'].shape), lspec(p['conv_b'].shape), lspec(p['lru_wa_bd'].shape),
                  lspec(p['lru_wx_bd'].shape), lspec(p['lru_b_a'].shape), lspec(p['lru_b_x'].shape),
                  lspec(p['lru_lambda'].shape), pl.BlockSpec((1, 128), lambda i: (0, 0)),
                  lspec(p['ret_norm'].shape), lspec(p['sinks16'].shape), whole, whole, whole],
        out_specs=[seq2(N_OC), seq3(True, WINDOW, 128), seq3(True, WINDOW, 128), seq2(3 * LRU_WIDTH),
                   seq2(LRU_WIDTH), seq3(True, sdim, RET_DV)],
        out_shape=[jax.ShapeDtypeStruct((nseq, N_OC), BF16),
                   jax.ShapeDtypeStruct((depth, nseq, WINDOW, 128), F32),
                   jax.ShapeDtypeStruct((depth, nseq, WINDOW, 128), F32),
                   jax.ShapeDtypeStruct((nseq, 3 * LRU_WIDTH), F32), jax.ShapeDtypeStruct((nseq, LRU_WIDTH), F32),
                   jax.ShapeDtypeStruct((depth, nseq, sdim, RET_DV), F32)],
        input_output_aliases={16: 1, 17: 2, 18: 5},
        scratch_shapes=[pltpu.VMEM((16 * gsz, 128), F32), pltpu.VMEM((16 * gsz, 128), F32),
                        pltpu.VMEM((2, 16 * gsz, 128), F32), pltpu.VMEM((2, 16 * gsz, 128), F32),
                        pltpu.VMEM((16 * gsz, 128), F32), pltpu.VMEM((16 * gsz, 128), F32)],
        compiler_params=pltpu.CompilerParams(dimension_semantics=("arbitrary",), vmem_limit_bytes=VMEM_LIMIT),
        name='mix_sample',
    )(pm, ck, cv, conv, h0, sret, p['conv_w'], p['conv_b'], p['lru_wa_bd'], p['lru_wx_bd'], p['lru_b_a'],
      p['lru_b_x'], p['lru_lambda'], p['rope_inv'], p['ret_norm'], p['sinks16'], *stacked)


def _block_diag(w):
    depth = w.shape[0]
    w = w.reshape(depth, 2, 4, LRU_BW, LRU_BW)
    eye = jnp.eye(4, dtype=w.dtype)
    return jnp.einsum('lsncd,nm->lsncmd', w, eye).reshape(depth, 2, 4 * LRU_BW, 4 * LRU_BW)


def kernel(x_prompt, x_sample, cache_swa_k, cache_swa_v, state_conv, state_lru, state_ret, meta_tokens, ffn1_norm,
           ffn1_w_gu, ffn1_w_down, mix_norm, w_in, conv_w, conv_b, lru_w_a, lru_b_a, lru_w_x, lru_b_x, lru_lambda,
           swa_sinks, ret_norm, w_branch_a, w_branch_b, w_branch_c, w_out, ffn2_norm, ffn2_w_gu, ffn2_w_down,
           final_norm):
    depth = w_in.shape[0]
    bsz, seq, _ = x_prompt.shape
    nseq = x_sample.shape[0]
    buf = cache_swa_k.shape[2]
    assert buf == WINDOW == BLK and x_sample.shape[1] == 1 and nseq % G_SEQ == 0
    t = seq + N_META
    pad = (-t) % BLK
    tp = t + pad
    assert (bsz * tp) % TM_DENSE == 0

    def row(v):
        return v.reshape(depth, 1, -1).astype(F32)

    def bf16(w):
        return w.astype(BF16)

    assert SWA_HEAD_DIM ** -0.5 == 0.125 and RET_DK ** -0.5 == 0.125
    cols = jnp.arange(N_MIX)
    col_scale = jnp.where(((cols >= C_QS) & (cols < C_KS)) | ((cols >= C_KR) & (cols < C_VR)), 0.125, 1.0)
    half = jnp.arange(128) % (RET_DK // 2)
    p = {
        'ffn1_norm': row(ffn1_norm), 'ffn1_w_gu': bf16(ffn1_w_gu), 'ffn1_w_down': bf16(ffn1_w_down),
        'mix_norm': row(mix_norm), 'w_mix': bf16(w_in[:, :, :N_MIX] * col_scale), 'w_gate': bf16(w_in[:, :, N_MIX:]),
        'conv_w': conv_w.astype(F32), 'conv_b': row(conv_b),
        'lru_wa_bd': _block_diag(lru_w_a).astype(BF16), 'lru_wx_bd': _block_diag(lru_w_x).astype(BF16),
        'lru_b_a': row(lru_b_a), 'lru_b_x': row(lru_b_x), 'lru_lambda': row(lru_lambda),
        'swa_sinks': swa_sinks.astype(F32),
        'sinks16': jnp.pad(jnp.broadcast_to(swa_sinks.astype(F32)[:, :, None], (depth, SWA_HEADS, 128)),
                           ((0, 0), (0, 16 - SWA_HEADS), (0, 0))),
        'ret_norm': row(ret_norm),
        'w_branch_a': bf16(w_branch_a), 'w_branch_b': bf16(w_branch_b),
        'w_branch_c': bf16(w_branch_c), 'w_out': bf16(w_out),
        'ffn2_norm': row(ffn2_norm), 'ffn2_w_gu': bf16(ffn2_w_gu), 'ffn2_w_down': bf16(ffn2_w_down),
        'final_norm': final_norm.reshape(1, D_MODEL).astype(F32),
        'rope_inv': (ROPE_BASE ** (-half.astype(F32) / (RET_DK // 2))).reshape(1, 128),
    }

    assert pad + N_META == BLK
    head = jnp.concatenate([jnp.zeros((pad, D_MODEL), F32), meta_tokens.astype(F32)], axis=0)
    xp = x_prompt.reshape(bsz * seq, D_MODEL)
    xs = x_sample.reshape(nseq, D_MODEL)
    ck = cache_swa_k.reshape(depth, nseq, buf, 128)
    cv = cache_swa_v.reshape(depth, nseq, buf, 128)
    conv = state_conv.reshape(depth, nseq, 3 * LRU_WIDTH)
    sret = state_ret.reshape(depth, nseq, RET_HEADS * RET_DK, RET_DV)
    cos, sin = _call_ropetab(p['rope_inv'], tp, pad)

    outs_p = [[] for _ in range(5)]
    outs_s = [[], []]
    stacked = (jnp.zeros(ck.shape, F32), jnp.zeros(cv.shape, F32), jnp.zeros(sret.shape, F32))
    for layer in range(depth):
        final = layer == depth - 1
        if layer == 0:
            x1, pm, pg = _call_in(xp, p, layer, TM_DENSE, head=head, nblk=tp // BLK)
        else:
            x1, pm, pg = _call_in(xp, p, layer, TM_DENSE)
        xp, kl, vl, cl, hl, sl = _call_mix_out(pm, cos, sin, x1, pg, p, layer, bsz, pad, final)
        for acc, o in zip(outs_p, (kl.reshape(bsz, buf, SWA_KV_HEADS, SWA_HEAD_DIM),
                                   vl.reshape(bsz, buf, SWA_KV_HEADS, SWA_HEAD_DIM),
                                   cl[:, 8 - (CONV_WIDTH - 1):, :], hl[:, 7, :],
                                   sl.reshape(bsz, RET_HEADS, RET_DK, RET_DV))):
            acc.append(o)

        x1, pm, pg = _call_in(xs, p, layer, nseq)
        oc, nk, nv, nc, nh, ns = _call_mix_sample(pm, ck, cv, conv, state_lru, sret, stacked, p, layer)
        stacked = (nk, nv, ns)
        xs = _call_out(x1, pg, oc, p, layer, nseq, final)
        outs_s[0].append(nc.reshape(nseq, CONV_WIDTH - 1, LRU_WIDTH))
        outs_s[1].append(nh)

    yp = xp.reshape(bsz, tp, D_MODEL)[:, pad + N_META:]
    ys = xs.reshape(nseq, 1, D_MODEL)
    nk, nv, ns = stacked
    return ((yp, ys) + tuple(jnp.stack(a) for a in outs_p)
            + (nk.reshape(depth, nseq, buf, SWA_KV_HEADS, SWA_HEAD_DIM),
               nv.reshape(depth, nseq, buf, SWA_KV_HEADS, SWA_HEAD_DIM),
               jnp.stack(outs_s[0]), jnp.stack(outs_s[1]),
               ns.reshape(depth, nseq, RET_HEADS, RET_DK, RET_DV)))
```

```python
import functools
import math

import jax
import jax.numpy as jnp
from jax import lax
from jax.experimental import pallas as pl
from jax.experimental.pallas import tpu as pltpu

F32 = jnp.float32
BF16 = jnp.bfloat16

D_MODEL = 1024
D_FF = 2048
N_META = 16
EPS = 1e-6
LRU_WIDTH = 512
LRU_BLOCKS = 8
LRU_BW = 64
CONV_WIDTH = 4
LRU_C = 8.0
SWA_HEAD_DIM = 64
SWA_HEADS = 8
SWA_KV_HEADS = 2
SWA_GROUP = 4
WINDOW = 128
RET_DK = 64
RET_DV = 128
RET_HEADS = 4
ROPE_BASE = 10000.0
GN_EPS = 1e-5
PAST_LEN = 8192

BLK = 128
N_MIX = 3328
N_GATE = 3 * D_MODEL
C_XA, C_YA, C_QS, C_KS, C_VS, C_QR, C_KR, C_VR, C_GR = 0, 512, 1024, 1536, 1664, 1792, 2048, 2304, 2816
C_BT = N_MIX
N_MIX_SEQ = N_MIX + LRU_WIDTH
N_OC = 3 * LRU_WIDTH
_MIX_CHUNKS = ((C_XA, C_YA, None), (C_YA, C_QS, jax.nn.gelu), (C_QS, C_KS, None), (C_KS, C_KR, None),
               (C_KR, C_KR + 512, None), (C_KR + 512, C_GR, None), (C_GR, N_MIX, jax.nn.silu))
LOG_G = tuple(math.log1p(-(2.0 ** (-5.0 - h))) for h in range(RET_HEADS))

TM_DENSE = 256
G_SEQ = 16
VMEM_LIMIT = 56 * 1024 * 1024


def _dot(a, b):
    return jnp.dot(a, b, preferred_element_type=F32)


def _dot_nt(a, b):
    return lax.dot_general(a, b, (((1,), (1,)), ((), ())), preferred_element_type=F32)


def _dot_tn(a, b):
    return lax.dot_general(a, b, (((0,), (0,)), ((), ())), preferred_element_type=F32)


def _rms(x, g):
    return x * lax.rsqrt(jnp.mean(x * x, axis=-1, keepdims=True) + EPS) * g


def _softplus(x):
    return jnp.maximum(x, 0.0) + jnp.log1p(jnp.exp(-jnp.abs(x)))


def _wt(ref, rows=None, cols=None):
    r = slice(None) if rows is None else slice(*rows)
    c = slice(None) if cols is None else slice(*cols)
    return ref[r, c]


def _drain(pieces):
    while True:
        try:
            next(pieces)
        except StopIteration as done:
            return done.value


FF_CHUNK = 256


def _ffn_pieces(u, wgu_ref, wd_ref):
    y = None
    act = None
    nchunk = D_FF // FF_CHUNK
    for c in range(nchunk + 1):
        if c < nchunk:
            lo, hi = c * FF_CHUNK, (c + 1) * FF_CHUNK
            gate = _dot(u, _wt(wgu_ref, cols=(lo, hi)))
            yield
            up = _dot(u, _wt(wgu_ref, cols=(D_FF + lo, D_FF + hi)))
            yield
        if c > 0:
            part = _dot(act, _wt(wd_ref, rows=((c - 1) * FF_CHUNK, c * FF_CHUNK)))
            y = part if y is None else y + part
            yield
        if c < nchunk:
            act = (jax.nn.silu(gate) * up).astype(BF16)
    return y


def _swiglu(u, wgu_ref, wd_ref):
    gu = _dot(u, _wt(wgu_ref))
    act = (jax.nn.silu(gu[:, :D_FF]) * gu[:, D_FF:]).astype(BF16)
    return _dot(act, _wt(wd_ref))


def _in_body(*refs, nblk=None, seq=False):
    if seq:
        refs, (cw_ref, cb_ref, wa_ref, wx_ref, ba_ref, bx_ref, lam_ref), (x1_ref, pm_ref, pg_ref, xt_ref, xtail) = (
            refs[:-12], refs[-12:-5], refs[-5:])
    else:
        refs, (x1_ref, pm_ref, pg_ref) = refs[:-3], refs[-3:]
    if nblk is None:
        x_ref, n1_ref, wgu_ref, wd_ref, n2_ref, wm_ref, wg_ref = refs
        x = x_ref[...]
    else:
        xa_ref, xb_ref, head_ref, n1_ref, wgu_ref, wd_ref, n2_ref, wm_ref, wg_ref = refs
        first = 2 * pl.program_id(0)
        x = jnp.concatenate([jnp.where(lax.rem(first + h, nblk) == 0, head_ref[...], ref[...])
                             for h, ref in enumerate((xa_ref, xb_ref))], axis=0)
    if seq:
        @pl.when(pl.program_id(0) == 0)
        def _():
            xtail[...] = jnp.zeros_like(xtail)

    x1 = x + 0.5 * _drain(_ffn_pieces(_rms(x, n1_ref[...]).astype(BF16), wgu_ref, wd_ref))
    x1_ref[...] = x1
    u2 = _rms(x1, n2_ref[...]).astype(BF16)
    jobs = ([(wm_ref, pm_ref, lo, hi, fn) for lo, hi, fn in _MIX_CHUNKS]
            + [(wg_ref, pg_ref, c, c + LRU_WIDTH, jax.nn.sigmoid) for c in range(0, N_GATE, LRU_WIDTH)])
    side = iter(())
    pending = None
    for job in jobs + [None]:
        res = None if job is None else _dot(u2, _wt(job[0], cols=(job[2], job[3])))
        if pending is not None:
            val, (_, o_ref, lo, hi, fn) = pending
            if seq and o_ref is pm_ref and lo == C_XA:
                side = _lru_front_pieces(val, xtail, cw_ref, cb_ref, wa_ref, wx_ref, ba_ref, bx_ref, lam_ref,
                                         pm_ref, xt_ref)
            else:
                o_ref[:, lo:hi] = val if fn is None else fn(val)
        next(side, None)
        pending = (res, job)
    _drain(side)


def _lru_front_pieces(xa, xtail, cw_ref, cb_ref, wa_ref, wx_ref, ba_ref, bx_ref, lam_ref, pm_ref, xt_ref):
    rows = xa.shape[0]
    row8 = lax.broadcasted_iota(jnp.int32, (8, 1), 0)
    hist = xtail[...]
    xc = cb_ref[...]
    for tap in range(CONV_WIDTH - 1):
        k = CONV_WIDTH - 1 - tap
        sh = pltpu.roll(xa, k, 0)
        top = jnp.where(row8 < k, pltpu.roll(hist, k, 0), sh[0:8, :])
        xc = xc + jnp.concatenate([top, sh[8:, :]], axis=0) * cw_ref[tap:tap + 1, :]
        yield
    xc = xc + xa * cw_ref[CONV_WIDTH - 1:CONV_WIDTH, :]
    xtail[...] = xa[rows - 8:rows, :]
    for h in range(rows // BLK):
        xt_ref[h] = xa[(h + 1) * BLK - 8:(h + 1) * BLK, :]
    yield
    a, bt = yield from _lru_gate_pieces(xc, wa_ref, wx_ref, ba_ref, bx_ref, lam_ref)
    pm_ref[:, C_XA:C_XA + LRU_WIDTH] = a
    pm_ref[:, C_BT:C_BT + LRU_WIDTH] = bt


def _merge_out(x1, pg, oc, wa_ref, wb_ref, wc_ref, wo_ref, n_ref, wgu_ref, wd_ref, fn_ref, final):
    g = pg
    merged = (g[:, :D_MODEL] * _dot(oc[:, :LRU_WIDTH], _wt(wa_ref))
              + g[:, D_MODEL:2 * D_MODEL] * _dot(oc[:, LRU_WIDTH:2 * LRU_WIDTH], _wt(wb_ref))
              + g[:, 2 * D_MODEL:] * _dot(oc[:, 2 * LRU_WIDTH:], _wt(wc_ref)))
    x2 = x1 + _dot(merged.astype(BF16), _wt(wo_ref))
    x3 = x2 + 0.5 * _swiglu(_rms(x2, n_ref[...]).astype(BF16), wgu_ref, wd_ref)
    if final:
        x3 = _rms(x3, fn_ref[...])
    return x3


def _merge_out_pieces(x1_ref, pg_ref, oc, wa_ref, wb_ref, wc_ref, wo_ref, n_ref, wgu_ref, wd_ref, fn_ref, out_ref,
                      final):
    half = D_MODEL // 2
    merged = [None, None]
    pending = None
    for b, w_ref in enumerate((wa_ref, wb_ref, wc_ref)):
        for c in range(2):
            nxt = (_dot(oc[:, b * LRU_WIDTH:(b + 1) * LRU_WIDTH], _wt(w_ref, cols=(c * half, (c + 1) * half))), c,
                   pg_ref[:, b * D_MODEL + c * half:b * D_MODEL + (c + 1) * half])
            if pending is not None:
                proj, pc, g = pending
                merged[pc] = g * proj if merged[pc] is None else merged[pc] + g * proj
            pending = nxt
            yield
    proj, pc, g = pending
    merged[pc] = merged[pc] + g * proj
    merged = jnp.concatenate(merged, axis=1).astype(BF16)
    x2 = []
    pending = None
    for c in range(0, D_MODEL, FF_CHUNK):
        nxt = (_dot(merged, _wt(wo_ref, cols=(c, c + FF_CHUNK))), c)
        if pending is not None:
            x2.append(x1_ref[:, pending[1]:pending[1] + FF_CHUNK] + pending[0])
        pending = nxt
        yield
    x2.append(x1_ref[:, pending[1]:pending[1] + FF_CHUNK] + pending[0])
    x2 = jnp.concatenate(x2, axis=1)
    u = _rms(x2, n_ref[...]).astype(BF16)
    yield
    y = yield from _ffn_pieces(u, wgu_ref, wd_ref)
    x3 = x2 + 0.5 * y
    out_ref[...] = _rms(x3, fn_ref[...]) if final else x3
    yield


def _alternate(first, second, ratio):
    live = [True, True]
    while any(live):
        for idx, (gen, count) in enumerate(((first, 1), (second, ratio))):
            for _ in range(count):
                if live[idx] and next(gen, StopIteration) is StopIteration:
                    live[idx] = False


def _out_body(x1_ref, pg_ref, oc_ref, wa_ref, wb_ref, wc_ref, wo_ref, n_ref, wgu_ref, wd_ref, fn_ref, out_ref,
              *, final):
    out_ref[...] = _merge_out(x1_ref[...], pg_ref[...], oc_ref[...], wa_ref, wb_ref, wc_ref, wo_ref, n_ref,
                              wgu_ref, wd_ref, fn_ref, final)


def _layer_spec(shape, layer):
    nd = len(shape)
    return pl.BlockSpec((None,) + tuple(shape[1:]), lambda *_: (layer,) + (0,) * (nd - 1),
                        pipeline_mode=pl.Buffered(1))


def _row_spec(tm, width):
    return pl.BlockSpec((tm, width), lambda i: (i, 0))


def _dense_params():
    return pltpu.CompilerParams(dimension_semantics=("arbitrary",), vmem_limit_bytes=VMEM_LIMIT)


def _prompt_block(f, nblk):
    return f - f // nblk - 1


_LRU_PARAMS = ('conv_w', 'conv_b', 'lru_wa_bd', 'lru_wx_bd', 'lru_b_a', 'lru_b_x', 'lru_lambda')


def _call_in(x, p, layer, tm, head=None, nblk=None, seq=False):
    if head is None:
        n = x.shape[0]
        x_specs, x_args = [_row_spec(tm, D_MODEL)], (x,)
    else:
        assert tm == 2 * BLK
        n = (x.shape[0] // BLK + x.shape[0] // BLK // (nblk - 1)) * BLK
        last = x.shape[0] // BLK - 1

        def half(h):
            return pl.BlockSpec((BLK, D_MODEL),
                                lambda i: (jnp.clip(_prompt_block(2 * i + h, nblk), 0, last), 0))

        x_specs = [half(0), half(1), pl.BlockSpec((BLK, D_MODEL), lambda i: (0, 0))]
        x_args = (x, x, head)
    weights = ('ffn1_norm', 'ffn1_w_gu', 'ffn1_w_down', 'mix_norm', 'w_mix', 'w_gate') + (_LRU_PARAMS if seq else ())
    width = N_MIX_SEQ if seq else N_MIX
    out_specs = [_row_spec(tm, D_MODEL), _row_spec(tm, width), _row_spec(tm, N_GATE)]
    out_shape = [jax.ShapeDtypeStruct((n, D_MODEL), F32), jax.ShapeDtypeStruct((n, width), F32),
                 jax.ShapeDtypeStruct((n, N_GATE), F32)]
    if seq:
        out_specs.append(pl.BlockSpec((tm // BLK, 8, LRU_WIDTH), lambda i: (i, 0, 0)))
        out_shape.append(jax.ShapeDtypeStruct((n // BLK, 8, LRU_WIDTH), F32))
    return pl.pallas_call(
        functools.partial(_in_body, nblk=None if head is None else nblk, seq=seq),
        grid=(n // tm,),
        in_specs=x_specs + [_layer_spec(p[k].shape, layer) for k in weights],
        out_specs=out_specs,
        out_shape=out_shape,
        scratch_shapes=[pltpu.VMEM((8, LRU_WIDTH), F32)] if seq else [],
        compiler_params=_dense_params(),
        name='layer_in',
    )(*x_args, *[p[k] for k in weights])


_OUT_WEIGHTS = ('w_branch_a', 'w_branch_b', 'w_branch_c', 'w_out', 'ffn2_norm', 'ffn2_w_gu', 'ffn2_w_down')


def _call_out(x1, pg, oc, p, layer, tm, final):
    n = x1.shape[0]
    return pl.pallas_call(
        functools.partial(_out_body, final=final),
        grid=(n // tm,),
        in_specs=[_row_spec(tm, D_MODEL), _row_spec(tm, N_GATE), _row_spec(tm, N_OC)]
        + [_layer_spec(p[k].shape, layer) for k in _OUT_WEIGHTS]
        + [pl.BlockSpec((1, D_MODEL), lambda i: (0, 0))],
        out_specs=_row_spec(tm, D_MODEL),
        out_shape=jax.ShapeDtypeStruct((n, D_MODEL), F32),
        compiler_params=_dense_params(),
        name='layer_out',
    )(x1, pg, oc, *[p[k] for k in _OUT_WEIGHTS], p['final_norm'])


def _lru_gate_pieces(xc, wa_ref, wx_ref, ba_ref, bx_ref, lam_ref):
    xcb = xc.astype(BF16)
    half = LRU_WIDTH // 2
    rpre = jnp.concatenate([_dot(xcb[:, :half], wa_ref[0]), _dot(xcb[:, half:], wa_ref[1])], axis=1)
    yield
    ipre = jnp.concatenate([_dot(xcb[:, :half], wx_ref[0]), _dot(xcb[:, half:], wx_ref[1])], axis=1)
    yield
    r = jax.nn.sigmoid(rpre + ba_ref[...])
    yield
    i = jax.nn.sigmoid(ipre + bx_ref[...])
    yield
    log_a = -LRU_C * r * _softplus(-lam_ref[...])
    a = jnp.exp(log_a)
    yield
    z = -jnp.tanh(log_a) * (a * a + 1.0)
    return a, jnp.where(z > 0.0, z * lax.rsqrt(z), 0.0) * (i * xc)


def _lru_gates(*args):
    return _drain(_lru_gate_pieces(*args))


def _rope_operands(cos, sin):
    lane = lax.broadcasted_iota(jnp.int32, (1, 4 * RET_DK), 1)
    first_half = (lane & (RET_DK - 1)) < RET_DK // 2
    cos2 = jnp.concatenate([cos, cos], axis=1)
    sin2 = jnp.concatenate([sin, sin], axis=1)
    return cos2, jnp.where(first_half, -sin2, sin2), first_half


def _rope(x, cos2, sin_signed, first_half):
    swapped = jnp.where(first_half, pltpu.roll(x, 4 * RET_DK - RET_DK // 2, 1), pltpu.roll(x, RET_DK // 2, 1))
    return x * cos2 + swapped * sin_signed


def _group_norm_gate(o, gain, gate):
    mu = jnp.mean(o, axis=-1, keepdims=True)
    d = o - mu
    var = jnp.mean(d * d, axis=-1, keepdims=True)
    return d * lax.rsqrt(var + GN_EPS) * gain * gate


def _ropetab_body(inv_ref, cos_ref, sin_ref, *, rows_per_step, pad):
    rows = lax.broadcasted_iota(jnp.int32, (rows_per_step, 1), 0)
    pos = (pl.program_id(0) * rows_per_step + rows - pad).astype(F32)
    ang = pos * inv_ref[...]
    cos_ref[...] = jnp.cos(ang)
    sin_ref[...] = jnp.sin(ang)


def _call_ropetab(inv, tp, pad):
    nblk = tp // BLK
    rps = BLK * max(d for d in (8, 5, 4, 2, 1) if nblk % d == 0)
    spec = pl.BlockSpec((rps, 128), lambda i: (i, 0))
    return pl.pallas_call(
        functools.partial(_ropetab_body, rows_per_step=rps, pad=pad),
        grid=(tp // rps,),
        in_specs=[pl.BlockSpec((1, 128), lambda i: (0, 0))],
        out_specs=[spec, spec],
        out_shape=[jax.ShapeDtypeStruct((tp, 128), F32), jax.ShapeDtypeStruct((tp, 128), F32)],
        name='rope_tables',
    )(inv)


def _block_masks(j, pad):
    lo = lax.broadcasted_iota(jnp.int32, (1, 128), 1) < 64
    rows = lax.broadcasted_iota(jnp.int32, (BLK, 1), 0)
    return lo, (j * BLK + rows) >= pad


def _mix_lru(pm_ref, ocs, hcar, a_s, b_s, h_s, *, pad, j, cur, r0):
    rs = slice(r0, r0 + BLK)
    fresh = j == 0
    _, valid = _block_masks(j, pad)
    a = pm_ref[rs,C_XA:C_XA + LRU_WIDTH]
    bt = jnp.where(valid, pm_ref[rs,C_BT:C_BT + LRU_WIDTH], 0.0)
    ngrp = BLK // 8
    for c in range(LRU_WIDTH // 128):
        a_s[c] = a[:, c * 128:(c + 1) * 128]
        b_s[c] = bt[:, c * 128:(c + 1) * 128]
    yield
    h_in = jnp.where(fresh, 0.0, hcar[7:8, :])
    for c in range(LRU_WIDTH // 128):
        prods, sums = [], []
        for r in range(8):
            ar = a_s[c, pl.ds(r, ngrp, stride=8), :]
            br = b_s[c, pl.ds(r, ngrp, stride=8), :]
            prods.append(ar if r == 0 else ar * prods[-1])
            sums.append(br if r == 0 else ar * sums[-1] + br)
        carry = h_in[:, c * 128:(c + 1) * 128]
        carries = []
        for g in range(ngrp):
            carries.append(carry)
            carry = prods[7][g:g + 1, :] * carry + sums[7][g:g + 1, :]
        carries = jnp.concatenate(carries, axis=0)
        for r in range(8):
            h_s[c, pl.ds(r, ngrp, stride=8), :] = prods[r] * carries + sums[r]
        yield
    h = jnp.concatenate([h_s[c] for c in range(LRU_WIDTH // 128)], axis=1)
    hcar[...] = h[BLK - 8:BLK, :]
    o_a = h * pm_ref[rs,C_YA:C_YA + LRU_WIDTH]
    ocs[cur, rs,0:LRU_WIDTH] = o_a.astype(BF16)
    yield


def _mix_swa(sinks_ref, pm_ref, ocs, kprev, vprev, *, layer, pad, j, cur, r0):
    rs = slice(r0, r0 + BLK)
    lo, _ = _block_masks(j, pad)
    k = pm_ref[rs,C_KS:C_KS + 128]
    v = pm_ref[rs,C_VS:C_VS + 128]
    k_sw = pltpu.roll(k, 64, 1)
    v_sw = pltpu.roll(v, 64, 1)
    kdup = (jnp.where(lo, k, k_sw).astype(BF16), jnp.where(lo, k_sw, k).astype(BF16))
    vdup = (jnp.where(lo, v, v_sw).astype(BF16), jnp.where(lo, v_sw, v).astype(BF16))
    row4 = lax.broadcasted_iota(jnp.int32, (SWA_GROUP * BLK, 1), 0)
    t4 = row4 & (BLK - 1)
    col = lax.broadcasted_iota(jnp.int32, (1, 2 * BLK), 1)
    ok = ((j - 1) * BLK + col >= pad) & (col > t4) & (col <= t4 + BLK)
    for h in range(SWA_KV_HEADS):
        parts = []
        for g in range(SWA_GROUP):
            head = SWA_GROUP * h + g
            slab = pm_ref[rs,C_QS + (head // 2) * 128:C_QS + (head // 2 + 1) * 128]
            parts.append(jnp.where(lo if head % 2 == 0 else jnp.logical_not(lo), slab, 0.0))
        qst = jnp.concatenate(parts, axis=0).astype(BF16)
        kcat = jnp.concatenate([kprev[h], kdup[h]], axis=0)
        vcat = jnp.concatenate([vprev[h], vdup[h]], axis=0)
        yield
        sc = jnp.where(ok, _dot_nt(qst, kcat), -jnp.inf)
        yield
        sk = jnp.full((SWA_GROUP * BLK, 1), sinks_ref[layer, SWA_GROUP * h], F32)
        for g in range(1, SWA_GROUP):
            sk = jnp.where(row4 >= g * BLK, sinks_ref[layer, SWA_GROUP * h + g], sk)
        m = jnp.maximum(jnp.max(sc, axis=-1, keepdims=True), sk)
        yield
        e = jnp.exp(sc - m)
        yield
        den = jnp.sum(e, axis=-1, keepdims=True) + jnp.exp(sk - m)
        yield
        o = _dot(e.astype(BF16), vcat) / den
        yield
        for sl in range(2):
            ge = 2 * sl
            slab = jnp.where(lo, o[ge * BLK:(ge + 1) * BLK, :], o[(ge + 1) * BLK:(ge + 2) * BLK, :])
            c0 = LRU_WIDTH + (2 * h + sl) * 128
            ocs[cur, rs,c0:c0 + 128] = slab.astype(BF16)
        kprev[h] = kdup[h]
        vprev[h] = vdup[h]
        yield


def _mix_ret(pm_ref, cos_ref, sin_ref, gn_ref, ocs, state, dec_t, cross_t, kdec_t, *, pad, j, cur, r0):
    rs = slice(r0, r0 + BLK)
    fresh = j == 0
    lo, valid = _block_masks(j, pad)
    cos2, sin_signed, first_half = _rope_operands(cos_ref[...], sin_ref[...])
    qc = _rope(pm_ref[rs,C_QR:C_QR + 256], cos2, sin_signed, first_half)
    yield
    kc = _rope(pm_ref[rs,C_KR:C_KR + 256], cos2, sin_signed, first_half)
    kc = jnp.where(valid, kc, 0.0)
    yield
    st = [jnp.where(fresh, 0.0, state[sl * 128:(sl + 1) * 128, :]) for sl in range(2)]
    yield
    upd = [None, None]
    for h in range(RET_HEADS):
        sl = h // 2
        half = lo if h % 2 == 0 else jnp.logical_not(lo)
        qm = jnp.where(half, qc[:, sl * 128:(sl + 1) * 128], 0.0).astype(BF16)
        kslab = kc[:, sl * 128:(sl + 1) * 128]
        vh = jnp.where(valid, pm_ref[rs,C_VR + h * RET_DV:C_VR + (h + 1) * RET_DV], 0.0).astype(BF16)
        sc = (_dot_nt(qm, kslab.astype(BF16)) * dec_t[h]).astype(BF16)
        yield
        o = _dot(sc, vh)
        o = o + _dot(qm, st[sl].astype(BF16)) * cross_t[h]
        yield
        c0 = 2 * LRU_WIDTH + h * RET_DV
        oc = _group_norm_gate(o, gn_ref[:, h * RET_DV:(h + 1) * RET_DV],
                              pm_ref[rs,C_GR + h * RET_DV:C_GR + (h + 1) * RET_DV])
        ocs[cur, rs,c0:c0 + RET_DV] = oc.astype(BF16)
        yield
        km = jnp.where(half, kslab * kdec_t[sl], 0.0).astype(BF16)
        u = _dot_tn(km, vh)
        upd[sl] = u if upd[sl] is None else upd[sl] + u
        yield
    srow = lax.broadcasted_iota(jnp.int32, (128, 1), 0)
    for sl in range(2):
        gcol = jnp.where(srow < RET_DK, math.exp(BLK * LOG_G[2 * sl]), math.exp(BLK * LOG_G[2 * sl + 1]))
        state[sl * 128:(sl + 1) * 128, :] = gcol * st[sl] + upd[sl]


def _mixout_body(sinks_ref, pm_ref, cos0_ref, sin0_ref, cos1_ref, sin1_ref, x1_ref, pg_ref, gn_ref,
                 wba_ref, wbb_ref, wbc_ref, wo_ref, n2_ref, wgu_ref, wd_ref, fn_ref,
                 out_ref, klast_ref, vlast_ref, hlast_ref, sret_ref,
                 ocs, hcar, kprev, vprev, state, dec_t, cross_t, kdec_t, snap_h, snap_s, a_s, b_s, h_s,
                 *, layer, pad, nblk, final):
    s = pl.program_id(0)

    @pl.when(s == 0)
    def _():
        lo = lax.broadcasted_iota(jnp.int32, (1, 128), 1) < 64
        ti = lax.broadcasted_iota(jnp.int32, (BLK, 1), 0).astype(F32)
        tj = lax.broadcasted_iota(jnp.int32, (1, BLK), 1).astype(F32)
        diff = ti - tj
        for h in range(RET_HEADS):
            dec_t[h] = jnp.exp(jnp.where(diff >= 0, diff * LOG_G[h], -jnp.inf))
            cross_t[h] = jnp.broadcast_to(jnp.exp((ti + 1.0) * LOG_G[h]), (BLK, 128))
        for sl in range(2):
            kdec_t[sl] = jnp.where(lo, jnp.exp((BLK - 1.0 - ti) * LOG_G[2 * sl]),
                                   jnp.exp((BLK - 1.0 - ti) * LOG_G[2 * sl + 1]))
        for ref in (ocs, hcar, kprev, vprev, state):
            ref[...] = jnp.zeros_like(ref)

    cur = lax.rem(s, 2)
    j0 = lax.rem(2 * s, nblk)
    j1 = lax.rem(2 * s + 1, nblk)
    dense = _merge_out_pieces(x1_ref, pg_ref, ocs[lax.rem(s + 1, 2)], wba_ref, wbb_ref, wbc_ref, wo_ref, n2_ref,
                              wgu_ref, wd_ref, fn_ref, out_ref, final)

    def lru(j, r0):
        return _mix_lru(pm_ref, ocs, hcar, a_s, b_s, h_s, pad=pad, j=j, cur=cur, r0=r0)

    def swa(j, r0):
        return _mix_swa(sinks_ref, pm_ref, ocs, kprev, vprev, layer=layer, pad=pad, j=j, cur=cur, r0=r0)

    def ret(j, r0, cos_ref, sin_ref):
        return _mix_ret(pm_ref, cos_ref, sin_ref, gn_ref, ocs, state, dec_t, cross_t, kdec_t,
                        pad=pad, j=j, cur=cur, r0=r0)

    def mixers():
        yield from lru(j0, 0)
        yield from swa(j0, 0)
        yield from ret(j0, 0, cos0_ref, sin0_ref)
        snap_h[...] = hcar[...]
        snap_s[...] = state[...]
        yield from lru(j1, BLK)
        yield from swa(j1, BLK)
        yield from ret(j1, BLK, cos1_ref, sin1_ref)

    _alternate(dense, mixers(), 3)

    def write_state(r0, hl, st):
        klast_ref[0] = pm_ref[r0:r0 + BLK, C_KS:C_KS + 128]
        vlast_ref[0] = pm_ref[r0:r0 + BLK, C_VS:C_VS + 128]
        hlast_ref[0] = hl[...]
        sret_ref[0] = st[...]

    @pl.when(j0 == nblk - 1)
    def _():
        write_state(0, snap_h, snap_s)

    @pl.when(j1 == nblk - 1)
    def _():
        write_state(BLK, hcar, state)


def _call_mix_out(pm, cos, sin, x1, pg, p, layer, bsz, pad, final):
    n = pm.shape[0]
    nb = n // BLK
    nblk = nb // bsz
    assert nb % 2 == 0 and nblk >= 2
    steps = nb // 2
    rows = 2 * BLK

    def lspec(shape):
        nd = len(shape)
        return pl.BlockSpec((None,) + tuple(shape[1:]), lambda s: (layer,) + (0,) * (nd - 1))

    def cur(width):
        return pl.BlockSpec((rows, width), lambda s: (jnp.minimum(s, steps - 1), 0))

    def prev(width):
        return pl.BlockSpec((rows, width), lambda s: (jnp.maximum(s - 1, 0), 0))

    def last(shape):
        return pl.BlockSpec((1,) + shape,
                            lambda s: (jnp.minimum(2 * s, nb - 1) // nblk,) + (0,) * len(shape))

    def tab(half):
        return pl.BlockSpec((BLK, 128), lambda s: (lax.rem(2 * s + half, nblk), 0))

    return pl.pallas_call(
        functools.partial(_mixout_body, layer=layer, pad=pad, nblk=nblk, final=final),
        grid=(steps + 1,),
        in_specs=[pl.BlockSpec(memory_space=pltpu.SMEM), cur(N_MIX_SEQ), tab(0), tab(0), tab(1), tab(1),
                  prev(D_MODEL), prev(N_GATE), lspec(p['ret_norm'].shape)]
        + [_layer_spec(p[k].shape, layer) for k in _OUT_WEIGHTS]
        + [pl.BlockSpec((1, D_MODEL), lambda s: (0, 0))],
        out_specs=[prev(D_MODEL), last((BLK, 128)), last((BLK, 128)), last((8, LRU_WIDTH)),
                   last((RET_HEADS * RET_DK, RET_DV))],
        out_shape=[jax.ShapeDtypeStruct((n, D_MODEL), F32),
                   jax.ShapeDtypeStruct((bsz, BLK, 128), F32), jax.ShapeDtypeStruct((bsz, BLK, 128), F32),
                   jax.ShapeDtypeStruct((bsz, 8, LRU_WIDTH), F32),
                   jax.ShapeDtypeStruct((bsz, RET_HEADS * RET_DK, RET_DV), F32)],
        scratch_shapes=[pltpu.VMEM((2, rows, N_OC), BF16), pltpu.VMEM((8, LRU_WIDTH), F32),
                        pltpu.VMEM((SWA_KV_HEADS, BLK, 128), BF16), pltpu.VMEM((SWA_KV_HEADS, BLK, 128), BF16),
                        pltpu.VMEM((RET_HEADS * RET_DK, RET_DV), F32),
                        pltpu.VMEM((RET_HEADS, BLK, BLK), F32), pltpu.VMEM((RET_HEADS, BLK, 128), F32),
                        pltpu.VMEM((2, BLK, 128), F32),
                        pltpu.VMEM((8, LRU_WIDTH), F32), pltpu.VMEM((RET_HEADS * RET_DK, RET_DV), F32)]
        + [pltpu.VMEM((LRU_WIDTH // 128, BLK, 128), F32)] * 3,
        compiler_params=pltpu.CompilerParams(dimension_semantics=("arbitrary",), vmem_limit_bytes=VMEM_LIMIT),
        name='mix_out',
    )(p['swa_sinks'], pm, cos, sin, cos, sin, x1, pg, p['ret_norm'], *[p[k] for k in _OUT_WEIGHTS],
      p['final_norm'])


def _mixs_body(pm_ref, ck_ref, cv_ref, conv_ref, h0_ref, s_ref, cw_ref, cb_ref, wa_ref, wx_ref, ba_ref, bx_ref,
               lam_ref, inv_ref, gn_ref, sk_ref, nk_all, nv_all, ns_all,
               oc_ref, nk_ref, nv_ref, nconv_ref, nh_ref, ns_ref,
               qb, o8, qr_s, kr_s, v4_s, o2_s):
    gsz = G_SEQ
    nrow = 16
    rs = slice(None)

    @pl.when(pl.program_id(0) == 0)
    def _():
        qb[...] = jnp.zeros_like(qb)
        qr_s[...] = jnp.zeros_like(qr_s)
        kr_s[...] = jnp.zeros_like(kr_s)
        v4_s[...] = jnp.zeros_like(v4_s)

    lane = lax.broadcasted_iota(jnp.int32, (1, 128), 1)
    lo = lane < 64

    xa = pm_ref[rs,C_XA:C_XA + LRU_WIDTH]
    h1 = conv_ref[:, LRU_WIDTH:2 * LRU_WIDTH]
    h2 = conv_ref[:, 2 * LRU_WIDTH:3 * LRU_WIDTH]
    xc = cb_ref[...] + conv_ref[:, 0:LRU_WIDTH] * cw_ref[0:1, :]
    xc = xc + h1 * cw_ref[1:2, :]
    xc = xc + h2 * cw_ref[2:3, :]
    xc = xc + xa * cw_ref[3:4, :]
    nconv_ref[:, 0:LRU_WIDTH] = h1
    nconv_ref[:, LRU_WIDTH:2 * LRU_WIDTH] = h2
    nconv_ref[:, 2 * LRU_WIDTH:3 * LRU_WIDTH] = xa
    a, bt = _lru_gates(xc, wa_ref, wx_ref, ba_ref, bx_ref, lam_ref)
    hn = bt + a * h0_ref[...]
    nh_ref[...] = hn
    oc_ref[:, 0:LRU_WIDTH] = (hn * pm_ref[rs,C_YA:C_YA + LRU_WIDTH]).astype(BF16)

    nk_ref[:, 0:WINDOW - 1, :] = ck_ref[:, 1:WINDOW, :]
    nv_ref[:, 0:WINDOW - 1, :] = cv_ref[:, 1:WINDOW, :]
    for b in range(gsz):
        nk_ref[b, WINDOW - 1:WINDOW, :] = pm_ref[b:b + 1, C_KS:C_KS + 128]
        nv_ref[b, WINDOW - 1:WINDOW, :] = pm_ref[b:b + 1, C_VS:C_VS + 128]

    for r in range(SWA_HEADS):
        h = r // SWA_GROUP
        slab = pm_ref[rs,C_QS + (r // 2) * 128:C_QS + (r // 2 + 1) * 128]
        if r % 2 != h:
            slab = pltpu.roll(slab, 64, 1)
        qb[r * gsz:(r + 1) * gsz, :] = jnp.where(lo if h == 0 else jnp.logical_not(lo), slab, 0.0)

    ang = float(PAST_LEN) * inv_ref[...]
    cos2, sin_signed, first_half = _rope_operands(jnp.cos(ang), jnp.sin(ang))
    qc = _rope(pm_ref[rs,C_QR:C_QR + 256], cos2, sin_signed, first_half)
    kc = _rope(pm_ref[rs,C_KR:C_KR + 256], cos2, sin_signed, first_half)
    lane256 = lax.broadcasted_iota(jnp.int32, (1, RET_HEADS * RET_DK), 1)
    for r in range(RET_HEADS):
        hm = (lane256 >= r * RET_DK) & (lane256 < (r + 1) * RET_DK)
        qm = jnp.where(hm, qc, 0.0)
        km = jnp.where(hm, kc, 0.0)
        for c in range(2):
            qr_s[c, r * gsz:(r + 1) * gsz, :] = qm[:, c * 128:(c + 1) * 128]
            kr_s[c, r * gsz:(r + 1) * gsz, :] = km[:, c * 128:(c + 1) * 128]
        v4_s[r * gsz:(r + 1) * gsz, :] = pm_ref[rs,C_VR + r * RET_DV:C_VR + (r + 1) * RET_DV]

    srow = lax.broadcasted_iota(jnp.int32, (RET_HEADS * RET_DK, 1), 0)
    gcol = jnp.full((RET_HEADS * RET_DK, 1), math.exp(LOG_G[0]), F32)
    for r in range(1, RET_HEADS):
        gcol = jnp.where(srow >= r * RET_DK, math.exp(LOG_G[r]), gcol)
    sk = sk_ref[:, 0:1]

    lanes = 4

    def per_group(i, carry):
        seqs = [i * lanes + q for q in range(lanes)]
        rows = [pl.ds(b, nrow, stride=gsz) for b in seqs]
        s = [_dot_nt(qb[r, :].astype(BF16), nk_ref[b].astype(BF16)) for b, r in zip(seqs, rows)]
        sb = [s_ref[b] for b in seqs]
        o2 = [_dot(qr_s[0, r, :].astype(BF16), x[0:128, :].astype(BF16))
              + _dot(qr_s[1, r, :].astype(BF16), x[128:256, :].astype(BF16)) for r, x in zip(rows, sb)]
        v4 = [v4_s[r, :].astype(BF16) for r in rows]
        kv = [jnp.concatenate([_dot_tn(kr_s[0, r, :].astype(BF16), v), _dot_tn(kr_s[1, r, :].astype(BF16), v)],
                              axis=0) for r, v in zip(rows, v4)]
        m = [jnp.maximum(jnp.max(x, axis=-1, keepdims=True), sk) for x in s]
        e = [jnp.exp(x - y) for x, y in zip(s, m)]
        den = [jnp.sum(x, axis=-1, keepdims=True) + jnp.exp(sk - y) for x, y in zip(e, m)]
        o = [_dot(x.astype(BF16), nv_ref[b].astype(BF16)) / d for x, b, d in zip(e, seqs, den)]
        for q, (b, r) in enumerate(zip(seqs, rows)):
            o2_s[r, :] = o2[q]
            ns_ref[b] = gcol * sb[q] + kv[q]
            o8[r, :] = o[q]
        return carry

    lax.fori_loop(0, gsz // lanes, per_group, 0)

    for sl in range(4):
        h = sl // 2
        ev = o8[(2 * sl) * gsz:(2 * sl + 1) * gsz, :]
        od = o8[(2 * sl + 1) * gsz:(2 * sl + 2) * gsz, :]
        if h != 0:
            ev = pltpu.roll(ev, 64, 1)
        if h != 1:
            od = pltpu.roll(od, 64, 1)
        oc_ref[:, LRU_WIDTH + sl * 128:LRU_WIDTH + (sl + 1) * 128] = jnp.where(lo, ev, od).astype(BF16)

    prod = qc * kc
    p_hi = prod.astype(BF16)
    p_lo = (prod - p_hi.astype(F32)).astype(BF16)
    er = lax.broadcasted_iota(jnp.int32, (RET_HEADS * RET_DK, RET_HEADS * RET_DV), 0) // RET_DK
    ec = lax.broadcasted_iota(jnp.int32, (RET_HEADS * RET_DK, RET_HEADS * RET_DV), 1) // RET_DV
    expand = jnp.where(er == ec, 1.0, 0.0).astype(BF16)
    qk = _dot(p_hi, expand) + _dot(p_lo, expand)
    for r in range(RET_HEADS):
        cs = slice(r * RET_DV, (r + 1) * RET_DV)
        o = qk[:, cs] * pm_ref[rs,C_VR + r * RET_DV:C_VR + (r + 1) * RET_DV]
        o = o + o2_s[r * gsz:(r + 1) * gsz, :] * math.exp(LOG_G[r])
        oc = _group_norm_gate(o, gn_ref[:, cs], pm_ref[rs,C_GR + r * RET_DV:C_GR + (r + 1) * RET_DV])
        oc_ref[:, 2 * LRU_WIDTH + r * RET_DV:2 * LRU_WIDTH + (r + 1) * RET_DV] = oc.astype(BF16)


def _call_mix_sample(pm, ck, cv, conv, h0, sret, stacked, p, layer):
    nseq = pm.shape[0]
    depth = ck.shape[0]
    whole = pl.BlockSpec(memory_space=pl.ANY)
    gsz = G_SEQ

    def lspec(shape):
        nd = len(shape)
        return pl.BlockSpec((None,) + tuple(shape[1:]), lambda i: (layer,) + (0,) * (nd - 1))

    def seq2(width):
        return pl.BlockSpec((gsz, width), lambda i: (i, 0))

    def seq3(layered, d1, d2):
        if layered:
            return pl.BlockSpec((None, gsz, d1, d2), lambda i: (layer, i, 0, 0))
        return pl.BlockSpec((gsz, d1, d2), lambda i: (i, 0, 0))

    sdim = RET_HEADS * RET_DK
    return pl.pallas_call(
        _mixs_body,
        grid=(nseq // gsz,),
        in_specs=[seq2(N_MIX), seq3(True, WINDOW, 128), seq3(True, WINDOW, 128),
                  pl.BlockSpec((None, gsz, 3 * LRU_WIDTH), lambda i: (layer, i, 0)),
                  pl.BlockSpec((None, gsz, LRU_WIDTH), lambda i: (layer, i, 0)),
                  seq3(True, sdim, RET_DV),
                  lspec(p['conv_w'].shape), lspec(p['conv_b'].shape), lspec(p['lru_wa_bd'].shape),
                  lspec(p['lru_wx_bd'].shape), lspec(p['lru_b_a'].shape), lspec(p['lru_b_x'].shape),
                  lspec(p['lru_lambda'].shape), pl.BlockSpec((1, 128), lambda i: (0, 0)),
                  lspec(p['ret_norm'].shape), lspec(p['sinks16'].shape), whole, whole, whole],
        out_specs=[seq2(N_OC), seq3(True, WINDOW, 128), seq3(True, WINDOW, 128), seq2(3 * LRU_WIDTH),
                   seq2(LRU_WIDTH), seq3(True, sdim, RET_DV)],
        out_shape=[jax.ShapeDtypeStruct((nseq, N_OC), BF16),
                   jax.ShapeDtypeStruct((depth, nseq, WINDOW, 128), F32),
                   jax.ShapeDtypeStruct((depth, nseq, WINDOW, 128), F32),
                   jax.ShapeDtypeStruct((nseq, 3 * LRU_WIDTH), F32), jax.ShapeDtypeStruct((nseq, LRU_WIDTH), F32),
                   jax.ShapeDtypeStruct((depth, nseq, sdim, RET_DV), F32)],
        input_output_aliases={16: 1, 17: 2, 18: 5},
        scratch_shapes=[pltpu.VMEM((16 * gsz, 128), F32), pltpu.VMEM((16 * gsz, 128), F32),
                        pltpu.VMEM((2, 16 * gsz, 128), F32), pltpu.VMEM((2, 16 * gsz, 128), F32),
                        pltpu.VMEM((16 * gsz, 128), F32), pltpu.VMEM((16 * gsz, 128), F32)],
        compiler_params=pltpu.CompilerParams(dimension_semantics=("arbitrary",), vmem_limit_bytes=VMEM_LIMIT),
        name='mix_sample',
    )(pm, ck, cv, conv, h0, sret, p['conv_w'], p['conv_b'], p['lru_wa_bd'], p['lru_wx_bd'], p['lru_b_a'],
      p['lru_b_x'], p['lru_lambda'], p['rope_inv'], p['ret_norm'], p['sinks16'], *stacked)


def _block_diag(w):
    depth = w.shape[0]
    w = w.reshape(depth, 2, 4, LRU_BW, LRU_BW)
    eye = jnp.eye(4, dtype=w.dtype)
    return jnp.einsum('lsncd,nm->lsncmd', w, eye).reshape(depth, 2, 4 * LRU_BW, 4 * LRU_BW)


def kernel(x_prompt, x_sample, cache_swa_k, cache_swa_v, state_conv, state_lru, state_ret, meta_tokens, ffn1_norm,
           ffn1_w_gu, ffn1_w_down, mix_norm, w_in, conv_w, conv_b, lru_w_a, lru_b_a, lru_w_x, lru_b_x, lru_lambda,
           swa_sinks, ret_norm, w_branch_a, w_branch_b, w_branch_c, w_out, ffn2_norm, ffn2_w_gu, ffn2_w_down,
           final_norm):
    depth = w_in.shape[0]
    bsz, seq, _ = x_prompt.shape
    nseq = x_sample.shape[0]
    buf = cache_swa_k.shape[2]
    assert buf == WINDOW == BLK and x_sample.shape[1] == 1 and nseq % G_SEQ == 0
    t = seq + N_META
    pad = (-t) % BLK
    tp = t + pad
    assert (bsz * tp) % TM_DENSE == 0

    def row(v):
        return v.reshape(depth, 1, -1).astype(F32)

    def bf16(w):
        return w.astype(BF16)

    assert SWA_HEAD_DIM ** -0.5 == 0.125 and RET_DK ** -0.5 == 0.125
    cols = jnp.arange(N_MIX)
    col_scale = jnp.where(((cols >= C_QS) & (cols < C_KS)) | ((cols >= C_KR) & (cols < C_VR)), 0.125, 1.0)
    half = jnp.arange(128) % (RET_DK // 2)
    p = {
        'ffn1_norm': row(ffn1_norm), 'ffn1_w_gu': bf16(ffn1_w_gu), 'ffn1_w_down': bf16(ffn1_w_down),
        'mix_norm': row(mix_norm), 'w_mix': bf16(w_in[:, :, :N_MIX] * col_scale), 'w_gate': bf16(w_in[:, :, N_MIX:]),
        'conv_w': conv_w.astype(F32), 'conv_b': row(conv_b),
        'lru_wa_bd': _block_diag(lru_w_a).astype(BF16), 'lru_wx_bd': _block_diag(lru_w_x).astype(BF16),
        'lru_b_a': row(lru_b_a), 'lru_b_x': row(lru_b_x), 'lru_lambda': row(lru_lambda),
        'swa_sinks': swa_sinks.astype(F32),
        'sinks16': jnp.pad(jnp.broadcast_to(swa_sinks.astype(F32)[:, :, None], (depth, SWA_HEADS, 128)),
                           ((0, 0), (0, 16 - SWA_HEADS), (0, 0))),
        'ret_norm': row(ret_norm),
        'w_branch_a': bf16(w_branch_a), 'w_branch_b': bf16(w_branch_b),
        'w_branch_c': bf16(w_branch_c), 'w_out': bf16(w_out),
        'ffn2_norm': row(ffn2_norm), 'ffn2_w_gu': bf16(ffn2_w_gu), 'ffn2_w_down': bf16(ffn2_w_down),
        'final_norm': final_norm.reshape(1, D_MODEL).astype(F32),
        'rope_inv': (ROPE_BASE ** (-half.astype(F32) / (RET_DK // 2))).reshape(1, 128),
    }

    assert pad + N_META == BLK
    head = jnp.concatenate([jnp.zeros((pad, D_MODEL), F32), meta_tokens.astype(F32)], axis=0)
    xp = x_prompt.reshape(bsz * seq, D_MODEL)
    xs = x_sample.reshape(nseq, D_MODEL)
    ck = cache_swa_k.reshape(depth, nseq, buf, 128)
    cv = cache_swa_v.reshape(depth, nseq, buf, 128)
    conv = state_conv.reshape(depth, nseq, 3 * LRU_WIDTH)
    sret = state_ret.reshape(depth, nseq, RET_HEADS * RET_DK, RET_DV)
    cos, sin = _call_ropetab(p['rope_inv'], tp, pad)

    outs_p = [[] for _ in range(5)]
    outs_s = [[], []]
    stacked = (jnp.zeros(ck.shape, F32), jnp.zeros(cv.shape, F32), jnp.zeros(sret.shape, F32))
    for layer in range(depth):
        final = layer == depth - 1
        if layer == 0:
            x1, pm, pg, tails = _call_in(xp, p, layer, TM_DENSE, head=head, nblk=tp // BLK, seq=True)
        else:
            x1, pm, pg, tails = _call_in(xp, p, layer, TM_DENSE, seq=True)
        cl = tails[tp // BLK - 1::tp // BLK]
        xp, kl, vl, hl, sl = _call_mix_out(pm, cos, sin, x1, pg, p, layer, bsz, pad, final)
        for acc, o in zip(outs_p, (kl.reshape(bsz, buf, SWA_KV_HEADS, SWA_HEAD_DIM),
                                   vl.reshape(bsz, buf, SWA_KV_HEADS, SWA_HEAD_DIM),
                                   cl[:, 8 - (CONV_WIDTH - 1):, :], hl[:, 7, :],
                                   sl.reshape(bsz, RET_HEADS, RET_DK, RET_DV))):
            acc.append(o)

        x1, pm, pg = _call_in(xs, p, layer, nseq)
        oc, nk, nv, nc, nh, ns = _call_mix_sample(pm, ck, cv, conv, state_lru, sret, stacked, p, layer)
        stacked = (nk, nv, ns)
        xs = _call_out(x1, pg, oc, p, layer, nseq, final)
        outs_s[0].append(nc.reshape(nseq, CONV_WIDTH - 1, LRU_WIDTH))
        outs_s[1].append(nh)

    yp = xp.reshape(bsz, tp, D_MODEL)[:, pad + N_META:]
    ys = xs.reshape(nseq, 1, D_MODEL)
    nk, nv, ns = stacked
    return ((yp, ys) + tuple(jnp.stack(a) for a in outs_p)
            + (nk.reshape(depth, nseq, buf, SWA_KV_HEADS, SWA_HEAD_DIM),
               nv.reshape(depth, nseq, buf, SWA_KV_HEADS, SWA_HEAD_DIM),
               jnp.stack(outs_s[0]), jnp.stack(outs_s[1]),
               ns.reshape(depth, nseq, RET_HEADS, RET_DK, RET_DV)))
```

```python
import functools
import math

import jax
import jax.numpy as jnp
from jax import lax
from jax.experimental import pallas as pl
from jax.experimental.pallas import tpu as pltpu

F32 = jnp.float32
BF16 = jnp.bfloat16

D_MODEL = 1024
D_FF = 2048
N_META = 16
EPS = 1e-6
LRU_WIDTH = 512
LRU_BLOCKS = 8
LRU_BW = 64
CONV_WIDTH = 4
LRU_C = 8.0
SWA_HEAD_DIM = 64
SWA_HEADS = 8
SWA_KV_HEADS = 2
SWA_GROUP = 4
WINDOW = 128
RET_DK = 64
RET_DV = 128
RET_HEADS = 4
ROPE_BASE = 10000.0
GN_EPS = 1e-5
PAST_LEN = 8192

BLK = 128
N_MIX = 3328
N_GATE = 3 * D_MODEL
C_XA, C_YA, C_QS, C_KS, C_VS, C_QR, C_KR, C_VR, C_GR = 0, 512, 1024, 1536, 1664, 1792, 2048, 2304, 2816
C_BT = N_MIX
N_MIX_SEQ = N_MIX + LRU_WIDTH
N_OC = 3 * LRU_WIDTH
_MIX_CHUNKS = ((C_XA, C_YA, None), (C_YA, C_QS, jax.nn.gelu), (C_QS, C_KS, None), (C_KS, C_KR, None),
               (C_KR, C_KR + 512, None), (C_KR + 512, C_GR, None), (C_GR, N_MIX, jax.nn.silu))
LOG_G = tuple(math.log1p(-(2.0 ** (-5.0 - h))) for h in range(RET_HEADS))

TM_DENSE = 256
G_SEQ = 16
VMEM_LIMIT = 56 * 1024 * 1024


def _dot(a, b):
    return jnp.dot(a, b, preferred_element_type=F32)


def _dot_nt(a, b):
    return lax.dot_general(a, b, (((1,), (1,)), ((), ())), preferred_element_type=F32)


def _dot_tn(a, b):
    return lax.dot_general(a, b, (((0,), (0,)), ((), ())), preferred_element_type=F32)


def _rms(x, g):
    return x * lax.rsqrt(jnp.mean(x * x, axis=-1, keepdims=True) + EPS) * g


def _softplus(x):
    return jnp.maximum(x, 0.0) + jnp.log1p(jnp.exp(-jnp.abs(x)))


def _wt(ref, rows=None, cols=None):
    r = slice(None) if rows is None else slice(*rows)
    c = slice(None) if cols is None else slice(*cols)
    return ref[r, c]


def _drain(pieces):
    while True:
        try:
            next(pieces)
        except StopIteration as done:
            return done.value


FF_CHUNK = 256


def _ffn_pieces(u, wgu_ref, wd_ref):
    y = None
    act = None
    nchunk = D_FF // FF_CHUNK
    for c in range(nchunk + 1):
        if c < nchunk:
            lo, hi = c * FF_CHUNK, (c + 1) * FF_CHUNK
            gate = _dot(u, _wt(wgu_ref, cols=(lo, hi)))
            yield
            up = _dot(u, _wt(wgu_ref, cols=(D_FF + lo, D_FF + hi)))
            yield
        if c > 0:
            part = _dot(act, _wt(wd_ref, rows=((c - 1) * FF_CHUNK, c * FF_CHUNK)))
            y = part if y is None else y + part
            yield
        if c < nchunk:
            act = (jax.nn.silu(gate) * up).astype(BF16)
    return y


def _swiglu(u, wgu_ref, wd_ref):
    gu = _dot(u, _wt(wgu_ref))
    act = (jax.nn.silu(gu[:, :D_FF]) * gu[:, D_FF:]).astype(BF16)
    return _dot(act, _wt(wd_ref))


def _in_body(*refs, nblk=None, seq=False, ntile=None):
    if seq:
        (refs, (cw_ref, cb_ref, wa_ref, wx_ref, ba_ref, bx_ref, lam_ref),
         (x1_ref, pm_ref, pg_ref, xt_ref, xtail, u2_prev)) = refs[:-13], refs[-13:-6], refs[-6:]
    else:
        refs, (x1_ref, pm_ref, pg_ref) = refs[:-3], refs[-3:]
    if nblk is None:
        x_ref, n1_ref, wgu_ref, wd_ref, n2_ref, wm_ref, wg_ref = refs
        x = x_ref[...]
    else:
        xa_ref, xb_ref, head_ref, n1_ref, wgu_ref, wd_ref, n2_ref, wm_ref, wg_ref = refs
        first = 2 * jnp.minimum(pl.program_id(0), ntile - 1)
        x = jnp.concatenate([jnp.where(lax.rem(first + h, nblk) == 0, head_ref[...], ref[...])
                             for h, ref in enumerate((xa_ref, xb_ref))], axis=0)
    def project(u2):
        jobs = ([(wm_ref, pm_ref, lo, hi, fn) for lo, hi, fn in _MIX_CHUNKS]
                + [(wg_ref, pg_ref, c, c + LRU_WIDTH, jax.nn.sigmoid) for c in range(0, N_GATE, LRU_WIDTH)])
        side = iter(())
        pending = None
        for job in jobs + [None]:
            res = None if job is None else _dot(u2, _wt(job[0], cols=(job[2], job[3])))
            if pending is not None:
                val, (_, o_ref, lo, hi, fn) = pending
                if seq and o_ref is pm_ref and lo == C_XA:
                    side = _lru_front_pieces(val, xtail, cw_ref, cb_ref, wa_ref, wx_ref, ba_ref, bx_ref, lam_ref,
                                             pm_ref, xt_ref)
                else:
                    o_ref[:, lo:hi] = val if fn is None else fn(val)
            next(side, None)
            pending = (res, job)
            yield
        _drain(side)

    if not seq:
        x1 = x + 0.5 * _drain(_ffn_pieces(_rms(x, n1_ref[...]).astype(BF16), wgu_ref, wd_ref))
        x1_ref[...] = x1
        _drain(project(_rms(x1, n2_ref[...]).astype(BF16)))
        return

    @pl.when(pl.program_id(0) == 0)
    def _():
        xtail[...] = jnp.zeros_like(xtail)
        u2_prev[...] = jnp.zeros_like(u2_prev)

    proj = project(u2_prev[...])
    for _ in range(2):
        next(proj, None)
    got = {}

    def ffn():
        got['y'] = yield from _ffn_pieces(_rms(x, n1_ref[...]).astype(BF16), wgu_ref, wd_ref)

    ffn = ffn()
    nchunk = D_FF // FF_CHUNK
    nproj = len(_MIX_CHUNKS) + N_GATE // LRU_WIDTH
    for _ in range(nchunk):
        for _ in range(-(-(nproj - 4) // nchunk)):
            next(proj, None)
        for _ in range(3):
            next(ffn, None)
    _drain(ffn)
    x1 = x + 0.5 * got['y']
    x1_ref[...] = x1
    u2_prev[...] = _rms(x1, n2_ref[...]).astype(BF16)
    _drain(proj)


def _lru_front_pieces(xa, xtail, cw_ref, cb_ref, wa_ref, wx_ref, ba_ref, bx_ref, lam_ref, pm_ref, xt_ref):
    rows = xa.shape[0]
    row8 = lax.broadcasted_iota(jnp.int32, (8, 1), 0)
    hist = xtail[...]
    xc = cb_ref[...]
    for tap in range(CONV_WIDTH - 1):
        k = CONV_WIDTH - 1 - tap
        sh = pltpu.roll(xa, k, 0)
        top = jnp.where(row8 < k, pltpu.roll(hist, k, 0), sh[0:8, :])
        xc = xc + jnp.concatenate([top, sh[8:, :]], axis=0) * cw_ref[tap:tap + 1, :]
        yield
    xc = xc + xa * cw_ref[CONV_WIDTH - 1:CONV_WIDTH, :]
    xtail[...] = xa[rows - 8:rows, :]
    for h in range(rows // BLK):
        xt_ref[h] = xa[(h + 1) * BLK - 8:(h + 1) * BLK, :]
    yield
    a, bt = yield from _lru_gate_pieces(xc, wa_ref, wx_ref, ba_ref, bx_ref, lam_ref)
    pm_ref[:, C_XA:C_XA + LRU_WIDTH] = a
    pm_ref[:, C_BT:C_BT + LRU_WIDTH] = bt


def _merge_out(x1, pg, oc, wa_ref, wb_ref, wc_ref, wo_ref, n_ref, wgu_ref, wd_ref, fn_ref, final):
    g = pg
    merged = (g[:, :D_MODEL] * _dot(oc[:, :LRU_WIDTH], _wt(wa_ref))
              + g[:, D_MODEL:2 * D_MODEL] * _dot(oc[:, LRU_WIDTH:2 * LRU_WIDTH], _wt(wb_ref))
              + g[:, 2 * D_MODEL:] * _dot(oc[:, 2 * LRU_WIDTH:], _wt(wc_ref)))
    x2 = x1 + _dot(merged.astype(BF16), _wt(wo_ref))
    x3 = x2 + 0.5 * _swiglu(_rms(x2, n_ref[...]).astype(BF16), wgu_ref, wd_ref)
    if final:
        x3 = _rms(x3, fn_ref[...])
    return x3


def _merge_out_pieces(x1_ref, pg_ref, oc, wa_ref, wb_ref, wc_ref, wo_ref, n_ref, wgu_ref, wd_ref, fn_ref, out_ref,
                      final):
    half = D_MODEL // 2
    merged = [None, None]
    pending = None
    for b, w_ref in enumerate((wa_ref, wb_ref, wc_ref)):
        for c in range(2):
            nxt = (_dot(oc[:, b * LRU_WIDTH:(b + 1) * LRU_WIDTH], _wt(w_ref, cols=(c * half, (c + 1) * half))), c,
                   pg_ref[:, b * D_MODEL + c * half:b * D_MODEL + (c + 1) * half])
            if pending is not None:
                proj, pc, g = pending
                merged[pc] = g * proj if merged[pc] is None else merged[pc] + g * proj
            pending = nxt
            yield
    proj, pc, g = pending
    merged[pc] = merged[pc] + g * proj
    merged = jnp.concatenate(merged, axis=1).astype(BF16)
    x2 = []
    pending = None
    for c in range(0, D_MODEL, FF_CHUNK):
        nxt = (_dot(merged, _wt(wo_ref, cols=(c, c + FF_CHUNK))), c)
        if pending is not None:
            x2.append(x1_ref[:, pending[1]:pending[1] + FF_CHUNK] + pending[0])
        pending = nxt
        yield
    x2.append(x1_ref[:, pending[1]:pending[1] + FF_CHUNK] + pending[0])
    x2 = jnp.concatenate(x2, axis=1)
    u = _rms(x2, n_ref[...]).astype(BF16)
    yield
    y = yield from _ffn_pieces(u, wgu_ref, wd_ref)
    x3 = x2 + 0.5 * y
    out_ref[...] = _rms(x3, fn_ref[...]) if final else x3
    yield


def _alternate(first, second, ratio):
    live = [True, True]
    while any(live):
        for idx, (gen, count) in enumerate(((first, 1), (second, ratio))):
            for _ in range(count):
                if live[idx] and next(gen, StopIteration) is StopIteration:
                    live[idx] = False


def _out_body(x1_ref, pg_ref, oc_ref, wa_ref, wb_ref, wc_ref, wo_ref, n_ref, wgu_ref, wd_ref, fn_ref, out_ref,
              *, final):
    out_ref[...] = _merge_out(x1_ref[...], pg_ref[...], oc_ref[...], wa_ref, wb_ref, wc_ref, wo_ref, n_ref,
                              wgu_ref, wd_ref, fn_ref, final)


def _layer_spec(shape, layer):
    nd = len(shape)
    return pl.BlockSpec((None,) + tuple(shape[1:]), lambda *_: (layer,) + (0,) * (nd - 1),
                        pipeline_mode=pl.Buffered(1))


def _row_spec(tm, width):
    return pl.BlockSpec((tm, width), lambda i: (i, 0))


def _dense_params():
    return pltpu.CompilerParams(dimension_semantics=("arbitrary",), vmem_limit_bytes=VMEM_LIMIT)


def _prompt_block(f, nblk):
    return f - f // nblk - 1


_LRU_PARAMS = ('conv_w', 'conv_b', 'lru_wa_bd', 'lru_wx_bd', 'lru_b_a', 'lru_b_x', 'lru_lambda')


def _call_in(x, p, layer, tm, head=None, nblk=None, seq=False):
    if head is None:
        n = x.shape[0]
    else:
        assert tm == 2 * BLK
        n = (x.shape[0] // BLK + x.shape[0] // BLK // (nblk - 1)) * BLK
    ntile = n // tm

    def cur(i):
        return jnp.minimum(i, ntile - 1)

    def lag(i):
        return jnp.maximum(i - 1, 0) if seq else i

    if head is None:
        x_specs, x_args = [pl.BlockSpec((tm, D_MODEL), lambda i: (cur(i), 0))], (x,)
    else:
        last = x.shape[0] // BLK - 1

        def half(h):
            return pl.BlockSpec((BLK, D_MODEL),
                                lambda i: (jnp.clip(_prompt_block(2 * cur(i) + h, nblk), 0, last), 0))

        x_specs = [half(0), half(1), pl.BlockSpec((BLK, D_MODEL), lambda i: (0, 0))]
        x_args = (x, x, head)
    weights = ('ffn1_norm', 'ffn1_w_gu', 'ffn1_w_down', 'mix_norm', 'w_mix', 'w_gate') + (_LRU_PARAMS if seq else ())
    width = N_MIX_SEQ if seq else N_MIX
    out_specs = [pl.BlockSpec((tm, D_MODEL), lambda i: (cur(i), 0)), pl.BlockSpec((tm, width), lambda i: (lag(i), 0)),
                 pl.BlockSpec((tm, N_GATE), lambda i: (lag(i), 0))]
    out_shape = [jax.ShapeDtypeStruct((n, D_MODEL), F32), jax.ShapeDtypeStruct((n, width), F32),
                 jax.ShapeDtypeStruct((n, N_GATE), F32)]
    if seq:
        out_specs.append(pl.BlockSpec((tm // BLK, 8, LRU_WIDTH), lambda i: (lag(i), 0, 0)))
        out_shape.append(jax.ShapeDtypeStruct((n // BLK, 8, LRU_WIDTH), F32))
    return pl.pallas_call(
        functools.partial(_in_body, nblk=None if head is None else nblk, seq=seq, ntile=ntile),
        grid=(ntile + 1 if seq else ntile,),
        in_specs=x_specs + [_layer_spec(p[k].shape, layer) for k in weights],
        out_specs=out_specs,
        out_shape=out_shape,
        scratch_shapes=[pltpu.VMEM((8, LRU_WIDTH), F32), pltpu.VMEM((tm, D_MODEL), BF16)] if seq else [],
        compiler_params=_dense_params(),
        name='layer_in',
    )(*x_args, *[p[k] for k in weights])


_OUT_WEIGHTS = ('w_branch_a', 'w_branch_b', 'w_branch_c', 'w_out', 'ffn2_norm', 'ffn2_w_gu', 'ffn2_w_down')


def _call_out(x1, pg, oc, p, layer, tm, final):
    n = x1.shape[0]
    return pl.pallas_call(
        functools.partial(_out_body, final=final),
        grid=(n // tm,),
        in_specs=[_row_spec(tm, D_MODEL), _row_spec(tm, N_GATE), _row_spec(tm, N_OC)]
        + [_layer_spec(p[k].shape, layer) for k in _OUT_WEIGHTS]
        + [pl.BlockSpec((1, D_MODEL), lambda i: (0, 0))],
        out_specs=_row_spec(tm, D_MODEL),
        out_shape=jax.ShapeDtypeStruct((n, D_MODEL), F32),
        compiler_params=_dense_params(),
        name='layer_out',
    )(x1, pg, oc, *[p[k] for k in _OUT_WEIGHTS], p['final_norm'])


def _lru_gate_pieces(xc, wa_ref, wx_ref, ba_ref, bx_ref, lam_ref):
    xcb = xc.astype(BF16)
    half = LRU_WIDTH // 2
    rpre = jnp.concatenate([_dot(xcb[:, :half], wa_ref[0]), _dot(xcb[:, half:], wa_ref[1])], axis=1)
    yield
    ipre = jnp.concatenate([_dot(xcb[:, :half], wx_ref[0]), _dot(xcb[:, half:], wx_ref[1])], axis=1)
    yield
    r = jax.nn.sigmoid(rpre + ba_ref[...])
    yield
    i = jax.nn.sigmoid(ipre + bx_ref[...])
    yield
    log_a = -LRU_C * r * _softplus(-lam_ref[...])
    a = jnp.exp(log_a)
    yield
    z = -jnp.tanh(log_a) * (a * a + 1.0)
    return a, jnp.where(z > 0.0, z * lax.rsqrt(z), 0.0) * (i * xc)


def _lru_gates(*args):
    return _drain(_lru_gate_pieces(*args))


def _rope_operands(cos, sin):
    lane = lax.broadcasted_iota(jnp.int32, (1, 4 * RET_DK), 1)
    first_half = (lane & (RET_DK - 1)) < RET_DK // 2
    cos2 = jnp.concatenate([cos, cos], axis=1)
    sin2 = jnp.concatenate([sin, sin], axis=1)
    return cos2, jnp.where(first_half, -sin2, sin2), first_half


def _rope(x, cos2, sin_signed, first_half):
    swapped = jnp.where(first_half, pltpu.roll(x, 4 * RET_DK - RET_DK // 2, 1), pltpu.roll(x, RET_DK // 2, 1))
    return x * cos2 + swapped * sin_signed


def _group_norm_gate(o, gain, gate):
    mu = jnp.mean(o, axis=-1, keepdims=True)
    d = o - mu
    var = jnp.mean(d * d, axis=-1, keepdims=True)
    return d * lax.rsqrt(var + GN_EPS) * gain * gate


def _ropetab_body(inv_ref, cos_ref, sin_ref, *, rows_per_step, pad):
    rows = lax.broadcasted_iota(jnp.int32, (rows_per_step, 1), 0)
    pos = (pl.program_id(0) * rows_per_step + rows - pad).astype(F32)
    ang = pos * inv_ref[...]
    cos_ref[...] = jnp.cos(ang)
    sin_ref[...] = jnp.sin(ang)


def _call_ropetab(inv, tp, pad):
    nblk = tp // BLK
    rps = BLK * max(d for d in (8, 5, 4, 2, 1) if nblk % d == 0)
    spec = pl.BlockSpec((rps, 128), lambda i: (i, 0))
    return pl.pallas_call(
        functools.partial(_ropetab_body, rows_per_step=rps, pad=pad),
        grid=(tp // rps,),
        in_specs=[pl.BlockSpec((1, 128), lambda i: (0, 0))],
        out_specs=[spec, spec],
        out_shape=[jax.ShapeDtypeStruct((tp, 128), F32), jax.ShapeDtypeStruct((tp, 128), F32)],
        name='rope_tables',
    )(inv)


def _block_masks(j, pad):
    lo = lax.broadcasted_iota(jnp.int32, (1, 128), 1) < 64
    rows = lax.broadcasted_iota(jnp.int32, (BLK, 1), 0)
    return lo, (j * BLK + rows) >= pad


def _mix_lru(pm_ref, ocs, hcar, a_s, b_s, h_s, *, pad, j, cur, r0):
    rs = slice(r0, r0 + BLK)
    fresh = j == 0
    _, valid = _block_masks(j, pad)
    a = pm_ref[rs,C_XA:C_XA + LRU_WIDTH]
    bt = jnp.where(valid, pm_ref[rs,C_BT:C_BT + LRU_WIDTH], 0.0)
    ngrp = BLK // 8
    for c in range(LRU_WIDTH // 128):
        a_s[c] = a[:, c * 128:(c + 1) * 128]
        b_s[c] = bt[:, c * 128:(c + 1) * 128]
    yield
    h_in = jnp.where(fresh, 0.0, hcar[7:8, :])
    for c in range(LRU_WIDTH // 128):
        prods, sums = [], []
        for r in range(8):
            ar = a_s[c, pl.ds(r, ngrp, stride=8), :]
            br = b_s[c, pl.ds(r, ngrp, stride=8), :]
            prods.append(ar if r == 0 else ar * prods[-1])
            sums.append(br if r == 0 else ar * sums[-1] + br)
        carry = h_in[:, c * 128:(c + 1) * 128]
        carries = []
        for g in range(ngrp):
            carries.append(carry)
            carry = prods[7][g:g + 1, :] * carry + sums[7][g:g + 1, :]
        carries = jnp.concatenate(carries, axis=0)
        for r in range(8):
            h_s[c, pl.ds(r, ngrp, stride=8), :] = prods[r] * carries + sums[r]
        yield
    h = jnp.concatenate([h_s[c] for c in range(LRU_WIDTH // 128)], axis=1)
    hcar[...] = h[BLK - 8:BLK, :]
    o_a = h * pm_ref[rs,C_YA:C_YA + LRU_WIDTH]
    ocs[cur, rs,0:LRU_WIDTH] = o_a.astype(BF16)
    yield


def _mix_swa(sinks_ref, pm_ref, ocs, kprev, vprev, *, layer, pad, j, cur, r0):
    rs = slice(r0, r0 + BLK)
    lo, _ = _block_masks(j, pad)
    k = pm_ref[rs,C_KS:C_KS + 128]
    v = pm_ref[rs,C_VS:C_VS + 128]
    k_sw = pltpu.roll(k, 64, 1)
    v_sw = pltpu.roll(v, 64, 1)
    kdup = (jnp.where(lo, k, k_sw).astype(BF16), jnp.where(lo, k_sw, k).astype(BF16))
    vdup = (jnp.where(lo, v, v_sw).astype(BF16), jnp.where(lo, v_sw, v).astype(BF16))
    row4 = lax.broadcasted_iota(jnp.int32, (SWA_GROUP * BLK, 1), 0)
    t4 = row4 & (BLK - 1)
    col = lax.broadcasted_iota(jnp.int32, (1, 2 * BLK), 1)
    ok = ((j - 1) * BLK + col >= pad) & (col > t4) & (col <= t4 + BLK)
    heads = range(SWA_KV_HEADS)
    kcat, vcat, raw, sk, m, e, den = ([None] * SWA_KV_HEADS for _ in range(7))
    for h in heads:
        parts = []
        for g in range(SWA_GROUP):
            head = SWA_GROUP * h + g
            slab = pm_ref[rs,C_QS + (head // 2) * 128:C_QS + (head // 2 + 1) * 128]
            parts.append(jnp.where(lo if head % 2 == 0 else jnp.logical_not(lo), slab, 0.0))
        qst = jnp.concatenate(parts, axis=0).astype(BF16)
        kcat[h] = jnp.concatenate([kprev[h], kdup[h]], axis=0)
        vcat[h] = jnp.concatenate([vprev[h], vdup[h]], axis=0)
        raw[h] = _dot_nt(qst, kcat[h])
        yield
    for h in heads:
        sc = jnp.where(ok, raw[h], -jnp.inf)
        sk[h] = jnp.full((SWA_GROUP * BLK, 1), sinks_ref[layer, SWA_GROUP * h], F32)
        for g in range(1, SWA_GROUP):
            sk[h] = jnp.where(row4 >= g * BLK, sinks_ref[layer, SWA_GROUP * h + g], sk[h])
        m[h] = jnp.maximum(jnp.max(sc, axis=-1, keepdims=True), sk[h])
        yield
        e[h] = jnp.exp(sc - m[h])
        yield
        den[h] = jnp.sum(e[h], axis=-1, keepdims=True) + jnp.exp(sk[h] - m[h])
        raw[h] = _dot(e[h].astype(BF16), vcat[h])
        yield
    for h in heads:
        o = raw[h] / den[h]
        for sl in range(2):
            ge = 2 * sl
            slab = jnp.where(lo, o[ge * BLK:(ge + 1) * BLK, :], o[(ge + 1) * BLK:(ge + 2) * BLK, :])
            c0 = LRU_WIDTH + (2 * h + sl) * 128
            ocs[cur, rs,c0:c0 + 128] = slab.astype(BF16)
        kprev[h] = kdup[h]
        vprev[h] = vdup[h]
        yield


def _mix_ret(pm_ref, cos_ref, sin_ref, gn_ref, ocs, state, dec_t, cross_t, kdec_t, *, pad, j, cur, r0):
    rs = slice(r0, r0 + BLK)
    fresh = j == 0
    lo, valid = _block_masks(j, pad)
    cos2, sin_signed, first_half = _rope_operands(cos_ref[...], sin_ref[...])
    qc = _rope(pm_ref[rs,C_QR:C_QR + 256], cos2, sin_signed, first_half)
    yield
    kc = _rope(pm_ref[rs,C_KR:C_KR + 256], cos2, sin_signed, first_half)
    kc = jnp.where(valid, kc, 0.0)
    yield
    st = [jnp.where(fresh, 0.0, state[sl * 128:(sl + 1) * 128, :]) for sl in range(2)]
    yield
    heads = range(RET_HEADS)
    qm, kslab, vh, raw, cross = ([None] * RET_HEADS for _ in range(5))
    for h in heads:
        sl = h // 2
        half = lo if h % 2 == 0 else jnp.logical_not(lo)
        qm[h] = jnp.where(half, qc[:, sl * 128:(sl + 1) * 128], 0.0).astype(BF16)
        kslab[h] = kc[:, sl * 128:(sl + 1) * 128]
        vh[h] = jnp.where(valid, pm_ref[rs,C_VR + h * RET_DV:C_VR + (h + 1) * RET_DV], 0.0).astype(BF16)
        raw[h] = _dot_nt(qm[h], kslab[h].astype(BF16))
        yield
    for h in heads:
        sc = (raw[h] * dec_t[h]).astype(BF16)
        raw[h] = _dot(sc, vh[h])
        cross[h] = _dot(qm[h], st[h // 2].astype(BF16))
        yield
    for h in heads:
        o = raw[h] + cross[h] * cross_t[h]
        c0 = 2 * LRU_WIDTH + h * RET_DV
        oc = _group_norm_gate(o, gn_ref[:, h * RET_DV:(h + 1) * RET_DV],
                              pm_ref[rs,C_GR + h * RET_DV:C_GR + (h + 1) * RET_DV])
        ocs[cur, rs,c0:c0 + RET_DV] = oc.astype(BF16)
        yield
    for h in heads:
        half = lo if h % 2 == 0 else jnp.logical_not(lo)
        km = jnp.where(half, kslab[h] * kdec_t[h // 2], 0.0).astype(BF16)
        raw[h] = _dot_tn(km, vh[h])
        yield
    srow = lax.broadcasted_iota(jnp.int32, (128, 1), 0)
    for sl in range(2):
        gcol = jnp.where(srow < RET_DK, math.exp(BLK * LOG_G[2 * sl]), math.exp(BLK * LOG_G[2 * sl + 1]))
        state[sl * 128:(sl + 1) * 128, :] = gcol * st[sl] + (raw[2 * sl] + raw[2 * sl + 1])


def _mixout_body(sinks_ref, pm_ref, cos0_ref, sin0_ref, cos1_ref, sin1_ref, x1_ref, pg_ref, gn_ref,
                 wba_ref, wbb_ref, wbc_ref, wo_ref, n2_ref, wgu_ref, wd_ref, fn_ref,
                 out_ref, klast_ref, vlast_ref, hlast_ref, sret_ref,
                 ocs, hcar, kprev, vprev, state, dec_t, cross_t, kdec_t, snap_h, snap_s, a_s, b_s, h_s,
                 *, layer, pad, nblk, final):
    s = pl.program_id(0)

    @pl.when(s == 0)
    def _():
        lo = lax.broadcasted_iota(jnp.int32, (1, 128), 1) < 64
        ti = lax.broadcasted_iota(jnp.int32, (BLK, 1), 0).astype(F32)
        tj = lax.broadcasted_iota(jnp.int32, (1, BLK), 1).astype(F32)
        diff = ti - tj
        for h in range(RET_HEADS):
            dec_t[h] = jnp.exp(jnp.where(diff >= 0, diff * LOG_G[h], -jnp.inf))
            cross_t[h] = jnp.broadcast_to(jnp.exp((ti + 1.0) * LOG_G[h]), (BLK, 128))
        for sl in range(2):
            kdec_t[sl] = jnp.where(lo, jnp.exp((BLK - 1.0 - ti) * LOG_G[2 * sl]),
                                   jnp.exp((BLK - 1.0 - ti) * LOG_G[2 * sl + 1]))
        for ref in (ocs, hcar, kprev, vprev, state):
            ref[...] = jnp.zeros_like(ref)

    cur = lax.rem(s, 2)
    j0 = lax.rem(2 * s, nblk)
    j1 = lax.rem(2 * s + 1, nblk)
    dense = _merge_out_pieces(x1_ref, pg_ref, ocs[lax.rem(s + 1, 2)], wba_ref, wbb_ref, wbc_ref, wo_ref, n2_ref,
                              wgu_ref, wd_ref, fn_ref, out_ref, final)

    def lru(j, r0):
        return _mix_lru(pm_ref, ocs, hcar, a_s, b_s, h_s, pad=pad, j=j, cur=cur, r0=r0)

    def swa(j, r0):
        return _mix_swa(sinks_ref, pm_ref, ocs, kprev, vprev, layer=layer, pad=pad, j=j, cur=cur, r0=r0)

    def ret(j, r0, cos_ref, sin_ref):
        return _mix_ret(pm_ref, cos_ref, sin_ref, gn_ref, ocs, state, dec_t, cross_t, kdec_t,
                        pad=pad, j=j, cur=cur, r0=r0)

    def mixers():
        yield from lru(j0, 0)
        yield from swa(j0, 0)
        yield from ret(j0, 0, cos0_ref, sin0_ref)
        snap_h[...] = hcar[...]
        snap_s[...] = state[...]
        yield from lru(j1, BLK)
        yield from swa(j1, BLK)
        yield from ret(j1, BLK, cos1_ref, sin1_ref)

    _alternate(dense, mixers(), 2)

    def write_state(r0, hl, st):
        klast_ref[0] = pm_ref[r0:r0 + BLK, C_KS:C_KS + 128]
        vlast_ref[0] = pm_ref[r0:r0 + BLK, C_VS:C_VS + 128]
        hlast_ref[0] = hl[...]
        sret_ref[0] = st[...]

    @pl.when(j0 == nblk - 1)
    def _():
        write_state(0, snap_h, snap_s)

    @pl.when(j1 == nblk - 1)
    def _():
        write_state(BLK, hcar, state)


def _call_mix_out(pm, cos, sin, x1, pg, p, layer, bsz, pad, final):
    n = pm.shape[0]
    nb = n // BLK
    nblk = nb // bsz
    assert nb % 2 == 0 and nblk >= 2
    steps = nb // 2
    rows = 2 * BLK

    def lspec(shape):
        nd = len(shape)
        return pl.BlockSpec((None,) + tuple(shape[1:]), lambda s: (layer,) + (0,) * (nd - 1))

    def cur(width):
        return pl.BlockSpec((rows, width), lambda s: (jnp.minimum(s, steps - 1), 0))

    def prev(width):
        return pl.BlockSpec((rows, width), lambda s: (jnp.maximum(s - 1, 0), 0))

    def last(shape):
        return pl.BlockSpec((1,) + shape,
                            lambda s: (jnp.minimum(2 * s, nb - 1) // nblk,) + (0,) * len(shape))

    def tab(half):
        return pl.BlockSpec((BLK, 128), lambda s: (lax.rem(2 * s + half, nblk), 0))

    return pl.pallas_call(
        functools.partial(_mixout_body, layer=layer, pad=pad, nblk=nblk, final=final),
        grid=(steps + 1,),
        in_specs=[pl.BlockSpec(memory_space=pltpu.SMEM), cur(N_MIX_SEQ), tab(0), tab(0), tab(1), tab(1),
                  prev(D_MODEL), prev(N_GATE), lspec(p['ret_norm'].shape)]
        + [_layer_spec(p[k].shape, layer) for k in _OUT_WEIGHTS]
        + [pl.BlockSpec((1, D_MODEL), lambda s: (0, 0))],
        out_specs=[prev(D_MODEL), last((BLK, 128)), last((BLK, 128)), last((8, LRU_WIDTH)),
                   last((RET_HEADS * RET_DK, RET_DV))],
        out_shape=[jax.ShapeDtypeStruct((n, D_MODEL), F32),
                   jax.ShapeDtypeStruct((bsz, BLK, 128), F32), jax.ShapeDtypeStruct((bsz, BLK, 128), F32),
                   jax.ShapeDtypeStruct((bsz, 8, LRU_WIDTH), F32),
                   jax.ShapeDtypeStruct((bsz, RET_HEADS * RET_DK, RET_DV), F32)],
        scratch_shapes=[pltpu.VMEM((2, rows, N_OC), BF16), pltpu.VMEM((8, LRU_WIDTH), F32),
                        pltpu.VMEM((SWA_KV_HEADS, BLK, 128), BF16), pltpu.VMEM((SWA_KV_HEADS, BLK, 128), BF16),
                        pltpu.VMEM((RET_HEADS * RET_DK, RET_DV), F32),
                        pltpu.VMEM((RET_HEADS, BLK, BLK), F32), pltpu.VMEM((RET_HEADS, BLK, 128), F32),
                        pltpu.VMEM((2, BLK, 128), F32),
                        pltpu.VMEM((8, LRU_WIDTH), F32), pltpu.VMEM((RET_HEADS * RET_DK, RET_DV), F32)]
        + [pltpu.VMEM((LRU_WIDTH // 128, BLK, 128), F32)] * 3,
        compiler_params=pltpu.CompilerParams(dimension_semantics=("arbitrary",), vmem_limit_bytes=VMEM_LIMIT),
        name='mix_out',
    )(p['swa_sinks'], pm, cos, sin, cos, sin, x1, pg, p['ret_norm'], *[p[k] for k in _OUT_WEIGHTS],
      p['final_norm'])


def _mixs_body(pm_ref, ck_ref, cv_ref, conv_ref, h0_ref, s_ref, cw_ref, cb_ref, wa_ref, wx_ref, ba_ref, bx_ref,
               lam_ref, inv_ref, gn_ref, sk_ref, nk_all, nv_all, ns_all,
               oc_ref, nk_ref, nv_ref, nconv_ref, nh_ref, ns_ref,
               qb, o8, qr_s, kr_s, v4_s, o2_s):
    gsz = G_SEQ
    nrow = 16
    rs = slice(None)

    @pl.when(pl.program_id(0) == 0)
    def _():
        qb[...] = jnp.zeros_like(qb)
        qr_s[...] = jnp.zeros_like(qr_s)
        kr_s[...] = jnp.zeros_like(kr_s)
        v4_s[...] = jnp.zeros_like(v4_s)

    lane = lax.broadcasted_iota(jnp.int32, (1, 128), 1)
    lo = lane < 64

    xa = pm_ref[rs,C_XA:C_XA + LRU_WIDTH]
    h1 = conv_ref[:, LRU_WIDTH:2 * LRU_WIDTH]
    h2 = conv_ref[:, 2 * LRU_WIDTH:3 * LRU_WIDTH]
    xc = cb_ref[...] + conv_ref[:, 0:LRU_WIDTH] * cw_ref[0:1, :]
    xc = xc + h1 * cw_ref[1:2, :]
    xc = xc + h2 * cw_ref[2:3, :]
    xc = xc + xa * cw_ref[3:4, :]
    nconv_ref[:, 0:LRU_WIDTH] = h1
    nconv_ref[:, LRU_WIDTH:2 * LRU_WIDTH] = h2
    nconv_ref[:, 2 * LRU_WIDTH:3 * LRU_WIDTH] = xa
    a, bt = _lru_gates(xc, wa_ref, wx_ref, ba_ref, bx_ref, lam_ref)
    hn = bt + a * h0_ref[...]
    nh_ref[...] = hn
    oc_ref[:, 0:LRU_WIDTH] = (hn * pm_ref[rs,C_YA:C_YA + LRU_WIDTH]).astype(BF16)

    nk_ref[:, 0:WINDOW - 1, :] = ck_ref[:, 1:WINDOW, :]
    nv_ref[:, 0:WINDOW - 1, :] = cv_ref[:, 1:WINDOW, :]
    for b in range(gsz):
        nk_ref[b, WINDOW - 1:WINDOW, :] = pm_ref[b:b + 1, C_KS:C_KS + 128]
        nv_ref[b, WINDOW - 1:WINDOW, :] = pm_ref[b:b + 1, C_VS:C_VS + 128]

    for r in range(SWA_HEADS):
        h = r // SWA_GROUP
        slab = pm_ref[rs,C_QS + (r // 2) * 128:C_QS + (r // 2 + 1) * 128]
        if r % 2 != h:
            slab = pltpu.roll(slab, 64, 1)
        qb[r * gsz:(r + 1) * gsz, :] = jnp.where(lo if h == 0 else jnp.logical_not(lo), slab, 0.0)

    ang = float(PAST_LEN) * inv_ref[...]
    cos2, sin_signed, first_half = _rope_operands(jnp.cos(ang), jnp.sin(ang))
    qc = _rope(pm_ref[rs,C_QR:C_QR + 256], cos2, sin_signed, first_half)
    kc = _rope(pm_ref[rs,C_KR:C_KR + 256], cos2, sin_signed, first_half)
    lane256 = lax.broadcasted_iota(jnp.int32, (1, RET_HEADS * RET_DK), 1)
    for r in range(RET_HEADS):
        hm = (lane256 >= r * RET_DK) & (lane256 < (r + 1) * RET_DK)
        qm = jnp.where(hm, qc, 0.0)
        km = jnp.where(hm, kc, 0.0)
        for c in range(2):
            qr_s[c, r * gsz:(r + 1) * gsz, :] = qm[:, c * 128:(c + 1) * 128]
            kr_s[c, r * gsz:(r + 1) * gsz, :] = km[:, c * 128:(c + 1) * 128]
        v4_s[r * gsz:(r + 1) * gsz, :] = pm_ref[rs,C_VR + r * RET_DV:C_VR + (r + 1) * RET_DV]

    srow = lax.broadcasted_iota(jnp.int32, (RET_HEADS * RET_DK, 1), 0)
    gcol = jnp.full((RET_HEADS * RET_DK, 1), math.exp(LOG_G[0]), F32)
    for r in range(1, RET_HEADS):
        gcol = jnp.where(srow >= r * RET_DK, math.exp(LOG_G[r]), gcol)
    sk = sk_ref[:, 0:1]

    lanes = 4

    def per_group(i, carry):
        seqs = [i * lanes + q for q in range(lanes)]
        rows = [pl.ds(b, nrow, stride=gsz) for b in seqs]
        s = [_dot_nt(qb[r, :].astype(BF16), nk_ref[b].astype(BF16)) for b, r in zip(seqs, rows)]
        sb = [s_ref[b] for b in seqs]
        o2 = [_dot(qr_s[0, r, :].astype(BF16), x[0:128, :].astype(BF16))
              + _dot(qr_s[1, r, :].astype(BF16), x[128:256, :].astype(BF16)) for r, x in zip(rows, sb)]
        v4 = [v4_s[r, :].astype(BF16) for r in rows]
        kv = [jnp.concatenate([_dot_tn(kr_s[0, r, :].astype(BF16), v), _dot_tn(kr_s[1, r, :].astype(BF16), v)],
                              axis=0) for r, v in zip(rows, v4)]
        m = [jnp.maximum(jnp.max(x, axis=-1, keepdims=True), sk) for x in s]
        e = [jnp.exp(x - y) for x, y in zip(s, m)]
        den = [jnp.sum(x, axis=-1, keepdims=True) + jnp.exp(sk - y) for x, y in zip(e, m)]
        o = [_dot(x.astype(BF16), nv_ref[b].astype(BF16)) / d for x, b, d in zip(e, seqs, den)]
        for q, (b, r) in enumerate(zip(seqs, rows)):
            o2_s[r, :] = o2[q]
            ns_ref[b] = gcol * sb[q] + kv[q]
            o8[r, :] = o[q]
        return carry

    lax.fori_loop(0, gsz // lanes, per_group, 0)

    for sl in range(4):
        h = sl // 2
        ev = o8[(2 * sl) * gsz:(2 * sl + 1) * gsz, :]
        od = o8[(2 * sl + 1) * gsz:(2 * sl + 2) * gsz, :]
        if h != 0:
            ev = pltpu.roll(ev, 64, 1)
        if h != 1:
            od = pltpu.roll(od, 64, 1)
        oc_ref[:, LRU_WIDTH + sl * 128:LRU_WIDTH + (sl + 1) * 128] = jnp.where(lo, ev, od).astype(BF16)

    prod = qc * kc
    p_hi = prod.astype(BF16)
    p_lo = (prod - p_hi.astype(F32)).astype(BF16)
    er = lax.broadcasted_iota(jnp.int32, (RET_HEADS * RET_DK, RET_HEADS * RET_DV), 0) // RET_DK
    ec = lax.broadcasted_iota(jnp.int32, (RET_HEADS * RET_DK, RET_HEADS * RET_DV), 1) // RET_DV
    expand = jnp.where(er == ec, 1.0, 0.0).astype(BF16)
    qk = _dot(p_hi, expand) + _dot(p_lo, expand)
    for r in range(RET_HEADS):
        cs = slice(r * RET_DV, (r + 1) * RET_DV)
        o = qk[:, cs] * pm_ref[rs,C_VR + r * RET_DV:C_VR + (r + 1) * RET_DV]
        o = o + o2_s[r * gsz:(r + 1) * gsz, :] * math.exp(LOG_G[r])
        oc = _group_norm_gate(o, gn_ref[:, cs], pm_ref[rs,C_GR + r * RET_DV:C_GR + (r + 1) * RET_DV])
        oc_ref[:, 2 * LRU_WIDTH + r * RET_DV:2 * LRU_WIDTH + (r + 1) * RET_DV] = oc.astype(BF16)


def _call_mix_sample(pm, ck, cv, conv, h0, sret, stacked, p, layer):
    nseq = pm.shape[0]
    depth = ck.shape[0]
    whole = pl.BlockSpec(memory_space=pl.ANY)
    gsz = G_SEQ

    def lspec(shape):
        nd = len(shape)
        return pl.BlockSpec((None,) + tuple(shape[1:]), lambda i: (layer,) + (0,) * (nd - 1))

    def seq2(width):
        return pl.BlockSpec((gsz, width), lambda i: (i, 0))

    def seq3(layered, d1, d2):
        if layered:
            return pl.BlockSpec((None, gsz, d1, d2), lambda i: (layer, i, 0, 0))
        return pl.BlockSpec((gsz, d1, d2), lambda i: (i, 0, 0))

    sdim = RET_HEADS * RET_DK
    return pl.pallas_call(
        _mixs_body,
        grid=(nseq // gsz,),
        in_specs=[seq2(N_MIX), seq3(True, WINDOW, 128), seq3(True, WINDOW, 128),
                  pl.BlockSpec((None, gsz, 3 * LRU_WIDTH), lambda i: (layer, i, 0)),
                  pl.BlockSpec((None, gsz, LRU_WIDTH), lambda i: (layer, i, 0)),
                  seq3(True, sdim, RET_DV),
                  lspec(p['conv_w'].shape), lspec(p['conv_b'].shape), lspec(p['lru_wa_bd'].shape),
                  lspec(p['lru_wx_bd'].shape), lspec(p['lru_b_a'].shape), lspec(p['lru_b_x'].shape),
                  lspec(p['lru_lambda'].shape), pl.BlockSpec((1, 128), lambda i: (0, 0)),
                  lspec(p['ret_norm'].shape), lspec(p['sinks16'].shape), whole, whole, whole],
        out_specs=[seq2(N_OC), seq3(True, WINDOW, 128), seq3(True, WINDOW, 128), seq2(3 * LRU_WIDTH),
                   seq2(LRU_WIDTH), seq3(True, sdim, RET_DV)],
        out_shape=[jax.ShapeDtypeStruct((nseq, N_OC), BF16),
                   jax.ShapeDtypeStruct((depth, nseq, WINDOW, 128), F32),
                   jax.ShapeDtypeStruct((depth, nseq, WINDOW, 128), F32),
                   jax.ShapeDtypeStruct((nseq, 3 * LRU_WIDTH), F32), jax.ShapeDtypeStruct((nseq, LRU_WIDTH), F32),
                   jax.ShapeDtypeStruct((depth, nseq, sdim, RET_DV), F32)],
        input_output_aliases={16: 1, 17: 2, 18: 5},
        scratch_shapes=[pltpu.VMEM((16 * gsz, 128), F32), pltpu.VMEM((16 * gsz, 128), F32),
                        pltpu.VMEM((2, 16 * gsz, 128), F32), pltpu.VMEM((2, 16 * gsz, 128), F32),
                        pltpu.VMEM((16 * gsz, 128), F32), pltpu.VMEM((16 * gsz, 128), F32)],
        compiler_params=pltpu.CompilerParams(dimension_semantics=("arbitrary",), vmem_limit_bytes=VMEM_LIMIT),
        name='mix_sample',
    )(pm, ck, cv, conv, h0, sret, p['conv_w'], p['conv_b'], p['lru_wa_bd'], p['lru_wx_bd'], p['lru_b_a'],
      p['lru_b_x'], p['lru_lambda'], p['rope_inv'], p['ret_norm'], p['sinks16'], *stacked)


def _block_diag(w):
    depth = w.shape[0]
    w = w.reshape(depth, 2, 4, LRU_BW, LRU_BW)
    eye = jnp.eye(4, dtype=w.dtype)
    return jnp.einsum('lsncd,nm->lsncmd', w, eye).reshape(depth, 2, 4 * LRU_BW, 4 * LRU_BW)


def kernel(x_prompt, x_sample, cache_swa_k, cache_swa_v, state_conv, state_lru, state_ret, meta_tokens, ffn1_norm,
           ffn1_w_gu, ffn1_w_down, mix_norm, w_in, conv_w, conv_b, lru_w_a, lru_b_a, lru_w_x, lru_b_x, lru_lambda,
           swa_sinks, ret_norm, w_branch_a, w_branch_b, w_branch_c, w_out, ffn2_norm, ffn2_w_gu, ffn2_w_down,
           final_norm):
    depth = w_in.shape[0]
    bsz, seq, _ = x_prompt.shape
    nseq = x_sample.shape[0]
    buf = cache_swa_k.shape[2]
    assert buf == WINDOW == BLK and x_sample.shape[1] == 1 and nseq % G_SEQ == 0
    t = seq + N_META
    pad = (-t) % BLK
    tp = t + pad
    assert (bsz * tp) % TM_DENSE == 0

    def row(v):
        return v.reshape(depth, 1, -1).astype(F32)

    def bf16(w):
        return w.astype(BF16)

    assert SWA_HEAD_DIM ** -0.5 == 0.125 and RET_DK ** -0.5 == 0.125
    cols = jnp.arange(N_MIX)
    col_scale = jnp.where(((cols >= C_QS) & (cols < C_KS)) | ((cols >= C_KR) & (cols < C_VR)), 0.125, 1.0)
    half = jnp.arange(128) % (RET_DK // 2)
    p = {
        'ffn1_norm': row(ffn1_norm), 'ffn1_w_gu': bf16(ffn1_w_gu), 'ffn1_w_down': bf16(ffn1_w_down),
        'mix_norm': row(mix_norm), 'w_mix': bf16(w_in[:, :, :N_MIX] * col_scale), 'w_gate': bf16(w_in[:, :, N_MIX:]),
        'conv_w': conv_w.astype(F32), 'conv_b': row(conv_b),
        'lru_wa_bd': _block_diag(lru_w_a).astype(BF16), 'lru_wx_bd': _block_diag(lru_w_x).astype(BF16),
        'lru_b_a': row(lru_b_a), 'lru_b_x': row(lru_b_x), 'lru_lambda': row(lru_lambda),
        'swa_sinks': swa_sinks.astype(F32),
        'sinks16': jnp.pad(jnp.broadcast_to(swa_sinks.astype(F32)[:, :, None], (depth, SWA_HEADS, 128)),
                           ((0, 0), (0, 16 - SWA_HEADS), (0, 0))),
        'ret_norm': row(ret_norm),
        'w_branch_a': bf16(w_branch_a), 'w_branch_b': bf16(w_branch_b),
        'w_branch_c': bf16(w_branch_c), 'w_out': bf16(w_out),
        'ffn2_norm': row(ffn2_norm), 'ffn2_w_gu': bf16(ffn2_w_gu), 'ffn2_w_down': bf16(ffn2_w_down),
        'final_norm': final_norm.reshape(1, D_MODEL).astype(F32),
        'rope_inv': (ROPE_BASE ** (-half.astype(F32) / (RET_DK // 2))).reshape(1, 128),
    }

    assert pad + N_META == BLK
    head = jnp.concatenate([jnp.zeros((pad, D_MODEL), F32), meta_tokens.astype(F32)], axis=0)
    xp = x_prompt.reshape(bsz * seq, D_MODEL)
    xs = x_sample.reshape(nseq, D_MODEL)
    ck = cache_swa_k.reshape(depth, nseq, buf, 128)
    cv = cache_swa_v.reshape(depth, nseq, buf, 128)
    conv = state_conv.reshape(depth, nseq, 3 * LRU_WIDTH)
    sret = state_ret.reshape(depth, nseq, RET_HEADS * RET_DK, RET_DV)
    cos, sin = _call_ropetab(p['rope_inv'], tp, pad)

    outs_p = [[] for _ in range(5)]
    outs_s = [[], []]
    stacked = (jnp.zeros(ck.shape, F32), jnp.zeros(cv.shape, F32), jnp.zeros(sret.shape, F32))
    for layer in range(depth):
        final = layer == depth - 1
        if layer == 0:
            x1, pm, pg, tails = _call_in(xp, p, layer, TM_DENSE, head=head, nblk=tp // BLK, seq=True)
        else:
            x1, pm, pg, tails = _call_in(xp, p, layer, TM_DENSE, seq=True)
        cl = tails[tp // BLK - 1::tp // BLK]
        xp, kl, vl, hl, sl = _call_mix_out(pm, cos, sin, x1, pg, p, layer, bsz, pad, final)
        for acc, o in zip(outs_p, (kl.reshape(bsz, buf, SWA_KV_HEADS, SWA_HEAD_DIM),
                                   vl.reshape(bsz, buf, SWA_KV_HEADS, SWA_HEAD_DIM),
                                   cl[:, 8 - (CONV_WIDTH - 1):, :], hl[:, 7, :],
                                   sl.reshape(bsz, RET_HEADS, RET_DK, RET_DV))):
            acc.append(o)

        x1, pm, pg = _call_in(xs, p, layer, nseq)
        oc, nk, nv, nc, nh, ns = _call_mix_sample(pm, ck, cv, conv, state_lru, sret, stacked, p, layer)
        stacked = (nk, nv, ns)
        xs = _call_out(x1, pg, oc, p, layer, nseq, final)
        outs_s[0].append(nc.reshape(nseq, CONV_WIDTH - 1, LRU_WIDTH))
        outs_s[1].append(nh)

    yp = xp.reshape(bsz, tp, D_MODEL)[:, pad + N_META:]
    ys = xs.reshape(nseq, 1, D_MODEL)
    nk, nv, ns = stacked
    return ((yp, ys) + tuple(jnp.stack(a) for a in outs_p)
            + (nk.reshape(depth, nseq, buf, SWA_KV_HEADS, SWA_HEAD_DIM),
               nv.reshape(depth, nseq, buf, SWA_KV_HEADS, SWA_HEAD_DIM),
               jnp.stack(outs_s[0]), jnp.stack(outs_s[1]),
               ns.reshape(depth, nseq, RET_HEADS, RET_DK, RET_DV)))
```

```python
import functools
import math

import jax
import jax.numpy as jnp
from jax import lax
from jax.experimental import pallas as pl
from jax.experimental.pallas import tpu as pltpu

F32 = jnp.float32
BF16 = jnp.bfloat16

D_MODEL = 1024
D_FF = 2048
N_META = 16
EPS = 1e-6
LRU_WIDTH = 512
LRU_BLOCKS = 8
LRU_BW = 64
CONV_WIDTH = 4
LRU_C = 8.0
SWA_HEAD_DIM = 64
SWA_HEADS = 8
SWA_KV_HEADS = 2
SWA_GROUP = 4
WINDOW = 128
RET_DK = 64
RET_DV = 128
RET_HEADS = 4
ROPE_BASE = 10000.0
GN_EPS = 1e-5
PAST_LEN = 8192

BLK = 128
N_MIX = 3328
N_GATE = 3 * D_MODEL
C_XA, C_YA, C_QS, C_KS, C_VS, C_QR, C_KR, C_VR, C_GR = 0, 512, 1024, 1536, 1664, 1792, 2048, 2304, 2816
C_BT = N_MIX
N_MIX_SEQ = N_MIX + LRU_WIDTH
N_OC = 3 * LRU_WIDTH
_MIX_CHUNKS = ((C_XA, C_YA, None), (C_YA, C_QS, jax.nn.gelu), (C_QS, C_KS, None), (C_KS, C_KR, None),
               (C_KR, C_KR + 512, None), (C_KR + 512, C_GR, None), (C_GR, N_MIX, jax.nn.silu))
LOG_G = tuple(math.log1p(-(2.0 ** (-5.0 - h))) for h in range(RET_HEADS))

TM_DENSE = 256
G_SEQ = 16
VMEM_LIMIT = 56 * 1024 * 1024


def _dot(a, b):
    return jnp.dot(a, b, preferred_element_type=F32)


def _dot_nt(a, b):
    return lax.dot_general(a, b, (((1,), (1,)), ((), ())), preferred_element_type=F32)


def _dot_tn(a, b):
    return lax.dot_general(a, b, (((0,), (0,)), ((), ())), preferred_element_type=F32)


def _rms(x, g):
    return x * lax.rsqrt(jnp.mean(x * x, axis=-1, keepdims=True) + EPS) * g


def _softplus(x):
    return jnp.maximum(x, 0.0) + jnp.log1p(jnp.exp(-jnp.abs(x)))


def _wt(ref, rows=None, cols=None):
    r = slice(None) if rows is None else slice(*rows)
    c = slice(None) if cols is None else slice(*cols)
    return ref[r, c]


def _drain(pieces):
    while True:
        try:
            next(pieces)
        except StopIteration as done:
            return done.value


FF_CHUNK = 256


def _ffn_pieces(u, wgu_ref, wd_ref):
    y = None
    act = None
    nchunk = D_FF // FF_CHUNK
    for c in range(nchunk + 1):
        if c < nchunk:
            lo, hi = c * FF_CHUNK, (c + 1) * FF_CHUNK
            gate = _dot(u, _wt(wgu_ref, cols=(lo, hi)))
            yield
            up = _dot(u, _wt(wgu_ref, cols=(D_FF + lo, D_FF + hi)))
            yield
        if c > 0:
            part = _dot(act, _wt(wd_ref, rows=((c - 1) * FF_CHUNK, c * FF_CHUNK)))
            y = part if y is None else y + part
            yield
        if c < nchunk:
            act = (jax.nn.silu(gate) * up).astype(BF16)
    return y


def _swiglu(u, wgu_ref, wd_ref):
    gu = _dot(u, _wt(wgu_ref))
    act = (jax.nn.silu(gu[:, :D_FF]) * gu[:, D_FF:]).astype(BF16)
    return _dot(act, _wt(wd_ref))


def _in_body(*refs, nblk=None, seq=False):
    if seq:
        refs, (cw_ref, cb_ref, wa_ref, wx_ref, ba_ref, bx_ref, lam_ref), (x1_ref, pm_ref, pg_ref, xt_ref, xtail) = (
            refs[:-12], refs[-12:-5], refs[-5:])
    else:
        refs, (x1_ref, pm_ref, pg_ref) = refs[:-3], refs[-3:]
    if nblk is None:
        x_ref, n1_ref, wgu_ref, wd_ref, n2_ref, wm_ref, wg_ref = refs
        x = x_ref[...]
    else:
        xa_ref, xb_ref, head_ref, n1_ref, wgu_ref, wd_ref, n2_ref, wm_ref, wg_ref = refs
        first = 2 * pl.program_id(0)
        x = jnp.concatenate([jnp.where(lax.rem(first + h, nblk) == 0, head_ref[...], ref[...])
                             for h, ref in enumerate((xa_ref, xb_ref))], axis=0)
    if seq:
        @pl.when(pl.program_id(0) == 0)
        def _():
            xtail[...] = jnp.zeros_like(xtail)

    x1 = x + 0.5 * _drain(_ffn_pieces(_rms(x, n1_ref[...]).astype(BF16), wgu_ref, wd_ref))
    x1_ref[...] = x1
    u2 = _rms(x1, n2_ref[...]).astype(BF16)
    jobs = ([(wm_ref, pm_ref, lo, hi, fn) for lo, hi, fn in _MIX_CHUNKS]
            + [(wg_ref, pg_ref, c, c + LRU_WIDTH, jax.nn.sigmoid) for c in range(0, N_GATE, LRU_WIDTH)])
    side = iter(())
    pending = None
    for job in jobs + [None]:
        res = None if job is None else _dot(u2, _wt(job[0], cols=(job[2], job[3])))
        if pending is not None:
            val, (_, o_ref, lo, hi, fn) = pending
            if seq and o_ref is pm_ref and lo == C_XA:
                side = _lru_front_pieces(val, xtail, cw_ref, cb_ref, wa_ref, wx_ref, ba_ref, bx_ref, lam_ref,
                                         pm_ref, xt_ref)
            else:
                o_ref[:, lo:hi] = val if fn is None else fn(val)
        next(side, None)
        pending = (res, job)
    _drain(side)


def _lru_front_pieces(xa, xtail, cw_ref, cb_ref, wa_ref, wx_ref, ba_ref, bx_ref, lam_ref, pm_ref, xt_ref):
    rows = xa.shape[0]
    row8 = lax.broadcasted_iota(jnp.int32, (8, 1), 0)
    hist = xtail[...]
    xc = cb_ref[...]
    for tap in range(CONV_WIDTH - 1):
        k = CONV_WIDTH - 1 - tap
        sh = pltpu.roll(xa, k, 0)
        top = jnp.where(row8 < k, pltpu.roll(hist, k, 0), sh[0:8, :])
        xc = xc + jnp.concatenate([top, sh[8:, :]], axis=0) * cw_ref[tap:tap + 1, :]
        yield
    xc = xc + xa * cw_ref[CONV_WIDTH - 1:CONV_WIDTH, :]
    xtail[...] = xa[rows - 8:rows, :]
    for h in range(rows // BLK):
        xt_ref[h] = xa[(h + 1) * BLK - 8:(h + 1) * BLK, :]
    yield
    a, bt = yield from _lru_gate_pieces(xc, wa_ref, wx_ref, ba_ref, bx_ref, lam_ref)
    pm_ref[:, C_XA:C_XA + LRU_WIDTH] = a
    pm_ref[:, C_BT:C_BT + LRU_WIDTH] = bt


def _merge_out(x1, pg, oc, wa_ref, wb_ref, wc_ref, wo_ref, n_ref, wgu_ref, wd_ref, fn_ref, final):
    g = pg
    merged = (g[:, :D_MODEL] * _dot(oc[:, :LRU_WIDTH], _wt(wa_ref))
              + g[:, D_MODEL:2 * D_MODEL] * _dot(oc[:, LRU_WIDTH:2 * LRU_WIDTH], _wt(wb_ref))
              + g[:, 2 * D_MODEL:] * _dot(oc[:, 2 * LRU_WIDTH:], _wt(wc_ref)))
    x2 = x1 + _dot(merged.astype(BF16), _wt(wo_ref))
    x3 = x2 + 0.5 * _swiglu(_rms(x2, n_ref[...]).astype(BF16), wgu_ref, wd_ref)
    if final:
        x3 = _rms(x3, fn_ref[...])
    return x3


def _merge_out_pieces(x1_ref, pg_ref, oc, wa_ref, wb_ref, wc_ref, wo_ref, n_ref, wgu_ref, wd_ref, fn_ref, out_ref,
                      final):
    half = D_MODEL // 2
    merged = [None, None]
    pending = None
    for b, w_ref in enumerate((wa_ref, wb_ref, wc_ref)):
        for c in range(2):
            nxt = (_dot(oc[:, b * LRU_WIDTH:(b + 1) * LRU_WIDTH], _wt(w_ref, cols=(c * half, (c + 1) * half))), c,
                   pg_ref[:, b * D_MODEL + c * half:b * D_MODEL + (c + 1) * half])
            if pending is not None:
                proj, pc, g = pending
                merged[pc] = g * proj if merged[pc] is None else merged[pc] + g * proj
            pending = nxt
            yield
    proj, pc, g = pending
    merged[pc] = merged[pc] + g * proj
    merged = jnp.concatenate(merged, axis=1).astype(BF16)
    x2 = []
    pending = None
    for c in range(0, D_MODEL, FF_CHUNK):
        nxt = (_dot(merged, _wt(wo_ref, cols=(c, c + FF_CHUNK))), c)
        if pending is not None:
            x2.append(x1_ref[:, pending[1]:pending[1] + FF_CHUNK] + pending[0])
        pending = nxt
        yield
    x2.append(x1_ref[:, pending[1]:pending[1] + FF_CHUNK] + pending[0])
    x2 = jnp.concatenate(x2, axis=1)
    u = _rms(x2, n_ref[...]).astype(BF16)
    yield
    y = yield from _ffn_pieces(u, wgu_ref, wd_ref)
    x3 = x2 + 0.5 * y
    out_ref[...] = _rms(x3, fn_ref[...]) if final else x3
    yield


def _alternate(first, second, ratio):
    live = [True, True]
    while any(live):
        for idx, (gen, count) in enumerate(((first, 1), (second, ratio))):
            for _ in range(count):
                if live[idx] and next(gen, StopIteration) is StopIteration:
                    live[idx] = False


def _out_body(x1_ref, pg_ref, oc_ref, wa_ref, wb_ref, wc_ref, wo_ref, n_ref, wgu_ref, wd_ref, fn_ref, out_ref,
              *, final):
    out_ref[...] = _merge_out(x1_ref[...], pg_ref[...], oc_ref[...], wa_ref, wb_ref, wc_ref, wo_ref, n_ref,
                              wgu_ref, wd_ref, fn_ref, final)


def _layer_spec(shape, layer):
    nd = len(shape)
    return pl.BlockSpec((None,) + tuple(shape[1:]), lambda *_: (layer,) + (0,) * (nd - 1),
                        pipeline_mode=pl.Buffered(1))


def _row_spec(tm, width):
    return pl.BlockSpec((tm, width), lambda i: (i, 0))


def _dense_params():
    return pltpu.CompilerParams(dimension_semantics=("arbitrary",), vmem_limit_bytes=VMEM_LIMIT)


def _prompt_block(f, nblk):
    return f - f // nblk - 1


_LRU_PARAMS = ('conv_w', 'conv_b', 'lru_wa_bd', 'lru_wx_bd', 'lru_b_a', 'lru_b_x', 'lru_lambda')


def _call_in(x, p, layer, tm, head=None, nblk=None, seq=False):
    if head is None:
        n = x.shape[0]
        x_specs, x_args = [_row_spec(tm, D_MODEL)], (x,)
    else:
        assert tm == 2 * BLK
        n = (x.shape[0] // BLK + x.shape[0] // BLK // (nblk - 1)) * BLK
        last = x.shape[0] // BLK - 1

        def half(h):
            return pl.BlockSpec((BLK, D_MODEL),
                                lambda i: (jnp.clip(_prompt_block(2 * i + h, nblk), 0, last), 0))

        x_specs = [half(0), half(1), pl.BlockSpec((BLK, D_MODEL), lambda i: (0, 0))]
        x_args = (x, x, head)
    weights = ('ffn1_norm', 'ffn1_w_gu', 'ffn1_w_down', 'mix_norm', 'w_mix', 'w_gate') + (_LRU_PARAMS if seq else ())
    width = N_MIX_SEQ if seq else N_MIX
    out_specs = [_row_spec(tm, D_MODEL), _row_spec(tm, width), _row_spec(tm, N_GATE)]
    out_shape = [jax.ShapeDtypeStruct((n, D_MODEL), F32), jax.ShapeDtypeStruct((n, width), F32),
                 jax.ShapeDtypeStruct((n, N_GATE), F32)]
    if seq:
        out_specs.append(pl.BlockSpec((tm // BLK, 8, LRU_WIDTH), lambda i: (i, 0, 0)))
        out_shape.append(jax.ShapeDtypeStruct((n // BLK, 8, LRU_WIDTH), F32))
    return pl.pallas_call(
        functools.partial(_in_body, nblk=None if head is None else nblk, seq=seq),
        grid=(n // tm,),
        in_specs=x_specs + [_layer_spec(p[k].shape, layer) for k in weights],
        out_specs=out_specs,
        out_shape=out_shape,
        scratch_shapes=[pltpu.VMEM((8, LRU_WIDTH), F32)] if seq else [],
        compiler_params=_dense_params(),
        name='layer_in',
    )(*x_args, *[p[k] for k in weights])


_OUT_WEIGHTS = ('w_branch_a', 'w_branch_b', 'w_branch_c', 'w_out', 'ffn2_norm', 'ffn2_w_gu', 'ffn2_w_down')


def _call_out(x1, pg, oc, p, layer, tm, final):
    n = x1.shape[0]
    return pl.pallas_call(
        functools.partial(_out_body, final=final),
        grid=(n // tm,),
        in_specs=[_row_spec(tm, D_MODEL), _row_spec(tm, N_GATE), _row_spec(tm, N_OC)]
        + [_layer_spec(p[k].shape, layer) for k in _OUT_WEIGHTS]
        + [pl.BlockSpec((1, D_MODEL), lambda i: (0, 0))],
        out_specs=_row_spec(tm, D_MODEL),
        out_shape=jax.ShapeDtypeStruct((n, D_MODEL), F32),
        compiler_params=_dense_params(),
        name='layer_out',
    )(x1, pg, oc, *[p[k] for k in _OUT_WEIGHTS], p['final_norm'])


def _lru_gate_pieces(xc, wa_ref, wx_ref, ba_ref, bx_ref, lam_ref):
    xcb = xc.astype(BF16)
    half = LRU_WIDTH // 2
    rpre = jnp.concatenate([_dot(xcb[:, :half], wa_ref[0]), _dot(xcb[:, half:], wa_ref[1])], axis=1)
    yield
    ipre = jnp.concatenate([_dot(xcb[:, :half], wx_ref[0]), _dot(xcb[:, half:], wx_ref[1])], axis=1)
    yield
    r = jax.nn.sigmoid(rpre + ba_ref[...])
    yield
    i = jax.nn.sigmoid(ipre + bx_ref[...])
    yield
    log_a = -LRU_C * r * _softplus(-lam_ref[...])
    a = jnp.exp(log_a)
    yield
    z = -jnp.tanh(log_a) * (a * a + 1.0)
    return a, jnp.where(z > 0.0, z * lax.rsqrt(z), 0.0) * (i * xc)


def _lru_gates(*args):
    return _drain(_lru_gate_pieces(*args))


def _rope_operands(cos, sin):
    lane = lax.broadcasted_iota(jnp.int32, (1, 4 * RET_DK), 1)
    first_half = (lane & (RET_DK - 1)) < RET_DK // 2
    cos2 = jnp.concatenate([cos, cos], axis=1)
    sin2 = jnp.concatenate([sin, sin], axis=1)
    return cos2, jnp.where(first_half, -sin2, sin2), first_half


def _rope(x, cos2, sin_signed, first_half):
    swapped = jnp.where(first_half, pltpu.roll(x, 4 * RET_DK - RET_DK // 2, 1), pltpu.roll(x, RET_DK // 2, 1))
    return x * cos2 + swapped * sin_signed


def _group_norm_gate(o, gain, gate):
    mu = jnp.mean(o, axis=-1, keepdims=True)
    d = o - mu
    var = jnp.mean(d * d, axis=-1, keepdims=True)
    return d * lax.rsqrt(var + GN_EPS) * gain * gate


def _ropetab_body(inv_ref, cos_ref, sin_ref, *, rows_per_step, pad):
    rows = lax.broadcasted_iota(jnp.int32, (rows_per_step, 1), 0)
    pos = (pl.program_id(0) * rows_per_step + rows - pad).astype(F32)
    ang = pos * inv_ref[...]
    cos_ref[...] = jnp.cos(ang)
    sin_ref[...] = jnp.sin(ang)


def _call_ropetab(inv, tp, pad):
    nblk = tp // BLK
    rps = BLK * max(d for d in (8, 5, 4, 2, 1) if nblk % d == 0)
    spec = pl.BlockSpec((rps, 128), lambda i: (i, 0))
    return pl.pallas_call(
        functools.partial(_ropetab_body, rows_per_step=rps, pad=pad),
        grid=(tp // rps,),
        in_specs=[pl.BlockSpec((1, 128), lambda i: (0, 0))],
        out_specs=[spec, spec],
        out_shape=[jax.ShapeDtypeStruct((tp, 128), F32), jax.ShapeDtypeStruct((tp, 128), F32)],
        name='rope_tables',
    )(inv)


def _block_masks(j, pad):
    lo = lax.broadcasted_iota(jnp.int32, (1, 128), 1) < 64
    rows = lax.broadcasted_iota(jnp.int32, (BLK, 1), 0)
    return lo, (j * BLK + rows) >= pad


def _mix_lru(pm_ref, ocs, hcar, a_s, b_s, h_s, *, pad, j, cur, r0):
    rs = slice(r0, r0 + BLK)
    fresh = j == 0
    _, valid = _block_masks(j, pad)
    a = pm_ref[rs,C_XA:C_XA + LRU_WIDTH]
    bt = jnp.where(valid, pm_ref[rs,C_BT:C_BT + LRU_WIDTH], 0.0)
    ngrp = BLK // 8
    for c in range(LRU_WIDTH // 128):
        a_s[c] = a[:, c * 128:(c + 1) * 128]
        b_s[c] = bt[:, c * 128:(c + 1) * 128]
    yield
    h_in = jnp.where(fresh, 0.0, hcar[7:8, :])
    for c in range(LRU_WIDTH // 128):
        prods, sums = [], []
        for r in range(8):
            ar = a_s[c, pl.ds(r, ngrp, stride=8), :]
            br = b_s[c, pl.ds(r, ngrp, stride=8), :]
            prods.append(ar if r == 0 else ar * prods[-1])
            sums.append(br if r == 0 else ar * sums[-1] + br)
        carry = h_in[:, c * 128:(c + 1) * 128]
        carries = []
        for g in range(ngrp):
            carries.append(carry)
            carry = prods[7][g:g + 1, :] * carry + sums[7][g:g + 1, :]
        carries = jnp.concatenate(carries, axis=0)
        for r in range(8):
            h_s[c, pl.ds(r, ngrp, stride=8), :] = prods[r] * carries + sums[r]
        yield
    h = jnp.concatenate([h_s[c] for c in range(LRU_WIDTH // 128)], axis=1)
    hcar[...] = h[BLK - 8:BLK, :]
    o_a = h * pm_ref[rs,C_YA:C_YA + LRU_WIDTH]
    ocs[cur, rs,0:LRU_WIDTH] = o_a.astype(BF16)
    yield


def _mix_swa(sinks_ref, pm_ref, ocs, kprev, vprev, *, layer, pad, j, cur, r0):
    rs = slice(r0, r0 + BLK)
    lo, _ = _block_masks(j, pad)
    k = pm_ref[rs,C_KS:C_KS + 128]
    v = pm_ref[rs,C_VS:C_VS + 128]
    k_sw = pltpu.roll(k, 64, 1)
    v_sw = pltpu.roll(v, 64, 1)
    kdup = (jnp.where(lo, k, k_sw).astype(BF16), jnp.where(lo, k_sw, k).astype(BF16))
    vdup = (jnp.where(lo, v, v_sw).astype(BF16), jnp.where(lo, v_sw, v).astype(BF16))
    row4 = lax.broadcasted_iota(jnp.int32, (SWA_GROUP * BLK, 1), 0)
    t4 = row4 & (BLK - 1)
    col = lax.broadcasted_iota(jnp.int32, (1, 2 * BLK), 1)
    ok = ((j - 1) * BLK + col >= pad) & (col > t4) & (col <= t4 + BLK)
    for h in range(SWA_KV_HEADS):
        parts = []
        for g in range(SWA_GROUP):
            head = SWA_GROUP * h + g
            slab = pm_ref[rs,C_QS + (head // 2) * 128:C_QS + (head // 2 + 1) * 128]
            parts.append(jnp.where(lo if head % 2 == 0 else jnp.logical_not(lo), slab, 0.0))
        qst = jnp.concatenate(parts, axis=0).astype(BF16)
        kcat = jnp.concatenate([kprev[h], kdup[h]], axis=0)
        vcat = jnp.concatenate([vprev[h], vdup[h]], axis=0)
        yield
        sc = jnp.where(ok, _dot_nt(qst, kcat), -jnp.inf)
        yield
        sk = jnp.full((SWA_GROUP * BLK, 1), sinks_ref[layer, SWA_GROUP * h], F32)
        for g in range(1, SWA_GROUP):
            sk = jnp.where(row4 >= g * BLK, sinks_ref[layer, SWA_GROUP * h + g], sk)
        m = jnp.maximum(jnp.max(sc, axis=-1, keepdims=True), sk)
        yield
        e = jnp.exp(sc - m)
        yield
        den = jnp.sum(e, axis=-1, keepdims=True) + jnp.exp(sk - m)
        yield
        o = _dot(e.astype(BF16), vcat) / den
        yield
        for sl in range(2):
            ge = 2 * sl
            slab = jnp.where(lo, o[ge * BLK:(ge + 1) * BLK, :], o[(ge + 1) * BLK:(ge + 2) * BLK, :])
            c0 = LRU_WIDTH + (2 * h + sl) * 128
            ocs[cur, rs,c0:c0 + 128] = slab.astype(BF16)
        kprev[h] = kdup[h]
        vprev[h] = vdup[h]
        yield


def _mix_ret(pm_ref, cos_ref, sin_ref, gn_ref, ocs, state, dec_t, cross_t, kdec_t, *, pad, j, cur, r0):
    rs = slice(r0, r0 + BLK)
    fresh = j == 0
    lo, valid = _block_masks(j, pad)
    cos2, sin_signed, first_half = _rope_operands(cos_ref[...], sin_ref[...])
    qc = _rope(pm_ref[rs,C_QR:C_QR + 256], cos2, sin_signed, first_half)
    yield
    kc = _rope(pm_ref[rs,C_KR:C_KR + 256], cos2, sin_signed, first_half)
    kc = jnp.where(valid, kc, 0.0)
    yield
    st = [jnp.where(fresh, 0.0, state[sl * 128:(sl + 1) * 128, :]) for sl in range(2)]
    yield
    upd = [None, None]
    for h in range(RET_HEADS):
        sl = h // 2
        half = lo if h % 2 == 0 else jnp.logical_not(lo)
        qm = jnp.where(half, qc[:, sl * 128:(sl + 1) * 128], 0.0).astype(BF16)
        kslab = kc[:, sl * 128:(sl + 1) * 128]
        vh = jnp.where(valid, pm_ref[rs,C_VR + h * RET_DV:C_VR + (h + 1) * RET_DV], 0.0).astype(BF16)
        sc = (_dot_nt(qm, kslab.astype(BF16)) * dec_t[h]).astype(BF16)
        yield
        o = _dot(sc, vh)
        o = o + _dot(qm, st[sl].astype(BF16)) * cross_t[h]
        yield
        c0 = 2 * LRU_WIDTH + h * RET_DV
        oc = _group_norm_gate(o, gn_ref[:, h * RET_DV:(h + 1) * RET_DV],
                              pm_ref[rs,C_GR + h * RET_DV:C_GR + (h + 1) * RET_DV])
        ocs[cur, rs,c0:c0 + RET_DV] = oc.astype(BF16)
        yield
        km = jnp.where(half, kslab * kdec_t[sl], 0.0).astype(BF16)
        u = _dot_tn(km, vh)
        upd[sl] = u if upd[sl] is None else upd[sl] + u
        yield
    srow = lax.broadcasted_iota(jnp.int32, (128, 1), 0)
    for sl in range(2):
        gcol = jnp.where(srow < RET_DK, math.exp(BLK * LOG_G[2 * sl]), math.exp(BLK * LOG_G[2 * sl + 1]))
        state[sl * 128:(sl + 1) * 128, :] = gcol * st[sl] + upd[sl]


def _mixout_body(sinks_ref, pm_ref, cos0_ref, sin0_ref, cos1_ref, sin1_ref, x1_ref, pg_ref, gn_ref,
                 wba_ref, wbb_ref, wbc_ref, wo_ref, n2_ref, wgu_ref, wd_ref, fn_ref,
                 out_ref, klast_ref, vlast_ref, hlast_ref, sret_ref,
                 ocs, hcar, kprev, vprev, state, dec_t, cross_t, kdec_t, snap_h, snap_s, a_s, b_s, h_s,
                 *, layer, pad, nblk, final):
    s = pl.program_id(0)

    @pl.when(s == 0)
    def _():
        lo = lax.broadcasted_iota(jnp.int32, (1, 128), 1) < 64
        ti = lax.broadcasted_iota(jnp.int32, (BLK, 1), 0).astype(F32)
        tj = lax.broadcasted_iota(jnp.int32, (1, BLK), 1).astype(F32)
        diff = ti - tj
        for h in range(RET_HEADS):
            dec_t[h] = jnp.exp(jnp.where(diff >= 0, diff * LOG_G[h], -jnp.inf))
            cross_t[h] = jnp.broadcast_to(jnp.exp((ti + 1.0) * LOG_G[h]), (BLK, 128))
        for sl in range(2):
            kdec_t[sl] = jnp.where(lo, jnp.exp((BLK - 1.0 - ti) * LOG_G[2 * sl]),
                                   jnp.exp((BLK - 1.0 - ti) * LOG_G[2 * sl + 1]))
        for ref in (ocs, hcar, kprev, vprev, state):
            ref[...] = jnp.zeros_like(ref)

    cur = lax.rem(s, 2)
    j0 = lax.rem(2 * s, nblk)
    j1 = lax.rem(2 * s + 1, nblk)
    dense = _merge_out_pieces(x1_ref, pg_ref, ocs[lax.rem(s + 1, 2)], wba_ref, wbb_ref, wbc_ref, wo_ref, n2_ref,
                              wgu_ref, wd_ref, fn_ref, out_ref, final)

    def lru(j, r0):
        return _mix_lru(pm_ref, ocs, hcar, a_s, b_s, h_s, pad=pad, j=j, cur=cur, r0=r0)

    def swa(j, r0):
        return _mix_swa(sinks_ref, pm_ref, ocs, kprev, vprev, layer=layer, pad=pad, j=j, cur=cur, r0=r0)

    def ret(j, r0, cos_ref, sin_ref):
        return _mix_ret(pm_ref, cos_ref, sin_ref, gn_ref, ocs, state, dec_t, cross_t, kdec_t,
                        pad=pad, j=j, cur=cur, r0=r0)

    def mixers():
        yield from lru(j0, 0)
        yield from swa(j0, 0)
        yield from ret(j0, 0, cos0_ref, sin0_ref)
        snap_h[...] = hcar[...]
        snap_s[...] = state[...]
        yield from lru(j1, BLK)
        yield from swa(j1, BLK)
        yield from ret(j1, BLK, cos1_ref, sin1_ref)

    _alternate(dense, mixers(), 3)

    def write_state(r0, hl, st):
        klast_ref[0] = pm_ref[r0:r0 + BLK, C_KS:C_KS + 128]
        vlast_ref[0] = pm_ref[r0:r0 + BLK, C_VS:C_VS + 128]
        hlast_ref[0] = hl[...]
        sret_ref[0] = st[...]

    @pl.when(j0 == nblk - 1)
    def _():
        write_state(0, snap_h, snap_s)

    @pl.when(j1 == nblk - 1)
    def _():
        write_state(BLK, hcar, state)


def _call_mix_out(pm, cos, sin, x1, pg, p, layer, bsz, pad, final):
    n = pm.shape[0]
    nb = n // BLK
    nblk = nb // bsz
    assert nb % 2 == 0 and nblk >= 2
    steps = nb // 2
    rows = 2 * BLK

    def lspec(shape):
        nd = len(shape)
        return pl.BlockSpec((None,) + tuple(shape[1:]), lambda s: (layer,) + (0,) * (nd - 1))

    def cur(width):
        return pl.BlockSpec((rows, width), lambda s: (jnp.minimum(s, steps - 1), 0))

    def prev(width):
        return pl.BlockSpec((rows, width), lambda s: (jnp.maximum(s - 1, 0), 0))

    def last(shape):
        return pl.BlockSpec((1,) + shape,
                            lambda s: (jnp.minimum(2 * s, nb - 1) // nblk,) + (0,) * len(shape))

    def tab(half):
        return pl.BlockSpec((BLK, 128), lambda s: (lax.rem(2 * s + half, nblk), 0))

    return pl.pallas_call(
        functools.partial(_mixout_body, layer=layer, pad=pad, nblk=nblk, final=final),
        grid=(steps + 1,),
        in_specs=[pl.BlockSpec(memory_space=pltpu.SMEM), cur(N_MIX_SEQ), tab(0), tab(0), tab(1), tab(1),
                  prev(D_MODEL), prev(N_GATE), lspec(p['ret_norm'].shape)]
        + [_layer_spec(p[k].shape, layer) for k in _OUT_WEIGHTS]
        + [pl.BlockSpec((1, D_MODEL), lambda s: (0, 0))],
        out_specs=[prev(D_MODEL), last((BLK, 128)), last((BLK, 128)), last((8, LRU_WIDTH)),
                   last((RET_HEADS * RET_DK, RET_DV))],
        out_shape=[jax.ShapeDtypeStruct((n, D_MODEL), F32),
                   jax.ShapeDtypeStruct((bsz, BLK, 128), F32), jax.ShapeDtypeStruct((bsz, BLK, 128), F32),
                   jax.ShapeDtypeStruct((bsz, 8, LRU_WIDTH), F32),
                   jax.ShapeDtypeStruct((bsz, RET_HEADS * RET_DK, RET_DV), F32)],
        scratch_shapes=[pltpu.VMEM((2, rows, N_OC), BF16), pltpu.VMEM((8, LRU_WIDTH), F32),
                        pltpu.VMEM((SWA_KV_HEADS, BLK, 128), BF16), pltpu.VMEM((SWA_KV_HEADS, BLK, 128), BF16),
                        pltpu.VMEM((RET_HEADS * RET_DK, RET_DV), F32),
                        pltpu.VMEM((RET_HEADS, BLK, BLK), F32), pltpu.VMEM((RET_HEADS, BLK, 128), F32),
                        pltpu.VMEM((2, BLK, 128), F32),
                        pltpu.VMEM((8, LRU_WIDTH), F32), pltpu.VMEM((RET_HEADS * RET_DK, RET_DV), F32)]
        + [pltpu.VMEM((LRU_WIDTH // 128, BLK, 128), F32)] * 3,
        compiler_params=pltpu.CompilerParams(dimension_semantics=("arbitrary",), vmem_limit_bytes=VMEM_LIMIT),
        name='mix_out',
    )(p['swa_sinks'], pm, cos, sin, cos, sin, x1, pg, p['ret_norm'], *[p[k] for k in _OUT_WEIGHTS],
      p['final_norm'])


def _mixs_body(pm_ref, ck_ref, cv_ref, conv_ref, h0_ref, s_ref, cw_ref, cb_ref, wa_ref, wx_ref, ba_ref, bx_ref,
               lam_ref, inv_ref, gn_ref, sk_ref, nk_all, nv_all, ns_all,
               oc_ref, nk_ref, nv_ref, nconv_ref, nh_ref, ns_ref,
               qb, o8, qr_s, kr_s, v4_s, o2_s):
    gsz = G_SEQ
    nrow = 16
    rs = slice(None)

    @pl.when(pl.program_id(0) == 0)
    def _():
        qb[...] = jnp.zeros_like(qb)
        qr_s[...] = jnp.zeros_like(qr_s)
        kr_s[...] = jnp.zeros_like(kr_s)
        v4_s[...] = jnp.zeros_like(v4_s)

    lane = lax.broadcasted_iota(jnp.int32, (1, 128), 1)
    lo = lane < 64

    xa = pm_ref[rs,C_XA:C_XA + LRU_WIDTH]
    h1 = conv_ref[:, LRU_WIDTH:2 * LRU_WIDTH]
    h2 = conv_ref[:, 2 * LRU_WIDTH:3 * LRU_WIDTH]
    xc = cb_ref[...] + conv_ref[:, 0:LRU_WIDTH] * cw_ref[0:1, :]
    xc = xc + h1 * cw_ref[1:2, :]
    xc = xc + h2 * cw_ref[2:3, :]
    xc = xc + xa * cw_ref[3:4, :]
    nconv_ref[:, 0:LRU_WIDTH] = h1
    nconv_ref[:, LRU_WIDTH:2 * LRU_WIDTH] = h2
    nconv_ref[:, 2 * LRU_WIDTH:3 * LRU_WIDTH] = xa
    a, bt = _lru_gates(xc, wa_ref, wx_ref, ba_ref, bx_ref, lam_ref)
    hn = bt + a * h0_ref[...]
    nh_ref[...] = hn
    oc_ref[:, 0:LRU_WIDTH] = (hn * pm_ref[rs,C_YA:C_YA + LRU_WIDTH]).astype(BF16)

    nk_ref[:, 0:WINDOW - 1, :] = ck_ref[:, 1:WINDOW, :]
    nv_ref[:, 0:WINDOW - 1, :] = cv_ref[:, 1:WINDOW, :]
    for b in range(gsz):
        nk_ref[b, WINDOW - 1:WINDOW, :] = pm_ref[b:b + 1, C_KS:C_KS + 128]
        nv_ref[b, WINDOW - 1:WINDOW, :] = pm_ref[b:b + 1, C_VS:C_VS + 128]

    for r in range(SWA_HEADS):
        h = r // SWA_GROUP
        slab = pm_ref[rs,C_QS + (r // 2) * 128:C_QS + (r // 2 + 1) * 128]
        if r % 2 != h:
            slab = pltpu.roll(slab, 64, 1)
        qb[r * gsz:(r + 1) * gsz, :] = jnp.where(lo if h == 0 else jnp.logical_not(lo), slab, 0.0)

    ang = float(PAST_LEN) * inv_ref[...]
    cos2, sin_signed, first_half = _rope_operands(jnp.cos(ang), jnp.sin(ang))
    qc = _rope(pm_ref[rs,C_QR:C_QR + 256], cos2, sin_signed, first_half)
    kc = _rope(pm_ref[rs,C_KR:C_KR + 256], cos2, sin_signed, first_half)
    lane256 = lax.broadcasted_iota(jnp.int32, (1, RET_HEADS * RET_DK), 1)
    for r in range(RET_HEADS):
        hm = (lane256 >= r * RET_DK) & (lane256 < (r + 1) * RET_DK)
        qm = jnp.where(hm, qc, 0.0)
        km = jnp.where(hm, kc, 0.0)
        for c in range(2):
            qr_s[c, r * gsz:(r + 1) * gsz, :] = qm[:, c * 128:(c + 1) * 128]
            kr_s[c, r * gsz:(r + 1) * gsz, :] = km[:, c * 128:(c + 1) * 128]
        v4_s[r * gsz:(r + 1) * gsz, :] = pm_ref[rs,C_VR + r * RET_DV:C_VR + (r + 1) * RET_DV]

    srow = lax.broadcasted_iota(jnp.int32, (RET_HEADS * RET_DK, 1), 0)
    gcol = jnp.full((RET_HEADS * RET_DK, 1), math.exp(LOG_G[0]), F32)
    for r in range(1, RET_HEADS):
        gcol = jnp.where(srow >= r * RET_DK, math.exp(LOG_G[r]), gcol)
    sk = sk_ref[:, 0:1]

    lanes = 4

    def per_group(i, carry):
        seqs = [i * lanes + q for q in range(lanes)]
        rows = [pl.ds(b, nrow, stride=gsz) for b in seqs]
        s = [_dot_nt(qb[r, :].astype(BF16), nk_ref[b].astype(BF16)) for b, r in zip(seqs, rows)]
        sb = [s_ref[b] for b in seqs]
        o2 = [_dot(qr_s[0, r, :].astype(BF16), x[0:128, :].astype(BF16))
              + _dot(qr_s[1, r, :].astype(BF16), x[128:256, :].astype(BF16)) for r, x in zip(rows, sb)]
        v4 = [v4_s[r, :].astype(BF16) for r in rows]
        kv = [jnp.concatenate([_dot_tn(kr_s[0, r, :].astype(BF16), v), _dot_tn(kr_s[1, r, :].astype(BF16), v)],
                              axis=0) for r, v in zip(rows, v4)]
        m = [jnp.maximum(jnp.max(x, axis=-1, keepdims=True), sk) for x in s]
        e = [jnp.exp(x - y) for x, y in zip(s, m)]
        den = [jnp.sum(x, axis=-1, keepdims=True) + jnp.exp(sk - y) for x, y in zip(e, m)]
        o = [_dot(x.astype(BF16), nv_ref[b].astype(BF16)) / d for x, b, d in zip(e, seqs, den)]
        for q, (b, r) in enumerate(zip(seqs, rows)):
            o2_s[r, :] = o2[q]
            ns_ref[b] = gcol * sb[q] + kv[q]
            o8[r, :] = o[q]
        return carry

    lax.fori_loop(0, gsz // lanes, per_group, 0)

    for sl in range(4):
        h = sl // 2
        ev = o8[(2 * sl) * gsz:(2 * sl + 1) * gsz, :]
        od = o8[(2 * sl + 1) * gsz:(2 * sl + 2) * gsz, :]
        if h != 0:
            ev = pltpu.roll(ev, 64, 1)
        if h != 1:
            od = pltpu.roll(od, 64, 1)
        oc_ref[:, LRU_WIDTH + sl * 128:LRU_WIDTH + (sl + 1) * 128] = jnp.where(lo, ev, od).astype(BF16)

    prod = qc * kc
    p_hi = prod.astype(BF16)
    p_lo = (prod - p_hi.astype(F32)).astype(BF16)
    er = lax.broadcasted_iota(jnp.int32, (RET_HEADS * RET_DK, RET_HEADS * RET_DV), 0) // RET_DK
    ec = lax.broadcasted_iota(jnp.int32, (RET_HEADS * RET_DK, RET_HEADS * RET_DV), 1) // RET_DV
    expand = jnp.where(er == ec, 1.0, 0.0).astype(BF16)
    qk = _dot(p_hi, expand) + _dot(p_lo, expand)
    for r in range(RET_HEADS):
        cs = slice(r * RET_DV, (r + 1) * RET_DV)
        o = qk[:, cs] * pm_ref[rs,C_VR + r * RET_DV:C_VR + (r + 1) * RET_DV]
        o = o + o2_s[r * gsz:(r + 1) * gsz, :] * math.exp(LOG_G[r])
        oc = _group_norm_gate(o, gn_ref[:, cs], pm_ref[rs,C_GR + r * RET_DV:C_GR + (r + 1) * RET_DV])
        oc_ref[:, 2 * LRU_WIDTH + r * RET_DV:2 * LRU_WIDTH + (r + 1) * RET_DV] = oc.astype(BF16)


def _call_mix_sample(pm, ck, cv, conv, h0, sret, stacked, p, layer):
    nseq = pm.shape[0]
    depth = ck.shape[0]
    whole = pl.BlockSpec(memory_space=pl.ANY)
    gsz = G_SEQ

    def lspec(shape):
        nd = len(shape)
        return pl.BlockSpec((None,) + tuple(shape[1:]), lambda i: (layer,) + (0,) * (nd - 1))

    def seq2(width):
        return pl.BlockSpec((gsz, width), lambda i: (i, 0))

    def seq3(layered, d1, d2):
        if layered:
            return pl.BlockSpec((None, gsz, d1, d2), lambda i: (layer, i, 0, 0))
        return pl.BlockSpec((gsz, d1, d2), lambda i: (i, 0, 0))

    sdim = RET_HEADS * RET_DK
    return pl.pallas_call(
        _mixs_body,
        grid=(nseq // gsz,),
        in_specs=[seq2(N_MIX), seq3(True, WINDOW, 128), seq3(True, WINDOW, 128),
                  pl.BlockSpec((None, gsz, 3 * LRU_WIDTH), lambda i: (layer, i, 0)),
                  pl.BlockSpec((None, gsz, LRU_WIDTH), lambda i: (layer, i, 0)),
                  seq3(True, sdim, RET_DV),
                  lspec(p['conv_w'].shape), lspec(p['conv_b'].shape), lspec(p['lru_wa_bd'].shape),
                  lspec(p['lru_wx_bd'].shape), lspec(p['lru_b_a'].shape), lspec(p['lru_b_x'].shape),
                  lspec(p['lru_lambda'].shape), pl.BlockSpec((1, 128), lambda i: (0, 0)),
                  lspec(p['ret_norm'].shape), lspec(p['sinks16'].shape), whole, whole, whole],
        out_specs=[seq2(N_OC), seq3(True, WINDOW, 128), seq3(True, WINDOW, 128), seq2(3 * LRU_WIDTH),
                   seq2(LRU_WIDTH), seq3(True, sdim, RET_DV)],
        out_shape=[jax.ShapeDtypeStruct((nseq, N_OC), BF16),
                   jax.ShapeDtypeStruct((depth, nseq, WINDOW, 128), F32),
                   jax.ShapeDtypeStruct((depth, nseq, WINDOW, 128), F32),
                   jax.ShapeDtypeStruct((nseq, 3 * LRU_WIDTH), F32), jax.ShapeDtypeStruct((nseq, LRU_WIDTH), F32),
                   jax.ShapeDtypeStruct((depth, nseq, sdim, RET_DV), F32)],
        input_output_aliases={16: 1, 17: 2, 18: 5},
        scratch_shapes=[pltpu.VMEM((16 * gsz, 128), F32), pltpu.VMEM((16 * gsz, 128), F32),
                        pltpu.VMEM((2, 16 * gsz, 128), F32), pltpu.VMEM((2, 16 * gsz, 128), F32),
                        pltpu.VMEM((16 * gsz, 128), F32), pltpu.VMEM((16 * gsz, 128), F32)],
        compiler_params=pltpu.CompilerParams(dimension_semantics=("arbitrary",), vmem_limit_bytes=VMEM_LIMIT),
        name='mix_sample',
    )(pm, ck, cv, conv, h0, sret, p['conv_w'], p['conv_b'], p['lru_wa_bd'], p['lru_wx_bd'], p['lru_b_a'],
      p['lru_b_x'], p['lru_lambda'], p['rope_inv'], p['ret_norm'], p['sinks16'], *stacked)


def _block_diag(w):
    depth = w.shape[0]
    w = w.reshape(depth, 2, 4, LRU_BW, LRU_BW)
    eye = jnp.eye(4, dtype=w.dtype)
    return jnp.einsum('lsncd,nm->lsncmd', w, eye).reshape(depth, 2, 4 * LRU_BW, 4 * LRU_BW)


def kernel(x_prompt, x_sample, cache_swa_k, cache_swa_v, state_conv, state_lru, state_ret, meta_tokens, ffn1_norm,
           ffn1_w_gu, ffn1_w_down, mix_norm, w_in, conv_w, conv_b, lru_w_a, lru_b_a, lru_w_x, lru_b_x, lru_lambda,
           swa_sinks, ret_norm, w_branch_a, w_branch_b, w_branch_c, w_out, ffn2_norm, ffn2_w_gu, ffn2_w_down,
           final_norm):
    depth = w_in.shape[0]
    bsz, seq, _ = x_prompt.shape
    nseq = x_sample.shape[0]
    buf = cache_swa_k.shape[2]
    assert buf == WINDOW == BLK and x_sample.shape[1] == 1 and nseq % G_SEQ == 0
    t = seq + N_META
    pad = (-t) % BLK
    tp = t + pad
    assert (bsz * tp) % TM_DENSE == 0

    def row(v):
        return v.reshape(depth, 1, -1).astype(F32)

    def bf16(w):
        return w.astype(BF16)

    assert SWA_HEAD_DIM ** -0.5 == 0.125 and RET_DK ** -0.5 == 0.125
    cols = jnp.arange(N_MIX)
    col_scale = jnp.where(((cols >= C_QS) & (cols < C_KS)) | ((cols >= C_KR) & (cols < C_VR)), 0.125, 1.0)
    half = jnp.arange(128) % (RET_DK // 2)
    p = {
        'ffn1_norm': row(ffn1_norm), 'ffn1_w_gu': bf16(ffn1_w_gu), 'ffn1_w_down': bf16(ffn1_w_down),
        'mix_norm': row(mix_norm), 'w_mix': bf16(w_in[:, :, :N_MIX] * col_scale), 'w_gate': bf16(w_in[:, :, N_MIX:]),
        'conv_w': conv_w.astype(F32), 'conv_b': row(conv_b),
        'lru_wa_bd': _block_diag(lru_w_a).astype(BF16), 'lru_wx_bd': _block_diag(lru_w_x).astype(BF16),
        'lru_b_a': row(lru_b_a), 'lru_b_x': row(lru_b_x), 'lru_lambda': row(lru_lambda),
        'swa_sinks': swa_sinks.astype(F32),
        'sinks16': jnp.pad(jnp.broadcast_to(swa_sinks.astype(F32)[:, :, None], (depth, SWA_HEADS, 128)),
                           ((0, 0), (0, 16 - SWA_HEADS), (0, 0))),
        'ret_norm': row(ret_norm),
        'w_branch_a': bf16(w_branch_a), 'w_branch_b': bf16(w_branch_b),
        'w_branch_c': bf16(w_branch_c), 'w_out': bf16(w_out),
        'ffn2_norm': row(ffn2_norm), 'ffn2_w_gu': bf16(ffn2_w_gu), 'ffn2_w_down': bf16(ffn2_w_down),
        'final_norm': final_norm.reshape(1, D_MODEL).astype(F32),
        'rope_inv': (ROPE_BASE ** (-half.astype(F32) / (RET_DK // 2))).reshape(1, 128),
    }

    assert pad + N_META == BLK
    head = jnp.concatenate([jnp.zeros((pad, D_MODEL), F32), meta_tokens.astype(F32)], axis=0)
    xp = x_prompt.reshape(bsz * seq, D_MODEL)
    xs = x_sample.reshape(nseq, D_MODEL)
    ck = cache_swa_k.reshape(depth, nseq, buf, 128)
    cv = cache_swa_v.reshape(depth, nseq, buf, 128)
    conv = state_conv.reshape(depth, nseq, 3 * LRU_WIDTH)
    sret = state_ret.reshape(depth, nseq, RET_HEADS * RET_DK, RET_DV)
    cos, sin = _call_ropetab(p['rope_inv'], tp, pad)

    outs_p = [[] for _ in range(5)]
    outs_s = [[], []]
    stacked = (jnp.zeros(ck.shape, F32), jnp.zeros(cv.shape, F32), jnp.zeros(sret.shape, F32))
    for layer in range(depth):
        final = layer == depth - 1
        if layer == 0:
            x1, pm, pg, tails = _call_in(xp, p, layer, TM_DENSE, head=head, nblk=tp // BLK, seq=True)
        else:
            x1, pm, pg, tails = _call_in(xp, p, layer, TM_DENSE, seq=True)
        cl = tails[tp // BLK - 1::tp // BLK]
        xp, kl, vl, hl, sl = _call_mix_out(pm, cos, sin, x1, pg, p, layer, bsz, pad, final)
        for acc, o in zip(outs_p, (kl.reshape(bsz, buf, SWA_KV_HEADS, SWA_HEAD_DIM),
                                   vl.reshape(bsz, buf, SWA_KV_HEADS, SWA_HEAD_DIM),
                                   cl[:, 8 - (CONV_WIDTH - 1):, :], hl[:, 7, :],
                                   sl.reshape(bsz, RET_HEADS, RET_DK, RET_DV))):
            acc.append(o)

        x1, pm, pg = _call_in(xs, p, layer, nseq)
        oc, nk, nv, nc, nh, ns = _call_mix_sample(pm, ck, cv, conv, state_lru, sret, stacked, p, layer)
        stacked = (nk, nv, ns)
        xs = _call_out(x1, pg, oc, p, layer, nseq, final)
        outs_s[0].append(nc.reshape(nseq, CONV_WIDTH - 1, LRU_WIDTH))
        outs_s[1].append(nh)

    yp = xp.reshape(bsz, tp, D_MODEL)[:, pad + N_META:]
    ys = xs.reshape(nseq, 1, D_MODEL)
    nk, nv, ns = stacked
    return ((yp, ys) + tuple(jnp.stack(a) for a in outs_p)
            + (nk.reshape(depth, nseq, buf, SWA_KV_HEADS, SWA_HEAD_DIM),
               nv.reshape(depth, nseq, buf, SWA_KV_HEADS, SWA_HEAD_DIM),
               jnp.stack(outs_s[0]), jnp.stack(outs_s[1]),
               ns.reshape(depth, nseq, RET_HEADS, RET_DK, RET_DV)))
```

```python
import functools
import math

import jax
import jax.numpy as jnp
from jax import lax
from jax.experimental import pallas as pl
from jax.experimental.pallas import tpu as pltpu

F32 = jnp.float32
BF16 = jnp.bfloat16

D_MODEL = 1024
D_FF = 2048
N_META = 16
EPS = 1e-6
LRU_WIDTH = 512
LRU_BLOCKS = 8
LRU_BW = 64
CONV_WIDTH = 4
LRU_C = 8.0
SWA_HEAD_DIM = 64
SWA_HEADS = 8
SWA_KV_HEADS = 2
SWA_GROUP = 4
WINDOW = 128
RET_DK = 64
RET_DV = 128
RET_HEADS = 4
ROPE_BASE = 10000.0
GN_EPS = 1e-5
PAST_LEN = 8192

BLK = 128
N_MIX = 3328
N_GATE = 3 * D_MODEL
C_XA, C_YA, C_QS, C_KS, C_VS, C_QR, C_KR, C_VR, C_GR = 0, 512, 1024, 1536, 1664, 1792, 2048, 2304, 2816
C_BT = N_MIX
N_MIX_SEQ = N_MIX + LRU_WIDTH
N_OC = 3 * LRU_WIDTH
_MIX_CHUNKS = ((C_XA, C_YA, None), (C_YA, C_QS, jax.nn.gelu), (C_QS, C_KS, None), (C_KS, C_KR, None),
               (C_KR, C_KR + 512, None), (C_KR + 512, C_GR, None), (C_GR, N_MIX, jax.nn.silu))
LOG_G = tuple(math.log1p(-(2.0 ** (-5.0 - h))) for h in range(RET_HEADS))

TM_DENSE = 256
G_SEQ = 16
VMEM_LIMIT = 56 * 1024 * 1024


def _dot(a, b):
    return jnp.dot(a, b, preferred_element_type=F32)


def _dot_nt(a, b):
    return lax.dot_general(a, b, (((1,), (1,)), ((), ())), preferred_element_type=F32)


def _dot_tn(a, b):
    return lax.dot_general(a, b, (((0,), (0,)), ((), ())), preferred_element_type=F32)


def _rms(x, g):
    return x * lax.rsqrt(jnp.mean(x * x, axis=-1, keepdims=True) + EPS) * g


def _softplus(x):
    return jnp.maximum(x, 0.0) + jnp.log1p(jnp.exp(-jnp.abs(x)))


def _wt(ref, rows=None, cols=None):
    r = slice(None) if rows is None else slice(*rows)
    c = slice(None) if cols is None else slice(*cols)
    return ref[r, c]


def _drain(pieces):
    while True:
        try:
            next(pieces)
        except StopIteration as done:
            return done.value


FF_CHUNK = 256


def _ffn_pieces(u, wgu_ref, wd_ref):
    y = None
    act = None
    nchunk = D_FF // FF_CHUNK
    for c in range(nchunk + 1):
        if c < nchunk:
            lo, hi = c * FF_CHUNK, (c + 1) * FF_CHUNK
            gate = _dot(u, _wt(wgu_ref, cols=(lo, hi)))
            yield
            up = _dot(u, _wt(wgu_ref, cols=(D_FF + lo, D_FF + hi)))
            yield
        if c > 0:
            part = _dot(act, _wt(wd_ref, rows=((c - 1) * FF_CHUNK, c * FF_CHUNK)))
            y = part if y is None else y + part
            yield
        if c < nchunk:
            act = (jax.nn.silu(gate) * up).astype(BF16)
    return y


def _swiglu(u, wgu_ref, wd_ref):
    gu = _dot(u, _wt(wgu_ref))
    act = (jax.nn.silu(gu[:, :D_FF]) * gu[:, D_FF:]).astype(BF16)
    return _dot(act, _wt(wd_ref))


def _in_body(*refs, nblk=None, seq=False):
    if seq:
        refs, (cw_ref, cb_ref, wa_ref, wx_ref, ba_ref, bx_ref, lam_ref), (x1_ref, pm_ref, pg_ref, xt_ref, xtail) = (
            refs[:-12], refs[-12:-5], refs[-5:])
    else:
        refs, (x1_ref, pm_ref, pg_ref) = refs[:-3], refs[-3:]
    if nblk is None:
        x_ref, n1_ref, wgu_ref, wd_ref, n2_ref, wm_ref, wg_ref = refs
        x = x_ref[...]
    else:
        xa_ref, xb_ref, head_ref, n1_ref, wgu_ref, wd_ref, n2_ref, wm_ref, wg_ref = refs
        first = 2 * pl.program_id(0)
        x = jnp.concatenate([jnp.where(lax.rem(first + h, nblk) == 0, head_ref[...], ref[...])
                             for h, ref in enumerate((xa_ref, xb_ref))], axis=0)
    if seq:
        @pl.when(pl.program_id(0) == 0)
        def _():
            xtail[...] = jnp.zeros_like(xtail)

    x1 = x + 0.5 * _drain(_ffn_pieces(_rms(x, n1_ref[...]).astype(BF16), wgu_ref, wd_ref))
    x1_ref[...] = x1
    u2 = _rms(x1, n2_ref[...]).astype(BF16)
    jobs = ([(wm_ref, pm_ref, lo, hi, fn) for lo, hi, fn in _MIX_CHUNKS]
            + [(wg_ref, pg_ref, c, c + LRU_WIDTH, jax.nn.sigmoid) for c in range(0, N_GATE, LRU_WIDTH)])
    side = iter(())
    pending = None
    for job in jobs + [None]:
        res = None if job is None else _dot(u2, _wt(job[0], cols=(job[2], job[3])))
        if pending is not None:
            val, (_, o_ref, lo, hi, fn) = pending
            if seq and o_ref is pm_ref and lo == C_XA:
                side = _lru_front_pieces(val, xtail, cw_ref, cb_ref, wa_ref, wx_ref, ba_ref, bx_ref, lam_ref,
                                         pm_ref, xt_ref)
            else:
                o_ref[:, lo:hi] = val if fn is None else fn(val)
        next(side, None)
        pending = (res, job)
    _drain(side)


def _lru_front_pieces(xa, xtail, cw_ref, cb_ref, wa_ref, wx_ref, ba_ref, bx_ref, lam_ref, pm_ref, xt_ref):
    rows = xa.shape[0]
    row8 = lax.broadcasted_iota(jnp.int32, (8, 1), 0)
    hist = xtail[...]
    xc = cb_ref[...]
    for tap in range(CONV_WIDTH - 1):
        k = CONV_WIDTH - 1 - tap
        sh = pltpu.roll(xa, k, 0)
        top = jnp.where(row8 < k, pltpu.roll(hist, k, 0), sh[0:8, :])
        xc = xc + jnp.concatenate([top, sh[8:, :]], axis=0) * cw_ref[tap:tap + 1, :]
        yield
    xc = xc + xa * cw_ref[CONV_WIDTH - 1:CONV_WIDTH, :]
    xtail[...] = xa[rows - 8:rows, :]
    for h in range(rows // BLK):
        xt_ref[h] = xa[(h + 1) * BLK - 8:(h + 1) * BLK, :]
    yield
    a, bt = yield from _lru_gate_pieces(xc, wa_ref, wx_ref, ba_ref, bx_ref, lam_ref)
    pm_ref[:, C_XA:C_XA + LRU_WIDTH] = a
    pm_ref[:, C_BT:C_BT + LRU_WIDTH] = bt


def _merge_out(x1, pg, oc, wa_ref, wb_ref, wc_ref, wo_ref, n_ref, wgu_ref, wd_ref, fn_ref, final):
    g = pg
    merged = (g[:, :D_MODEL] * _dot(oc[:, :LRU_WIDTH], _wt(wa_ref))
              + g[:, D_MODEL:2 * D_MODEL] * _dot(oc[:, LRU_WIDTH:2 * LRU_WIDTH], _wt(wb_ref))
              + g[:, 2 * D_MODEL:] * _dot(oc[:, 2 * LRU_WIDTH:], _wt(wc_ref)))
    x2 = x1 + _dot(merged.astype(BF16), _wt(wo_ref))
    x3 = x2 + 0.5 * _swiglu(_rms(x2, n_ref[...]).astype(BF16), wgu_ref, wd_ref)
    if final:
        x3 = _rms(x3, fn_ref[...])
    return x3


def _merge_out_pieces(x1_ref, pg_ref, oc, wa_ref, wb_ref, wc_ref, wo_ref, n_ref, wgu_ref, wd_ref, fn_ref, out_ref,
                      final):
    half = D_MODEL // 2
    merged = [None, None]
    pending = None
    for b, w_ref in enumerate((wa_ref, wb_ref, wc_ref)):
        for c in range(2):
            nxt = (_dot(oc[:, b * LRU_WIDTH:(b + 1) * LRU_WIDTH], _wt(w_ref, cols=(c * half, (c + 1) * half))), c,
                   pg_ref[:, b * D_MODEL + c * half:b * D_MODEL + (c + 1) * half])
            if pending is not None:
                proj, pc, g = pending
                merged[pc] = g * proj if merged[pc] is None else merged[pc] + g * proj
            pending = nxt
            yield
    proj, pc, g = pending
    merged[pc] = merged[pc] + g * proj
    merged = jnp.concatenate(merged, axis=1).astype(BF16)
    x2 = []
    pending = None
    for c in range(0, D_MODEL, FF_CHUNK):
        nxt = (_dot(merged, _wt(wo_ref, cols=(c, c + FF_CHUNK))), c)
        if pending is not None:
            x2.append(x1_ref[:, pending[1]:pending[1] + FF_CHUNK] + pending[0])
        pending = nxt
        yield
    x2.append(x1_ref[:, pending[1]:pending[1] + FF_CHUNK] + pending[0])
    x2 = jnp.concatenate(x2, axis=1)
    u = _rms(x2, n_ref[...]).astype(BF16)
    yield
    y = yield from _ffn_pieces(u, wgu_ref, wd_ref)
    x3 = x2 + 0.5 * y
    out_ref[...] = _rms(x3, fn_ref[...]) if final else x3
    yield


def _alternate(first, second, ratio):
    live = [True, True]
    while any(live):
        for idx, (gen, count) in enumerate(((first, 1), (second, ratio))):
            for _ in range(count):
                if live[idx] and next(gen, StopIteration) is StopIteration:
                    live[idx] = False


def _out_body(x1_ref, pg_ref, oc_ref, wa_ref, wb_ref, wc_ref, wo_ref, n_ref, wgu_ref, wd_ref, fn_ref, out_ref,
              *, final):
    out_ref[...] = _merge_out(x1_ref[...], pg_ref[...], oc_ref[...], wa_ref, wb_ref, wc_ref, wo_ref, n_ref,
                              wgu_ref, wd_ref, fn_ref, final)


def _layer_spec(shape, layer):
    nd = len(shape)
    return pl.BlockSpec((None,) + tuple(shape[1:]), lambda *_: (layer,) + (0,) * (nd - 1),
                        pipeline_mode=pl.Buffered(1))


def _row_spec(tm, width):
    return pl.BlockSpec((tm, width), lambda i: (i, 0))


def _dense_params():
    return pltpu.CompilerParams(dimension_semantics=("arbitrary",), vmem_limit_bytes=VMEM_LIMIT)


def _prompt_block(f, nblk):
    return f - f // nblk - 1


_LRU_PARAMS = ('conv_w', 'conv_b', 'lru_wa_bd', 'lru_wx_bd', 'lru_b_a', 'lru_b_x', 'lru_lambda')


def _call_in(x, p, layer, tm, head=None, nblk=None, seq=False):
    if head is None:
        n = x.shape[0]
        x_specs, x_args = [_row_spec(tm, D_MODEL)], (x,)
    else:
        assert tm == 2 * BLK
        n = (x.shape[0] // BLK + x.shape[0] // BLK // (nblk - 1)) * BLK
        last = x.shape[0] // BLK - 1

        def half(h):
            return pl.BlockSpec((BLK, D_MODEL),
                                lambda i: (jnp.clip(_prompt_block(2 * i + h, nblk), 0, last), 0))

        x_specs = [half(0), half(1), pl.BlockSpec((BLK, D_MODEL), lambda i: (0, 0))]
        x_args = (x, x, head)
    weights = ('ffn1_norm', 'ffn1_w_gu', 'ffn1_w_down', 'mix_norm', 'w_mix', 'w_gate') + (_LRU_PARAMS if seq else ())
    width = N_MIX_SEQ if seq else N_MIX
    out_specs = [_row_spec(tm, D_MODEL), _row_spec(tm, width), _row_spec(tm, N_GATE)]
    out_shape = [jax.ShapeDtypeStruct((n, D_MODEL), F32), jax.ShapeDtypeStruct((n, width), F32),
                 jax.ShapeDtypeStruct((n, N_GATE), F32)]
    if seq:
        out_specs.append(pl.BlockSpec((tm // BLK, 8, LRU_WIDTH), lambda i: (i, 0, 0)))
        out_shape.append(jax.ShapeDtypeStruct((n // BLK, 8, LRU_WIDTH), F32))
    return pl.pallas_call(
        functools.partial(_in_body, nblk=None if head is None else nblk, seq=seq),
        grid=(n // tm,),
        in_specs=x_specs + [_layer_spec(p[k].shape, layer) for k in weights],
        out_specs=out_specs,
        out_shape=out_shape,
        scratch_shapes=[pltpu.VMEM((8, LRU_WIDTH), F32)] if seq else [],
        compiler_params=_dense_params(),
        name='layer_in',
    )(*x_args, *[p[k] for k in weights])


_OUT_WEIGHTS = ('w_branch_a', 'w_branch_b', 'w_branch_c', 'w_out', 'ffn2_norm', 'ffn2_w_gu', 'ffn2_w_down')


def _call_out(x1, pg, oc, p, layer, tm, final):
    n = x1.shape[0]
    return pl.pallas_call(
        functools.partial(_out_body, final=final),
        grid=(n // tm,),
        in_specs=[_row_spec(tm, D_MODEL), _row_spec(tm, N_GATE), _row_spec(tm, N_OC)]
        + [_layer_spec(p[k].shape, layer) for k in _OUT_WEIGHTS]
        + [pl.BlockSpec((1, D_MODEL), lambda i: (0, 0))],
        out_specs=_row_spec(tm, D_MODEL),
        out_shape=jax.ShapeDtypeStruct((n, D_MODEL), F32),
        compiler_params=_dense_params(),
        name='layer_out',
    )(x1, pg, oc, *[p[k] for k in _OUT_WEIGHTS], p['final_norm'])


def _lru_gate_pieces(xc, wa_ref, wx_ref, ba_ref, bx_ref, lam_ref):
    xcb = xc.astype(BF16)
    half = LRU_WIDTH // 2
    rpre = jnp.concatenate([_dot(xcb[:, :half], wa_ref[0]), _dot(xcb[:, half:], wa_ref[1])], axis=1)
    yield
    ipre = jnp.concatenate([_dot(xcb[:, :half], wx_ref[0]), _dot(xcb[:, half:], wx_ref[1])], axis=1)
    yield
    r = jax.nn.sigmoid(rpre + ba_ref[...])
    yield
    i = jax.nn.sigmoid(ipre + bx_ref[...])
    yield
    log_a = -LRU_C * r * _softplus(-lam_ref[...])
    a = jnp.exp(log_a)
    yield
    z = -jnp.tanh(log_a) * (a * a + 1.0)
    return a, jnp.where(z > 0.0, z * lax.rsqrt(z), 0.0) * (i * xc)


def _lru_gates(*args):
    return _drain(_lru_gate_pieces(*args))


def _rope_operands(cos, sin):
    lane = lax.broadcasted_iota(jnp.int32, (1, 4 * RET_DK), 1)
    first_half = (lane & (RET_DK - 1)) < RET_DK // 2
    cos2 = jnp.concatenate([cos, cos], axis=1)
    sin2 = jnp.concatenate([sin, sin], axis=1)
    return cos2, jnp.where(first_half, -sin2, sin2), first_half


def _rope(x, cos2, sin_signed, first_half):
    swapped = jnp.where(first_half, pltpu.roll(x, 4 * RET_DK - RET_DK // 2, 1), pltpu.roll(x, RET_DK // 2, 1))
    return x * cos2 + swapped * sin_signed


def _group_norm_gate(o, gain, gate):
    mu = jnp.mean(o, axis=-1, keepdims=True)
    d = o - mu
    var = jnp.mean(d * d, axis=-1, keepdims=True)
    return d * lax.rsqrt(var + GN_EPS) * gain * gate


def _ropetab_body(inv_ref, cos_ref, sin_ref, *, rows_per_step, pad):
    rows = lax.broadcasted_iota(jnp.int32, (rows_per_step, 1), 0)
    pos = (pl.program_id(0) * rows_per_step + rows - pad).astype(F32)
    ang = pos * inv_ref[...]
    cos_ref[...] = jnp.cos(ang)
    sin_ref[...] = jnp.sin(ang)


def _call_ropetab(inv, tp, pad):
    nblk = tp // BLK
    rps = BLK * max(d for d in (8, 5, 4, 2, 1) if nblk % d == 0)
    spec = pl.BlockSpec((rps, 128), lambda i: (i, 0))
    return pl.pallas_call(
        functools.partial(_ropetab_body, rows_per_step=rps, pad=pad),
        grid=(tp // rps,),
        in_specs=[pl.BlockSpec((1, 128), lambda i: (0, 0))],
        out_specs=[spec, spec],
        out_shape=[jax.ShapeDtypeStruct((tp, 128), F32), jax.ShapeDtypeStruct((tp, 128), F32)],
        name='rope_tables',
    )(inv)


def _block_masks(j, pad):
    lo = lax.broadcasted_iota(jnp.int32, (1, 128), 1) < 64
    rows = lax.broadcasted_iota(jnp.int32, (BLK, 1), 0)
    return lo, (j * BLK + rows) >= pad


def _mix_lru(pm_ref, ocs, hcar, a_s, b_s, h_s, *, pad, j, cur, r0):
    rs = slice(r0, r0 + BLK)
    fresh = j == 0
    _, valid = _block_masks(j, pad)
    a = pm_ref[rs,C_XA:C_XA + LRU_WIDTH]
    bt = jnp.where(valid, pm_ref[rs,C_BT:C_BT + LRU_WIDTH], 0.0)
    ngrp = BLK // 8
    for c in range(LRU_WIDTH // 128):
        a_s[c] = a[:, c * 128:(c + 1) * 128]
        b_s[c] = bt[:, c * 128:(c + 1) * 128]
    yield
    h_in = jnp.where(fresh, 0.0, hcar[7:8, :])
    for c in range(LRU_WIDTH // 128):
        prods, sums = [], []
        for r in range(8):
            ar = a_s[c, pl.ds(r, ngrp, stride=8), :]
            br = b_s[c, pl.ds(r, ngrp, stride=8), :]
            prods.append(ar if r == 0 else ar * prods[-1])
            sums.append(br if r == 0 else ar * sums[-1] + br)
        carry = h_in[:, c * 128:(c + 1) * 128]
        carries = []
        for g in range(ngrp):
            carries.append(carry)
            carry = prods[7][g:g + 1, :] * carry + sums[7][g:g + 1, :]
        carries = jnp.concatenate(carries, axis=0)
        for r in range(8):
            h_s[c, pl.ds(r, ngrp, stride=8), :] = prods[r] * carries + sums[r]
        yield
    h = jnp.concatenate([h_s[c] for c in range(LRU_WIDTH // 128)], axis=1)
    hcar[...] = h[BLK - 8:BLK, :]
    o_a = h * pm_ref[rs,C_YA:C_YA + LRU_WIDTH]
    ocs[cur, rs,0:LRU_WIDTH] = o_a.astype(BF16)
    yield


def _mix_swa(sinks_ref, pm_ref, ocs, kprev, vprev, *, layer, pad, j, cur, r0):
    rs = slice(r0, r0 + BLK)
    lo, _ = _block_masks(j, pad)
    k = pm_ref[rs,C_KS:C_KS + 128]
    v = pm_ref[rs,C_VS:C_VS + 128]
    k_sw = pltpu.roll(k, 64, 1)
    v_sw = pltpu.roll(v, 64, 1)
    kdup = (jnp.where(lo, k, k_sw).astype(BF16), jnp.where(lo, k_sw, k).astype(BF16))
    vdup = (jnp.where(lo, v, v_sw).astype(BF16), jnp.where(lo, v_sw, v).astype(BF16))
    row4 = lax.broadcasted_iota(jnp.int32, (SWA_GROUP * BLK, 1), 0)
    t4 = row4 & (BLK - 1)
    col = lax.broadcasted_iota(jnp.int32, (1, 2 * BLK), 1)
    ok = ((j - 1) * BLK + col >= pad) & (col > t4) & (col <= t4 + BLK)
    for h in range(SWA_KV_HEADS):
        parts = []
        for g in range(SWA_GROUP):
            head = SWA_GROUP * h + g
            slab = pm_ref[rs,C_QS + (head // 2) * 128:C_QS + (head // 2 + 1) * 128]
            parts.append(jnp.where(lo if head % 2 == 0 else jnp.logical_not(lo), slab, 0.0))
        qst = jnp.concatenate(parts, axis=0).astype(BF16)
        kcat = jnp.concatenate([kprev[h], kdup[h]], axis=0)
        vcat = jnp.concatenate([vprev[h], vdup[h]], axis=0)
        yield
        sc = jnp.where(ok, _dot_nt(qst, kcat), -jnp.inf)
        yield
        sk = jnp.full((SWA_GROUP * BLK, 1), sinks_ref[layer, SWA_GROUP * h], F32)
        for g in range(1, SWA_GROUP):
            sk = jnp.where(row4 >= g * BLK, sinks_ref[layer, SWA_GROUP * h + g], sk)
        m = jnp.maximum(jnp.max(sc, axis=-1, keepdims=True), sk)
        yield
        e = jnp.exp(sc - m)
        yield
        den = jnp.sum(e, axis=-1, keepdims=True) + jnp.exp(sk - m)
        yield
        o = _dot(e.astype(BF16), vcat) / den
        yield
        for sl in range(2):
            ge = 2 * sl
            slab = jnp.where(lo, o[ge * BLK:(ge + 1) * BLK, :], o[(ge + 1) * BLK:(ge + 2) * BLK, :])
            c0 = LRU_WIDTH + (2 * h + sl) * 128
            ocs[cur, rs,c0:c0 + 128] = slab.astype(BF16)
        kprev[h] = kdup[h]
        vprev[h] = vdup[h]
        yield


def _mix_ret(pm_ref, cos_ref, sin_ref, gn_ref, ocs, state, dec_t, cross_t, kdec_t, *, pad, j, cur, r0):
    rs = slice(r0, r0 + BLK)
    fresh = j == 0
    lo, valid = _block_masks(j, pad)
    cos2, sin_signed, first_half = _rope_operands(cos_ref[...], sin_ref[...])
    qc = _rope(pm_ref[rs,C_QR:C_QR + 256], cos2, sin_signed, first_half)
    yield
    kc = _rope(pm_ref[rs,C_KR:C_KR + 256], cos2, sin_signed, first_half)
    kc = jnp.where(valid, kc, 0.0)
    yield
    st = [jnp.where(fresh, 0.0, state[sl * 128:(sl + 1) * 128, :]) for sl in range(2)]
    yield
    upd = [None, None]
    for h in range(RET_HEADS):
        sl = h // 2
        half = lo if h % 2 == 0 else jnp.logical_not(lo)
        qm = jnp.where(half, qc[:, sl * 128:(sl + 1) * 128], 0.0).astype(BF16)
        kslab = kc[:, sl * 128:(sl + 1) * 128]
        vh = jnp.where(valid, pm_ref[rs,C_VR + h * RET_DV:C_VR + (h + 1) * RET_DV], 0.0).astype(BF16)
        sc = (_dot_nt(qm, kslab.astype(BF16)) * dec_t[h]).astype(BF16)
        yield
        o = _dot(sc, vh)
        o = o + _dot(qm, st[sl].astype(BF16)) * cross_t[h]
        yield
        c0 = 2 * LRU_WIDTH + h * RET_DV
        oc = _group_norm_gate(o, gn_ref[:, h * RET_DV:(h + 1) * RET_DV],
                              pm_ref[rs,C_GR + h * RET_DV:C_GR + (h + 1) * RET_DV])
        ocs[cur, rs,c0:c0 + RET_DV] = oc.astype(BF16)
        yield
        km = jnp.where(half, kslab * kdec_t[sl], 0.0).astype(BF16)
        u = _dot_tn(km, vh)
        upd[sl] = u if upd[sl] is None else upd[sl] + u
        yield
    srow = lax.broadcasted_iota(jnp.int32, (128, 1), 0)
    for sl in range(2):
        gcol = jnp.where(srow < RET_DK, math.exp(BLK * LOG_G[2 * sl]), math.exp(BLK * LOG_G[2 * sl + 1]))
        state[sl * 128:(sl + 1) * 128, :] = gcol * st[sl] + upd[sl]


def _mixout_body(sinks_ref, pm_ref, cos0_ref, sin0_ref, cos1_ref, sin1_ref, x1_ref, pg_ref, gn_ref,
                 wba_ref, wbb_ref, wbc_ref, wo_ref, n2_ref, wgu_ref, wd_ref, fn_ref,
                 out_ref, klast_ref, vlast_ref, hlast_ref, sret_ref,
                 ocs, hcar, kprev, vprev, state, dec_t, cross_t, kdec_t, snap_h, snap_s, a_s, b_s, h_s,
                 *, layer, pad, nblk, final):
    s = pl.program_id(0)

    @pl.when(s == 0)
    def _():
        lo = lax.broadcasted_iota(jnp.int32, (1, 128), 1) < 64
        ti = lax.broadcasted_iota(jnp.int32, (BLK, 1), 0).astype(F32)
        tj = lax.broadcasted_iota(jnp.int32, (1, BLK), 1).astype(F32)
        diff = ti - tj
        for h in range(RET_HEADS):
            dec_t[h] = jnp.exp(jnp.where(diff >= 0, diff * LOG_G[h], -jnp.inf))
            cross_t[h] = jnp.broadcast_to(jnp.exp((ti + 1.0) * LOG_G[h]), (BLK, 128))
        for sl in range(2):
            kdec_t[sl] = jnp.where(lo, jnp.exp((BLK - 1.0 - ti) * LOG_G[2 * sl]),
                                   jnp.exp((BLK - 1.0 - ti) * LOG_G[2 * sl + 1]))
        for ref in (ocs, hcar, kprev, vprev, state):
            ref[...] = jnp.zeros_like(ref)

    cur = lax.rem(s, 2)
    j0 = lax.rem(2 * s, nblk)
    j1 = lax.rem(2 * s + 1, nblk)
    dense = _merge_out_pieces(x1_ref, pg_ref, ocs[lax.rem(s + 1, 2)], wba_ref, wbb_ref, wbc_ref, wo_ref, n2_ref,
                              wgu_ref, wd_ref, fn_ref, out_ref, final)

    def lru(j, r0):
        return _mix_lru(pm_ref, ocs, hcar, a_s, b_s, h_s, pad=pad, j=j, cur=cur, r0=r0)

    def swa(j, r0):
        return _mix_swa(sinks_ref, pm_ref, ocs, kprev, vprev, layer=layer, pad=pad, j=j, cur=cur, r0=r0)

    def ret(j, r0, cos_ref, sin_ref):
        return _mix_ret(pm_ref, cos_ref, sin_ref, gn_ref, ocs, state, dec_t, cross_t, kdec_t,
                        pad=pad, j=j, cur=cur, r0=r0)

    def mixers():
        yield from lru(j0, 0)
        yield from swa(j0, 0)
        yield from ret(j0, 0, cos0_ref, sin0_ref)
        snap_h[...] = hcar[...]
        snap_s[...] = state[...]
        yield from lru(j1, BLK)
        yield from swa(j1, BLK)
        yield from ret(j1, BLK, cos1_ref, sin1_ref)

    _alternate(dense, mixers(), 3)

    def write_state(r0, hl, st):
        klast_ref[0] = pm_ref[r0:r0 + BLK, C_KS:C_KS + 128]
        vlast_ref[0] = pm_ref[r0:r0 + BLK, C_VS:C_VS + 128]
        hlast_ref[0] = hl[...]
        sret_ref[0] = st[...]

    @pl.when(j0 == nblk - 1)
    def _():
        write_state(0, snap_h, snap_s)

    @pl.when(j1 == nblk - 1)
    def _():
        write_state(BLK, hcar, state)


def _call_mix_out(pm, cos, sin, x1, pg, p, layer, bsz, pad, final):
    n = pm.shape[0]
    nb = n // BLK
    nblk = nb // bsz
    assert nb % 2 == 0 and nblk >= 2
    steps = nb // 2
    rows = 2 * BLK

    def lspec(shape):
        nd = len(shape)
        return pl.BlockSpec((None,) + tuple(shape[1:]), lambda s: (layer,) + (0,) * (nd - 1))

    def cur(width):
        return pl.BlockSpec((rows, width), lambda s: (jnp.minimum(s, steps - 1), 0))

    def prev(width):
        return pl.BlockSpec((rows, width), lambda s: (jnp.maximum(s - 1, 0), 0))

    def last(shape):
        return pl.BlockSpec((1,) + shape,
                            lambda s: (jnp.minimum(2 * s, nb - 1) // nblk,) + (0,) * len(shape))

    def tab(half):
        return pl.BlockSpec((BLK, 128), lambda s: (lax.rem(2 * s + half, nblk), 0))

    return pl.pallas_call(
        functools.partial(_mixout_body, layer=layer, pad=pad, nblk=nblk, final=final),
        grid=(steps + 1,),
        in_specs=[pl.BlockSpec(memory_space=pltpu.SMEM), cur(N_MIX_SEQ), tab(0), tab(0), tab(1), tab(1),
                  prev(D_MODEL), prev(N_GATE), lspec(p['ret_norm'].shape)]
        + [_layer_spec(p[k].shape, layer) for k in _OUT_WEIGHTS]
        + [pl.BlockSpec((1, D_MODEL), lambda s: (0, 0))],
        out_specs=[prev(D_MODEL), last((BLK, 128)), last((BLK, 128)), last((8, LRU_WIDTH)),
                   last((RET_HEADS * RET_DK, RET_DV))],
        out_shape=[jax.ShapeDtypeStruct((n, D_MODEL), F32),
                   jax.ShapeDtypeStruct((bsz, BLK, 128), F32), jax.ShapeDtypeStruct((bsz, BLK, 128), F32),
                   jax.ShapeDtypeStruct((bsz, 8, LRU_WIDTH), F32),
                   jax.ShapeDtypeStruct((bsz, RET_HEADS * RET_DK, RET_DV), F32)],
        scratch_shapes=[pltpu.VMEM((2, rows, N_OC), BF16), pltpu.VMEM((8, LRU_WIDTH), F32),
                        pltpu.VMEM((SWA_KV_HEADS, BLK, 128), BF16), pltpu.VMEM((SWA_KV_HEADS, BLK, 128), BF16),
                        pltpu.VMEM((RET_HEADS * RET_DK, RET_DV), F32),
                        pltpu.VMEM((RET_HEADS, BLK, BLK), F32), pltpu.VMEM((RET_HEADS, BLK, 128), F32),
                        pltpu.VMEM((2, BLK, 128), F32),
                        pltpu.VMEM((8, LRU_WIDTH), F32), pltpu.VMEM((RET_HEADS * RET_DK, RET_DV), F32)]
        + [pltpu.VMEM((LRU_WIDTH // 128, BLK, 128), F32)] * 3,
        compiler_params=pltpu.CompilerParams(dimension_semantics=("arbitrary",), vmem_limit_bytes=VMEM_LIMIT),
        name='mix_out',
    )(p['swa_sinks'], pm, cos, sin, cos, sin, x1, pg, p['ret_norm'], *[p[k] for k in _OUT_WEIGHTS],
      p['final_norm'])


def _mixs_body(pm_ref, ck_ref, cv_ref, conv_ref, h0_ref, s_ref, cw_ref, cb_ref, wa_ref, wx_ref, ba_ref, bx_ref,
               lam_ref, inv_ref, gn_ref, sk_ref, nk_all, nv_all, ns_all,
               oc_ref, nk_ref, nv_ref, nconv_ref, nh_ref, ns_ref,
               qb, o8, qr_s, kr_s, v4_s, o2_s):
    gsz = G_SEQ
    nrow = 16
    rs = slice(None)

    @pl.when(pl.program_id(0) == 0)
    def _():
        qb[...] = jnp.zeros_like(qb)
        qr_s[...] = jnp.zeros_like(qr_s)
        kr_s[...] = jnp.zeros_like(kr_s)
        v4_s[...] = jnp.zeros_like(v4_s)

    lane = lax.broadcasted_iota(jnp.int32, (1, 128), 1)
    lo = lane < 64

    xa = pm_ref[rs,C_XA:C_XA + LRU_WIDTH]
    h1 = conv_ref[:, LRU_WIDTH:2 * LRU_WIDTH]
    h2 = conv_ref[:, 2 * LRU_WIDTH:3 * LRU_WIDTH]
    xc = cb_ref[...] + conv_ref[:, 0:LRU_WIDTH] * cw_ref[0:1, :]
    xc = xc + h1 * cw_ref[1:2, :]
    xc = xc + h2 * cw_ref[2:3, :]
    xc = xc + xa * cw_ref[3:4, :]
    nconv_ref[:, 0:LRU_WIDTH] = h1
    nconv_ref[:, LRU_WIDTH:2 * LRU_WIDTH] = h2
    nconv_ref[:, 2 * LRU_WIDTH:3 * LRU_WIDTH] = xa
    a, bt = _lru_gates(xc, wa_ref, wx_ref, ba_ref, bx_ref, lam_ref)
    hn = bt + a * h0_ref[...]
    nh_ref[...] = hn
    oc_ref[:, 0:LRU_WIDTH] = (hn * pm_ref[rs,C_YA:C_YA + LRU_WIDTH]).astype(BF16)

    nk_ref[:, 0:WINDOW - 1, :] = ck_ref[:, 1:WINDOW, :]
    nv_ref[:, 0:WINDOW - 1, :] = cv_ref[:, 1:WINDOW, :]
    for b in range(gsz):
        nk_ref[b, WINDOW - 1:WINDOW, :] = pm_ref[b:b + 1, C_KS:C_KS + 128]
        nv_ref[b, WINDOW - 1:WINDOW, :] = pm_ref[b:b + 1, C_VS:C_VS + 128]

    for r in range(SWA_HEADS):
        h = r // SWA_GROUP
        slab = pm_ref[rs,C_QS + (r // 2) * 128:C_QS + (r // 2 + 1) * 128]
        if r % 2 != h:
            slab = pltpu.roll(slab, 64, 1)
        qb[r * gsz:(r + 1) * gsz, :] = jnp.where(lo if h == 0 else jnp.logical_not(lo), slab, 0.0)

    ang = float(PAST_LEN) * inv_ref[...]
    cos2, sin_signed, first_half = _rope_operands(jnp.cos(ang), jnp.sin(ang))
    qc = _rope(pm_ref[rs,C_QR:C_QR + 256], cos2, sin_signed, first_half)
    kc = _rope(pm_ref[rs,C_KR:C_KR + 256], cos2, sin_signed, first_half)
    lane256 = lax.broadcasted_iota(jnp.int32, (1, RET_HEADS * RET_DK), 1)
    for r in range(RET_HEADS):
        hm = (lane256 >= r * RET_DK) & (lane256 < (r + 1) * RET_DK)
        qm = jnp.where(hm, qc, 0.0)
        km = jnp.where(hm, kc, 0.0)
        for c in range(2):
            qr_s[c, r * gsz:(r + 1) * gsz, :] = qm[:, c * 128:(c + 1) * 128]
            kr_s[c, r * gsz:(r + 1) * gsz, :] = km[:, c * 128:(c + 1) * 128]
        v4_s[r * gsz:(r + 1) * gsz, :] = pm_ref[rs,C_VR + r * RET_DV:C_VR + (r + 1) * RET_DV]

    srow = lax.broadcasted_iota(jnp.int32, (RET_HEADS * RET_DK, 1), 0)
    gcol = jnp.full((RET_HEADS * RET_DK, 1), math.exp(LOG_G[0]), F32)
    for r in range(1, RET_HEADS):
        gcol = jnp.where(srow >= r * RET_DK, math.exp(LOG_G[r]), gcol)
    sk = sk_ref[:, 0:1]

    lanes = 8

    def per_group(i, carry):
        seqs = [i * lanes + q for q in range(lanes)]
        rows = [pl.ds(b, nrow, stride=gsz) for b in seqs]
        s = [_dot_nt(qb[r, :].astype(BF16), nk_ref[b].astype(BF16)) for b, r in zip(seqs, rows)]
        sb = [s_ref[b] for b in seqs]
        o2 = [_dot(qr_s[0, r, :].astype(BF16), x[0:128, :].astype(BF16))
              + _dot(qr_s[1, r, :].astype(BF16), x[128:256, :].astype(BF16)) for r, x in zip(rows, sb)]
        v4 = [v4_s[r, :].astype(BF16) for r in rows]
        kv = [jnp.concatenate([_dot_tn(kr_s[0, r, :].astype(BF16), v), _dot_tn(kr_s[1, r, :].astype(BF16), v)],
                              axis=0) for r, v in zip(rows, v4)]
        m = [jnp.maximum(jnp.max(x, axis=-1, keepdims=True), sk) for x in s]
        e = [jnp.exp(x - y) for x, y in zip(s, m)]
        den = [jnp.sum(x, axis=-1, keepdims=True) + jnp.exp(sk - y) for x, y in zip(e, m)]
        o = [_dot(x.astype(BF16), nv_ref[b].astype(BF16)) / d for x, b, d in zip(e, seqs, den)]
        for q, (b, r) in enumerate(zip(seqs, rows)):
            o2_s[r, :] = o2[q]
            ns_ref[b] = gcol * sb[q] + kv[q]
            o8[r, :] = o[q]
        return carry

    lax.fori_loop(0, gsz // lanes, per_group, 0)

    for sl in range(4):
        h = sl // 2
        ev = o8[(2 * sl) * gsz:(2 * sl + 1) * gsz, :]
        od = o8[(2 * sl + 1) * gsz:(2 * sl + 2) * gsz, :]
        if h != 0:
            ev = pltpu.roll(ev, 64, 1)
        if h != 1:
            od = pltpu.roll(od, 64, 1)
        oc_ref[:, LRU_WIDTH + sl * 128:LRU_WIDTH + (sl + 1) * 128] = jnp.where(lo, ev, od).astype(BF16)

    prod = qc * kc
    p_hi = prod.astype(BF16)
    p_lo = (prod - p_hi.astype(F32)).astype(BF16)
    er = lax.broadcasted_iota(jnp.int32, (RET_HEADS * RET_DK, RET_HEADS * RET_DV), 0) // RET_DK
    ec = lax.broadcasted_iota(jnp.int32, (RET_HEADS * RET_DK, RET_HEADS * RET_DV), 1) // RET_DV
    expand = jnp.where(er == ec, 1.0, 0.0).astype(BF16)
    qk = _dot(p_hi, expand) + _dot(p_lo, expand)
    for r in range(RET_HEADS):
        cs = slice(r * RET_DV, (r + 1) * RET_DV)
        o = qk[:, cs] * pm_ref[rs,C_VR + r * RET_DV:C_VR + (r + 1) * RET_DV]
        o = o + o2_s[r * gsz:(r + 1) * gsz, :] * math.exp(LOG_G[r])
        oc = _group_norm_gate(o, gn_ref[:, cs], pm_ref[rs,C_GR + r * RET_DV:C_GR + (r + 1) * RET_DV])
        oc_ref[:, 2 * LRU_WIDTH + r * RET_DV:2 * LRU_WIDTH + (r + 1) * RET_DV] = oc.astype(BF16)


def _call_mix_sample(pm, ck, cv, conv, h0, sret, stacked, p, layer):
    nseq = pm.shape[0]
    depth = ck.shape[0]
    whole = pl.BlockSpec(memory_space=pl.ANY)
    gsz = G_SEQ

    def lspec(shape):
        nd = len(shape)
        return pl.BlockSpec((None,) + tuple(shape[1:]), lambda i: (layer,) + (0,) * (nd - 1))

    def seq2(width):
        return pl.BlockSpec((gsz, width), lambda i: (i, 0))

    def seq3(layered, d1, d2):
        if layered:
            return pl.BlockSpec((None, gsz, d1, d2), lambda i: (layer, i, 0, 0))
        return pl.BlockSpec((gsz, d1, d2), lambda i: (i, 0, 0))

    sdim = RET_HEADS * RET_DK
    return pl.pallas_call(
        _mixs_body,
        grid=(nseq // gsz,),
        in_specs=[seq2(N_MIX), seq3(True, WINDOW, 128), seq3(True, WINDOW, 128),
                  pl.BlockSpec((None, gsz, 3 * LRU_WIDTH), lambda i: (layer, i, 0)),
                  pl.BlockSpec((None, gsz, LRU_WIDTH), lambda i: (layer, i, 0)),
                  seq3(True, sdim, RET_DV),
                  lspec(p['conv_w'].shape), lspec(p['conv_b'].shape), lspec(p['lru_wa_bd'].shape),
                  lspec(p['lru_wx_bd'].shape), lspec(p['lru_b_a'].shape), lspec(p['lru_b_x'].shape),
                  lspec(p['lru_lambda'].shape), pl.BlockSpec((1, 128), lambda i: (0, 0)),
                  lspec(p['ret_norm'].shape), lspec(p['sinks16'].shape), whole, whole, whole],
        out_specs=[seq2(N_OC), seq3(True, WINDOW, 128), seq3(True, WINDOW, 128), seq2(3 * LRU_WIDTH),
                   seq2(LRU_WIDTH), seq3(True, sdim, RET_DV)],
        out_shape=[jax.ShapeDtypeStruct((nseq, N_OC), BF16),
                   jax.ShapeDtypeStruct((depth, nseq, WINDOW, 128), F32),
                   jax.ShapeDtypeStruct((depth, nseq, WINDOW, 128), F32),
                   jax.ShapeDtypeStruct((nseq, 3 * LRU_WIDTH), F32), jax.ShapeDtypeStruct((nseq, LRU_WIDTH), F32),
                   jax.ShapeDtypeStruct((depth, nseq, sdim, RET_DV), F32)],
        input_output_aliases={16: 1, 17: 2, 18: 5},
        scratch_shapes=[pltpu.VMEM((16 * gsz, 128), F32), pltpu.VMEM((16 * gsz, 128), F32),
                        pltpu.VMEM((2, 16 * gsz, 128), F32), pltpu.VMEM((2, 16 * gsz, 128), F32),
                        pltpu.VMEM((16 * gsz, 128), F32), pltpu.VMEM((16 * gsz, 128), F32)],
        compiler_params=pltpu.CompilerParams(dimension_semantics=("arbitrary",), vmem_limit_bytes=VMEM_LIMIT),
        name='mix_sample',
    )(pm, ck, cv, conv, h0, sret, p['conv_w'], p['conv_b'], p['lru_wa_bd'], p['lru_wx_bd'], p['lru_b_a'],
      p['lru_b_x'], p['lru_lambda'], p['rope_inv'], p['ret_norm'], p['sinks16'], *stacked)


def _block_diag(w):
    depth = w.shape[0]
    w = w.reshape(depth, 2, 4, LRU_BW, LRU_BW)
    eye = jnp.eye(4, dtype=w.dtype)
    return jnp.einsum('lsncd,nm->lsncmd', w, eye).reshape(depth, 2, 4 * LRU_BW, 4 * LRU_BW)


def kernel(x_prompt, x_sample, cache_swa_k, cache_swa_v, state_conv, state_lru, state_ret, meta_tokens, ffn1_norm,
           ffn1_w_gu, ffn1_w_down, mix_norm, w_in, conv_w, conv_b, lru_w_a, lru_b_a, lru_w_x, lru_b_x, lru_lambda,
           swa_sinks, ret_norm, w_branch_a, w_branch_b, w_branch_c, w_out, ffn2_norm, ffn2_w_gu, ffn2_w_down,
           final_norm):
    depth = w_in.shape[0]
    bsz, seq, _ = x_prompt.shape
    nseq = x_sample.shape[0]
    buf = cache_swa_k.shape[2]
    assert buf == WINDOW == BLK and x_sample.shape[1] == 1 and nseq % G_SEQ == 0
    t = seq + N_META
    pad = (-t) % BLK
    tp = t + pad
    assert (bsz * tp) % TM_DENSE == 0

    def row(v):
        return v.reshape(depth, 1, -1).astype(F32)

    def bf16(w):
        return w.astype(BF16)

    assert SWA_HEAD_DIM ** -0.5 == 0.125 and RET_DK ** -0.5 == 0.125
    cols = jnp.arange(N_MIX)
    col_scale = jnp.where(((cols >= C_QS) & (cols < C_KS)) | ((cols >= C_KR) & (cols < C_VR)), 0.125, 1.0)
    half = jnp.arange(128) % (RET_DK // 2)
    p = {
        'ffn1_norm': row(ffn1_norm), 'ffn1_w_gu': bf16(ffn1_w_gu), 'ffn1_w_down': bf16(ffn1_w_down),
        'mix_norm': row(mix_norm), 'w_mix': bf16(w_in[:, :, :N_MIX] * col_scale), 'w_gate': bf16(w_in[:, :, N_MIX:]),
        'conv_w': conv_w.astype(F32), 'conv_b': row(conv_b),
        'lru_wa_bd': _block_diag(lru_w_a).astype(BF16), 'lru_wx_bd': _block_diag(lru_w_x).astype(BF16),
        'lru_b_a': row(lru_b_a), 'lru_b_x': row(lru_b_x), 'lru_lambda': row(lru_lambda),
        'swa_sinks': swa_sinks.astype(F32),
        'sinks16': jnp.pad(jnp.broadcast_to(swa_sinks.astype(F32)[:, :, None], (depth, SWA_HEADS, 128)),
                           ((0, 0), (0, 16 - SWA_HEADS), (0, 0))),
        'ret_norm': row(ret_norm),
        'w_branch_a': bf16(w_branch_a), 'w_branch_b': bf16(w_branch_b),
        'w_branch_c': bf16(w_branch_c), 'w_out': bf16(w_out),
        'ffn2_norm': row(ffn2_norm), 'ffn2_w_gu': bf16(ffn2_w_gu), 'ffn2_w_down': bf16(ffn2_w_down),
        'final_norm': final_norm.reshape(1, D_MODEL).astype(F32),
        'rope_inv': (ROPE_BASE ** (-half.astype(F32) / (RET_DK // 2))).reshape(1, 128),
    }

    assert pad + N_META == BLK
    head = jnp.concatenate([jnp.zeros((pad, D_MODEL), F32), meta_tokens.astype(F32)], axis=0)
    xp = x_prompt.reshape(bsz * seq, D_MODEL)
    xs = x_sample.reshape(nseq, D_MODEL)
    ck = cache_swa_k.reshape(depth, nseq, buf, 128)
    cv = cache_swa_v.reshape(depth, nseq, buf, 128)
    conv = state_conv.reshape(depth, nseq, 3 * LRU_WIDTH)
    sret = state_ret.reshape(depth, nseq, RET_HEADS * RET_DK, RET_DV)
    cos, sin = _call_ropetab(p['rope_inv'], tp, pad)

    outs_p = [[] for _ in range(5)]
    outs_s = [[], []]
    stacked = (jnp.zeros(ck.shape, F32), jnp.zeros(cv.shape, F32), jnp.zeros(sret.shape, F32))
    for layer in range(depth):
        final = layer == depth - 1
        if layer == 0:
            x1, pm, pg, tails = _call_in(xp, p, layer, TM_DENSE, head=head, nblk=tp // BLK, seq=True)
        else:
            x1, pm, pg, tails = _call_in(xp, p, layer, TM_DENSE, seq=True)
        cl = tails[tp // BLK - 1::tp // BLK]
        xp, kl, vl, hl, sl = _call_mix_out(pm, cos, sin, x1, pg, p, layer, bsz, pad, final)
        for acc, o in zip(outs_p, (kl.reshape(bsz, buf, SWA_KV_HEADS, SWA_HEAD_DIM),
                                   vl.reshape(bsz, buf, SWA_KV_HEADS, SWA_HEAD_DIM),
                                   cl[:, 8 - (CONV_WIDTH - 1):, :], hl[:, 7, :],
                                   sl.reshape(bsz, RET_HEADS, RET_DK, RET_DV))):
            acc.append(o)

        x1, pm, pg = _call_in(xs, p, layer, nseq)
        oc, nk, nv, nc, nh, ns = _call_mix_sample(pm, ck, cv, conv, state_lru, sret, stacked, p, layer)
        stacked = (nk, nv, ns)
        xs = _call_out(x1, pg, oc, p, layer, nseq, final)
        outs_s[0].append(nc.reshape(nseq, CONV_WIDTH - 1, LRU_WIDTH))
        outs_s[1].append(nh)

    yp = xp.reshape(bsz, tp, D_MODEL)[:, pad + N_META:]
    ys = xs.reshape(nseq, 1, D_MODEL)
    nk, nv, ns = stacked
    return ((yp, ys) + tuple(jnp.stack(a) for a in outs_p)
            + (nk.reshape(depth, nseq, buf, SWA_KV_HEADS, SWA_HEAD_DIM),
               nv.reshape(depth, nseq, buf, SWA_KV_HEADS, SWA_HEAD_DIM),
               jnp.stack(outs_s[0]), jnp.stack(outs_s[1]),
               ns.reshape(depth, nseq, RET_HEADS, RET_DK, RET_DV)))
```

```python
import functools
import math

import jax
import jax.numpy as jnp
from jax import lax
from jax.experimental import pallas as pl
from jax.experimental.pallas import tpu as pltpu

F32 = jnp.float32
BF16 = jnp.bfloat16

D_MODEL = 1024
D_FF = 2048
N_META = 16
EPS = 1e-6
LRU_WIDTH = 512
LRU_BLOCKS = 8
LRU_BW = 64
CONV_WIDTH = 4
LRU_C = 8.0
SWA_HEAD_DIM = 64
SWA_HEADS = 8
SWA_KV_HEADS = 2
SWA_GROUP = 4
WINDOW = 128
RET_DK = 64
RET_DV = 128
RET_HEADS = 4
ROPE_BASE = 10000.0
GN_EPS = 1e-5
PAST_LEN = 8192

BLK = 128
N_MIX = 3328
N_GATE = 3 * D_MODEL
C_XA, C_YA, C_QS, C_KS, C_VS, C_QR, C_KR, C_VR, C_GR = 0, 512, 1024, 1536, 1664, 1792, 2048, 2304, 2816
C_BT = N_MIX
N_MIX_SEQ = N_MIX + LRU_WIDTH
N_OC = 3 * LRU_WIDTH
_MIX_CHUNKS = ((C_XA, C_YA, None), (C_YA, C_QS, jax.nn.gelu), (C_QS, C_KS, None), (C_KS, C_KR, None),
               (C_KR, C_KR + 512, None), (C_KR + 512, C_GR, None), (C_GR, N_MIX, jax.nn.silu))
LOG_G = tuple(math.log1p(-(2.0 ** (-5.0 - h))) for h in range(RET_HEADS))

TM_DENSE = 256
G_SEQ = 32
VMEM_LIMIT = 56 * 1024 * 1024


def _dot(a, b):
    return jnp.dot(a, b, preferred_element_type=F32)


def _dot_nt(a, b):
    return lax.dot_general(a, b, (((1,), (1,)), ((), ())), preferred_element_type=F32)


def _dot_tn(a, b):
    return lax.dot_general(a, b, (((0,), (0,)), ((), ())), preferred_element_type=F32)


def _rms(x, g):
    return x * lax.rsqrt(jnp.mean(x * x, axis=-1, keepdims=True) + EPS) * g


def _softplus(x):
    return jnp.maximum(x, 0.0) + jnp.log1p(jnp.exp(-jnp.abs(x)))


def _wt(ref, rows=None, cols=None):
    r = slice(None) if rows is None else slice(*rows)
    c = slice(None) if cols is None else slice(*cols)
    return ref[r, c]


def _drain(pieces):
    while True:
        try:
            next(pieces)
        except StopIteration as done:
            return done.value


FF_CHUNK = 256


def _ffn_pieces(u, wgu_ref, wd_ref):
    y = None
    act = None
    nchunk = D_FF // FF_CHUNK
    for c in range(nchunk + 1):
        if c < nchunk:
            lo, hi = c * FF_CHUNK, (c + 1) * FF_CHUNK
            gate = _dot(u, _wt(wgu_ref, cols=(lo, hi)))
            yield
            up = _dot(u, _wt(wgu_ref, cols=(D_FF + lo, D_FF + hi)))
            yield
        if c > 0:
            part = _dot(act, _wt(wd_ref, rows=((c - 1) * FF_CHUNK, c * FF_CHUNK)))
            y = part if y is None else y + part
            yield
        if c < nchunk:
            act = (jax.nn.silu(gate) * up).astype(BF16)
    return y


def _swiglu(u, wgu_ref, wd_ref):
    gu = _dot(u, _wt(wgu_ref))
    act = (jax.nn.silu(gu[:, :D_FF]) * gu[:, D_FF:]).astype(BF16)
    return _dot(act, _wt(wd_ref))


def _in_body(*refs, nblk=None, seq=False):
    if seq:
        refs, (cw_ref, cb_ref, wa_ref, wx_ref, ba_ref, bx_ref, lam_ref), (x1_ref, pm_ref, pg_ref, xt_ref, xtail) = (
            refs[:-12], refs[-12:-5], refs[-5:])
    else:
        refs, (x1_ref, pm_ref, pg_ref) = refs[:-3], refs[-3:]
    if nblk is None:
        x_ref, n1_ref, wgu_ref, wd_ref, n2_ref, wm_ref, wg_ref = refs
        x = x_ref[...]
    else:
        xa_ref, xb_ref, head_ref, n1_ref, wgu_ref, wd_ref, n2_ref, wm_ref, wg_ref = refs
        first = 2 * pl.program_id(0)
        x = jnp.concatenate([jnp.where(lax.rem(first + h, nblk) == 0, head_ref[...], ref[...])
                             for h, ref in enumerate((xa_ref, xb_ref))], axis=0)
    if seq:
        @pl.when(pl.program_id(0) == 0)
        def _():
            xtail[...] = jnp.zeros_like(xtail)

    x1 = x + 0.5 * _drain(_ffn_pieces(_rms(x, n1_ref[...]).astype(BF16), wgu_ref, wd_ref))
    x1_ref[...] = x1
    u2 = _rms(x1, n2_ref[...]).astype(BF16)
    jobs = ([(wm_ref, pm_ref, lo, hi, fn) for lo, hi, fn in _MIX_CHUNKS]
            + [(wg_ref, pg_ref, c, c + LRU_WIDTH, jax.nn.sigmoid) for c in range(0, N_GATE, LRU_WIDTH)])
    side = iter(())
    pending = None
    for job in jobs + [None]:
        res = None if job is None else _dot(u2, _wt(job[0], cols=(job[2], job[3])))
        if pending is not None:
            val, (_, o_ref, lo, hi, fn) = pending
            if seq and o_ref is pm_ref and lo == C_XA:
                side = _lru_front_pieces(val, xtail, cw_ref, cb_ref, wa_ref, wx_ref, ba_ref, bx_ref, lam_ref,
                                         pm_ref, xt_ref)
            else:
                o_ref[:, lo:hi] = val if fn is None else fn(val)
        next(side, None)
        pending = (res, job)
    _drain(side)


def _lru_front_pieces(xa, xtail, cw_ref, cb_ref, wa_ref, wx_ref, ba_ref, bx_ref, lam_ref, pm_ref, xt_ref):
    rows = xa.shape[0]
    row8 = lax.broadcasted_iota(jnp.int32, (8, 1), 0)
    hist = xtail[...]
    xc = cb_ref[...]
    for tap in range(CONV_WIDTH - 1):
        k = CONV_WIDTH - 1 - tap
        sh = pltpu.roll(xa, k, 0)
        top = jnp.where(row8 < k, pltpu.roll(hist, k, 0), sh[0:8, :])
        xc = xc + jnp.concatenate([top, sh[8:, :]], axis=0) * cw_ref[tap:tap + 1, :]
        yield
    xc = xc + xa * cw_ref[CONV_WIDTH - 1:CONV_WIDTH, :]
    xtail[...] = xa[rows - 8:rows, :]
    for h in range(rows // BLK):
        xt_ref[h] = xa[(h + 1) * BLK - 8:(h + 1) * BLK, :]
    yield
    a, bt = yield from _lru_gate_pieces(xc, wa_ref, wx_ref, ba_ref, bx_ref, lam_ref)
    pm_ref[:, C_XA:C_XA + LRU_WIDTH] = a
    pm_ref[:, C_BT:C_BT + LRU_WIDTH] = bt


def _merge_out(x1, pg, oc, wa_ref, wb_ref, wc_ref, wo_ref, n_ref, wgu_ref, wd_ref, fn_ref, final):
    g = pg
    merged = (g[:, :D_MODEL] * _dot(oc[:, :LRU_WIDTH], _wt(wa_ref))
              + g[:, D_MODEL:2 * D_MODEL] * _dot(oc[:, LRU_WIDTH:2 * LRU_WIDTH], _wt(wb_ref))
              + g[:, 2 * D_MODEL:] * _dot(oc[:, 2 * LRU_WIDTH:], _wt(wc_ref)))
    x2 = x1 + _dot(merged.astype(BF16), _wt(wo_ref))
    x3 = x2 + 0.5 * _swiglu(_rms(x2, n_ref[...]).astype(BF16), wgu_ref, wd_ref)
    if final:
        x3 = _rms(x3, fn_ref[...])
    return x3


def _merge_out_pieces(x1_ref, pg_ref, oc, wa_ref, wb_ref, wc_ref, wo_ref, n_ref, wgu_ref, wd_ref, fn_ref, out_ref,
                      final):
    half = D_MODEL // 2
    merged = [None, None]
    pending = None
    for b, w_ref in enumerate((wa_ref, wb_ref, wc_ref)):
        for c in range(2):
            nxt = (_dot(oc[:, b * LRU_WIDTH:(b + 1) * LRU_WIDTH], _wt(w_ref, cols=(c * half, (c + 1) * half))), c,
                   pg_ref[:, b * D_MODEL + c * half:b * D_MODEL + (c + 1) * half])
            if pending is not None:
                proj, pc, g = pending
                merged[pc] = g * proj if merged[pc] is None else merged[pc] + g * proj
            pending = nxt
            yield
    proj, pc, g = pending
    merged[pc] = merged[pc] + g * proj
    merged = jnp.concatenate(merged, axis=1).astype(BF16)
    x2 = []
    pending = None
    for c in range(0, D_MODEL, FF_CHUNK):
        nxt = (_dot(merged, _wt(wo_ref, cols=(c, c + FF_CHUNK))), c)
        if pending is not None:
            x2.append(x1_ref[:, pending[1]:pending[1] + FF_CHUNK] + pending[0])
        pending = nxt
        yield
    x2.append(x1_ref[:, pending[1]:pending[1] + FF_CHUNK] + pending[0])
    x2 = jnp.concatenate(x2, axis=1)
    u = _rms(x2, n_ref[...]).astype(BF16)
    yield
    y = yield from _ffn_pieces(u, wgu_ref, wd_ref)
    x3 = x2 + 0.5 * y
    out_ref[...] = _rms(x3, fn_ref[...]) if final else x3
    yield


def _alternate(first, second, ratio):
    live = [True, True]
    while any(live):
        for idx, (gen, count) in enumerate(((first, 1), (second, ratio))):
            for _ in range(count):
                if live[idx] and next(gen, StopIteration) is StopIteration:
                    live[idx] = False


def _out_body(x1_ref, pg_ref, oc_ref, wa_ref, wb_ref, wc_ref, wo_ref, n_ref, wgu_ref, wd_ref, fn_ref, out_ref,
              *, final):
    out_ref[...] = _merge_out(x1_ref[...], pg_ref[...], oc_ref[...], wa_ref, wb_ref, wc_ref, wo_ref, n_ref,
                              wgu_ref, wd_ref, fn_ref, final)


def _layer_spec(shape, layer):
    nd = len(shape)
    return pl.BlockSpec((None,) + tuple(shape[1:]), lambda *_: (layer,) + (0,) * (nd - 1),
                        pipeline_mode=pl.Buffered(1))


def _row_spec(tm, width):
    return pl.BlockSpec((tm, width), lambda i: (i, 0))


def _dense_params():
    return pltpu.CompilerParams(dimension_semantics=("arbitrary",), vmem_limit_bytes=VMEM_LIMIT)


def _prompt_block(f, nblk):
    return f - f // nblk - 1


_LRU_PARAMS = ('conv_w', 'conv_b', 'lru_wa_bd', 'lru_wx_bd', 'lru_b_a', 'lru_b_x', 'lru_lambda')


def _call_in(x, p, layer, tm, head=None, nblk=None, seq=False):
    if head is None:
        n = x.shape[0]
        x_specs, x_args = [_row_spec(tm, D_MODEL)], (x,)
    else:
        assert tm == 2 * BLK
        n = (x.shape[0] // BLK + x.shape[0] // BLK // (nblk - 1)) * BLK
        last = x.shape[0] // BLK - 1

        def half(h):
            return pl.BlockSpec((BLK, D_MODEL),
                                lambda i: (jnp.clip(_prompt_block(2 * i + h, nblk), 0, last), 0))

        x_specs = [half(0), half(1), pl.BlockSpec((BLK, D_MODEL), lambda i: (0, 0))]
        x_args = (x, x, head)
    weights = ('ffn1_norm', 'ffn1_w_gu', 'ffn1_w_down', 'mix_norm', 'w_mix', 'w_gate') + (_LRU_PARAMS if seq else ())
    width = N_MIX_SEQ if seq else N_MIX
    out_specs = [_row_spec(tm, D_MODEL), _row_spec(tm, width), _row_spec(tm, N_GATE)]
    out_shape = [jax.ShapeDtypeStruct((n, D_MODEL), F32), jax.ShapeDtypeStruct((n, width), F32),
                 jax.ShapeDtypeStruct((n, N_GATE), F32)]
    if seq:
        out_specs.append(pl.BlockSpec((tm // BLK, 8, LRU_WIDTH), lambda i: (i, 0, 0)))
        out_shape.append(jax.ShapeDtypeStruct((n // BLK, 8, LRU_WIDTH), F32))
    return pl.pallas_call(
        functools.partial(_in_body, nblk=None if head is None else nblk, seq=seq),
        grid=(n // tm,),
        in_specs=x_specs + [_layer_spec(p[k].shape, layer) for k in weights],
        out_specs=out_specs,
        out_shape=out_shape,
        scratch_shapes=[pltpu.VMEM((8, LRU_WIDTH), F32)] if seq else [],
        compiler_params=_dense_params(),
        name='layer_in',
    )(*x_args, *[p[k] for k in weights])


_OUT_WEIGHTS = ('w_branch_a', 'w_branch_b', 'w_branch_c', 'w_out', 'ffn2_norm', 'ffn2_w_gu', 'ffn2_w_down')


def _call_out(x1, pg, oc, p, layer, tm, final):
    n = x1.shape[0]
    return pl.pallas_call(
        functools.partial(_out_body, final=final),
        grid=(n // tm,),
        in_specs=[_row_spec(tm, D_MODEL), _row_spec(tm, N_GATE), _row_spec(tm, N_OC)]
        + [_layer_spec(p[k].shape, layer) for k in _OUT_WEIGHTS]
        + [pl.BlockSpec((1, D_MODEL), lambda i: (0, 0))],
        out_specs=_row_spec(tm, D_MODEL),
        out_shape=jax.ShapeDtypeStruct((n, D_MODEL), F32),
        compiler_params=_dense_params(),
        name='layer_out',
    )(x1, pg, oc, *[p[k] for k in _OUT_WEIGHTS], p['final_norm'])


def _lru_gate_pieces(xc, wa_ref, wx_ref, ba_ref, bx_ref, lam_ref):
    xcb = xc.astype(BF16)
    half = LRU_WIDTH // 2
    rpre = jnp.concatenate([_dot(xcb[:, :half], wa_ref[0]), _dot(xcb[:, half:], wa_ref[1])], axis=1)
    yield
    ipre = jnp.concatenate([_dot(xcb[:, :half], wx_ref[0]), _dot(xcb[:, half:], wx_ref[1])], axis=1)
    yield
    r = jax.nn.sigmoid(rpre + ba_ref[...])
    yield
    i = jax.nn.sigmoid(ipre + bx_ref[...])
    yield
    log_a = -LRU_C * r * _softplus(-lam_ref[...])
    a = jnp.exp(log_a)
    yield
    z = -jnp.tanh(log_a) * (a * a + 1.0)
    return a, jnp.where(z > 0.0, z * lax.rsqrt(z), 0.0) * (i * xc)


def _lru_gates(*args):
    return _drain(_lru_gate_pieces(*args))


def _rope_operands(cos, sin):
    lane = lax.broadcasted_iota(jnp.int32, (1, 4 * RET_DK), 1)
    first_half = (lane & (RET_DK - 1)) < RET_DK // 2
    cos2 = jnp.concatenate([cos, cos], axis=1)
    sin2 = jnp.concatenate([sin, sin], axis=1)
    return cos2, jnp.where(first_half, -sin2, sin2), first_half


def _rope(x, cos2, sin_signed, first_half):
    swapped = jnp.where(first_half, pltpu.roll(x, 4 * RET_DK - RET_DK // 2, 1), pltpu.roll(x, RET_DK // 2, 1))
    return x * cos2 + swapped * sin_signed


def _group_norm_gate(o, gain, gate):
    mu = jnp.mean(o, axis=-1, keepdims=True)
    d = o - mu
    var = jnp.mean(d * d, axis=-1, keepdims=True)
    return d * lax.rsqrt(var + GN_EPS) * gain * gate


def _ropetab_body(inv_ref, cos_ref, sin_ref, *, rows_per_step, pad):
    rows = lax.broadcasted_iota(jnp.int32, (rows_per_step, 1), 0)
    pos = (pl.program_id(0) * rows_per_step + rows - pad).astype(F32)
    ang = pos * inv_ref[...]
    cos_ref[...] = jnp.cos(ang)
    sin_ref[...] = jnp.sin(ang)


def _call_ropetab(inv, tp, pad):
    nblk = tp // BLK
    rps = BLK * max(d for d in (8, 5, 4, 2, 1) if nblk % d == 0)
    spec = pl.BlockSpec((rps, 128), lambda i: (i, 0))
    return pl.pallas_call(
        functools.partial(_ropetab_body, rows_per_step=rps, pad=pad),
        grid=(tp // rps,),
        in_specs=[pl.BlockSpec((1, 128), lambda i: (0, 0))],
        out_specs=[spec, spec],
        out_shape=[jax.ShapeDtypeStruct((tp, 128), F32), jax.ShapeDtypeStruct((tp, 128), F32)],
        name='rope_tables',
    )(inv)


def _block_masks(j, pad):
    lo = lax.broadcasted_iota(jnp.int32, (1, 128), 1) < 64
    rows = lax.broadcasted_iota(jnp.int32, (BLK, 1), 0)
    return lo, (j * BLK + rows) >= pad


def _mix_lru(pm_ref, ocs, hcar, a_s, b_s, h_s, *, pad, j, cur, r0):
    rs = slice(r0, r0 + BLK)
    fresh = j == 0
    _, valid = _block_masks(j, pad)
    a = pm_ref[rs,C_XA:C_XA + LRU_WIDTH]
    bt = jnp.where(valid, pm_ref[rs,C_BT:C_BT + LRU_WIDTH], 0.0)
    ngrp = BLK // 8
    for c in range(LRU_WIDTH // 128):
        a_s[c] = a[:, c * 128:(c + 1) * 128]
        b_s[c] = bt[:, c * 128:(c + 1) * 128]
    yield
    h_in = jnp.where(fresh, 0.0, hcar[7:8, :])
    for c in range(LRU_WIDTH // 128):
        prods, sums = [], []
        for r in range(8):
            ar = a_s[c, pl.ds(r, ngrp, stride=8), :]
            br = b_s[c, pl.ds(r, ngrp, stride=8), :]
            prods.append(ar if r == 0 else ar * prods[-1])
            sums.append(br if r == 0 else ar * sums[-1] + br)
        carry = h_in[:, c * 128:(c + 1) * 128]
        carries = []
        for g in range(ngrp):
            carries.append(carry)
            carry = prods[7][g:g + 1, :] * carry + sums[7][g:g + 1, :]
        carries = jnp.concatenate(carries, axis=0)
        for r in range(8):
            h_s[c, pl.ds(r, ngrp, stride=8), :] = prods[r] * carries + sums[r]
        yield
    h = jnp.concatenate([h_s[c] for c in range(LRU_WIDTH // 128)], axis=1)
    hcar[...] = h[BLK - 8:BLK, :]
    o_a = h * pm_ref[rs,C_YA:C_YA + LRU_WIDTH]
    ocs[cur, rs,0:LRU_WIDTH] = o_a.astype(BF16)
    yield


def _mix_swa(sinks_ref, pm_ref, ocs, kprev, vprev, *, layer, pad, j, cur, r0):
    rs = slice(r0, r0 + BLK)
    lo, _ = _block_masks(j, pad)
    k = pm_ref[rs,C_KS:C_KS + 128]
    v = pm_ref[rs,C_VS:C_VS + 128]
    k_sw = pltpu.roll(k, 64, 1)
    v_sw = pltpu.roll(v, 64, 1)
    kdup = (jnp.where(lo, k, k_sw).astype(BF16), jnp.where(lo, k_sw, k).astype(BF16))
    vdup = (jnp.where(lo, v, v_sw).astype(BF16), jnp.where(lo, v_sw, v).astype(BF16))
    row4 = lax.broadcasted_iota(jnp.int32, (SWA_GROUP * BLK, 1), 0)
    t4 = row4 & (BLK - 1)
    col = lax.broadcasted_iota(jnp.int32, (1, 2 * BLK), 1)
    ok = ((j - 1) * BLK + col >= pad) & (col > t4) & (col <= t4 + BLK)
    for h in range(SWA_KV_HEADS):
        parts = []
        for g in range(SWA_GROUP):
            head = SWA_GROUP * h + g
            slab = pm_ref[rs,C_QS + (head // 2) * 128:C_QS + (head // 2 + 1) * 128]
            parts.append(jnp.where(lo if head % 2 == 0 else jnp.logical_not(lo), slab, 0.0))
        qst = jnp.concatenate(parts, axis=0).astype(BF16)
        kcat = jnp.concatenate([kprev[h], kdup[h]], axis=0)
        vcat = jnp.concatenate([vprev[h], vdup[h]], axis=0)
        yield
        sc = jnp.where(ok, _dot_nt(qst, kcat), -jnp.inf)
        yield
        sk = jnp.full((SWA_GROUP * BLK, 1), sinks_ref[layer, SWA_GROUP * h], F32)
        for g in range(1, SWA_GROUP):
            sk = jnp.where(row4 >= g * BLK, sinks_ref[layer, SWA_GROUP * h + g], sk)
        m = jnp.maximum(jnp.max(sc, axis=-1, keepdims=True), sk)
        yield
        e = jnp.exp(sc - m)
        yield
        den = jnp.sum(e, axis=-1, keepdims=True) + jnp.exp(sk - m)
        yield
        o = _dot(e.astype(BF16), vcat) / den
        yield
        for sl in range(2):
            ge = 2 * sl
            slab = jnp.where(lo, o[ge * BLK:(ge + 1) * BLK, :], o[(ge + 1) * BLK:(ge + 2) * BLK, :])
            c0 = LRU_WIDTH + (2 * h + sl) * 128
            ocs[cur, rs,c0:c0 + 128] = slab.astype(BF16)
        kprev[h] = kdup[h]
        vprev[h] = vdup[h]
        yield


def _mix_ret(pm_ref, cos_ref, sin_ref, gn_ref, ocs, state, dec_t, cross_t, kdec_t, *, pad, j, cur, r0):
    rs = slice(r0, r0 + BLK)
    fresh = j == 0
    lo, valid = _block_masks(j, pad)
    cos2, sin_signed, first_half = _rope_operands(cos_ref[...], sin_ref[...])
    qc = _rope(pm_ref[rs,C_QR:C_QR + 256], cos2, sin_signed, first_half)
    yield
    kc = _rope(pm_ref[rs,C_KR:C_KR + 256], cos2, sin_signed, first_half)
    kc = jnp.where(valid, kc, 0.0)
    yield
    st = [jnp.where(fresh, 0.0, state[sl * 128:(sl + 1) * 128, :]) for sl in range(2)]
    yield
    upd = [None, None]
    for h in range(RET_HEADS):
        sl = h // 2
        half = lo if h % 2 == 0 else jnp.logical_not(lo)
        qm = jnp.where(half, qc[:, sl * 128:(sl + 1) * 128], 0.0).astype(BF16)
        kslab = kc[:, sl * 128:(sl + 1) * 128]
        vh = jnp.where(valid, pm_ref[rs,C_VR + h * RET_DV:C_VR + (h + 1) * RET_DV], 0.0).astype(BF16)
        sc = (_dot_nt(qm, kslab.astype(BF16)) * dec_t[h]).astype(BF16)
        yield
        o = _dot(sc, vh)
        o = o + _dot(qm, st[sl].astype(BF16)) * cross_t[h]
        yield
        c0 = 2 * LRU_WIDTH + h * RET_DV
        oc = _group_norm_gate(o, gn_ref[:, h * RET_DV:(h + 1) * RET_DV],
                              pm_ref[rs,C_GR + h * RET_DV:C_GR + (h + 1) * RET_DV])
        ocs[cur, rs,c0:c0 + RET_DV] = oc.astype(BF16)
        yield
        km = jnp.where(half, kslab * kdec_t[sl], 0.0).astype(BF16)
        u = _dot_tn(km, vh)
        upd[sl] = u if upd[sl] is None else upd[sl] + u
        yield
    srow = lax.broadcasted_iota(jnp.int32, (128, 1), 0)
    for sl in range(2):
        gcol = jnp.where(srow < RET_DK, math.exp(BLK * LOG_G[2 * sl]), math.exp(BLK * LOG_G[2 * sl + 1]))
        state[sl * 128:(sl + 1) * 128, :] = gcol * st[sl] + upd[sl]


def _mixout_body(sinks_ref, pm_ref, cos0_ref, sin0_ref, cos1_ref, sin1_ref, x1_ref, pg_ref, gn_ref,
                 wba_ref, wbb_ref, wbc_ref, wo_ref, n2_ref, wgu_ref, wd_ref, fn_ref,
                 out_ref, klast_ref, vlast_ref, hlast_ref, sret_ref,
                 ocs, hcar, kprev, vprev, state, dec_t, cross_t, kdec_t, snap_h, snap_s, a_s, b_s, h_s,
                 *, layer, pad, nblk, final):
    s = pl.program_id(0)

    @pl.when(s == 0)
    def _():
        lo = lax.broadcasted_iota(jnp.int32, (1, 128), 1) < 64
        ti = lax.broadcasted_iota(jnp.int32, (BLK, 1), 0).astype(F32)
        tj = lax.broadcasted_iota(jnp.int32, (1, BLK), 1).astype(F32)
        diff = ti - tj
        for h in range(RET_HEADS):
            dec_t[h] = jnp.exp(jnp.where(diff >= 0, diff * LOG_G[h], -jnp.inf))
            cross_t[h] = jnp.broadcast_to(jnp.exp((ti + 1.0) * LOG_G[h]), (BLK, 128))
        for sl in range(2):
            kdec_t[sl] = jnp.where(lo, jnp.exp((BLK - 1.0 - ti) * LOG_G[2 * sl]),
                                   jnp.exp((BLK - 1.0 - ti) * LOG_G[2 * sl + 1]))
        for ref in (ocs, hcar, kprev, vprev, state):
            ref[...] = jnp.zeros_like(ref)

    cur = lax.rem(s, 2)
    j0 = lax.rem(2 * s, nblk)
    j1 = lax.rem(2 * s + 1, nblk)
    dense = _merge_out_pieces(x1_ref, pg_ref, ocs[lax.rem(s + 1, 2)], wba_ref, wbb_ref, wbc_ref, wo_ref, n2_ref,
                              wgu_ref, wd_ref, fn_ref, out_ref, final)

    def lru(j, r0):
        return _mix_lru(pm_ref, ocs, hcar, a_s, b_s, h_s, pad=pad, j=j, cur=cur, r0=r0)

    def swa(j, r0):
        return _mix_swa(sinks_ref, pm_ref, ocs, kprev, vprev, layer=layer, pad=pad, j=j, cur=cur, r0=r0)

    def ret(j, r0, cos_ref, sin_ref):
        return _mix_ret(pm_ref, cos_ref, sin_ref, gn_ref, ocs, state, dec_t, cross_t, kdec_t,
                        pad=pad, j=j, cur=cur, r0=r0)

    def mixers():
        yield from lru(j0, 0)
        yield from swa(j0, 0)
        yield from ret(j0, 0, cos0_ref, sin0_ref)
        snap_h[...] = hcar[...]
        snap_s[...] = state[...]
        yield from lru(j1, BLK)
        yield from swa(j1, BLK)
        yield from ret(j1, BLK, cos1_ref, sin1_ref)

    _alternate(dense, mixers(), 3)

    def write_state(r0, hl, st):
        klast_ref[0] = pm_ref[r0:r0 + BLK, C_KS:C_KS + 128]
        vlast_ref[0] = pm_ref[r0:r0 + BLK, C_VS:C_VS + 128]
        hlast_ref[0] = hl[...]
        sret_ref[0] = st[...]

    @pl.when(j0 == nblk - 1)
    def _():
        write_state(0, snap_h, snap_s)

    @pl.when(j1 == nblk - 1)
    def _():
        write_state(BLK, hcar, state)


def _call_mix_out(pm, cos, sin, x1, pg, p, layer, bsz, pad, final):
    n = pm.shape[0]
    nb = n // BLK
    nblk = nb // bsz
    assert nb % 2 == 0 and nblk >= 2
    steps = nb // 2
    rows = 2 * BLK

    def lspec(shape):
        nd = len(shape)
        return pl.BlockSpec((None,) + tuple(shape[1:]), lambda s: (layer,) + (0,) * (nd - 1))

    def cur(width):
        return pl.BlockSpec((rows, width), lambda s: (jnp.minimum(s, steps - 1), 0))

    def prev(width):
        return pl.BlockSpec((rows, width), lambda s: (jnp.maximum(s - 1, 0), 0))

    def last(shape):
        return pl.BlockSpec((1,) + shape,
                            lambda s: (jnp.minimum(2 * s, nb - 1) // nblk,) + (0,) * len(shape))

    def tab(half):
        return pl.BlockSpec((BLK, 128), lambda s: (lax.rem(2 * s + half, nblk), 0))

    return pl.pallas_call(
        functools.partial(_mixout_body, layer=layer, pad=pad, nblk=nblk, final=final),
        grid=(steps + 1,),
        in_specs=[pl.BlockSpec(memory_space=pltpu.SMEM), cur(N_MIX_SEQ), tab(0), tab(0), tab(1), tab(1),
                  prev(D_MODEL), prev(N_GATE), lspec(p['ret_norm'].shape)]
        + [_layer_spec(p[k].shape, layer) for k in _OUT_WEIGHTS]
        + [pl.BlockSpec((1, D_MODEL), lambda s: (0, 0))],
        out_specs=[prev(D_MODEL), last((BLK, 128)), last((BLK, 128)), last((8, LRU_WIDTH)),
                   last((RET_HEADS * RET_DK, RET_DV))],
        out_shape=[jax.ShapeDtypeStruct((n, D_MODEL), F32),
                   jax.ShapeDtypeStruct((bsz, BLK, 128), F32), jax.ShapeDtypeStruct((bsz, BLK, 128), F32),
                   jax.ShapeDtypeStruct((bsz, 8, LRU_WIDTH), F32),
                   jax.ShapeDtypeStruct((bsz, RET_HEADS * RET_DK, RET_DV), F32)],
        scratch_shapes=[pltpu.VMEM((2, rows, N_OC), BF16), pltpu.VMEM((8, LRU_WIDTH), F32),
                        pltpu.VMEM((SWA_KV_HEADS, BLK, 128), BF16), pltpu.VMEM((SWA_KV_HEADS, BLK, 128), BF16),
                        pltpu.VMEM((RET_HEADS * RET_DK, RET_DV), F32),
                        pltpu.VMEM((RET_HEADS, BLK, BLK), F32), pltpu.VMEM((RET_HEADS, BLK, 128), F32),
                        pltpu.VMEM((2, BLK, 128), F32),
                        pltpu.VMEM((8, LRU_WIDTH), F32), pltpu.VMEM((RET_HEADS * RET_DK, RET_DV), F32)]
        + [pltpu.VMEM((LRU_WIDTH // 128, BLK, 128), F32)] * 3,
        compiler_params=pltpu.CompilerParams(dimension_semantics=("arbitrary",), vmem_limit_bytes=VMEM_LIMIT),
        name='mix_out',
    )(p['swa_sinks'], pm, cos, sin, cos, sin, x1, pg, p['ret_norm'], *[p[k] for k in _OUT_WEIGHTS],
      p['final_norm'])


def _mixs_body(pm_ref, ck_ref, cv_ref, conv_ref, h0_ref, s_ref, cw_ref, cb_ref, wa_ref, wx_ref, ba_ref, bx_ref,
               lam_ref, inv_ref, gn_ref, sk_ref, nk_all, nv_all, ns_all,
               oc_ref, nk_ref, nv_ref, nconv_ref, nh_ref, ns_ref,
               qb, o8, qr_s, kr_s, v4_s, o2_s):
    gsz = G_SEQ
    nrow = 16
    rs = slice(None)

    @pl.when(pl.program_id(0) == 0)
    def _():
        qb[...] = jnp.zeros_like(qb)
        qr_s[...] = jnp.zeros_like(qr_s)
        kr_s[...] = jnp.zeros_like(kr_s)
        v4_s[...] = jnp.zeros_like(v4_s)

    lane = lax.broadcasted_iota(jnp.int32, (1, 128), 1)
    lo = lane < 64

    xa = pm_ref[rs,C_XA:C_XA + LRU_WIDTH]
    h1 = conv_ref[:, LRU_WIDTH:2 * LRU_WIDTH]
    h2 = conv_ref[:, 2 * LRU_WIDTH:3 * LRU_WIDTH]
    xc = cb_ref[...] + conv_ref[:, 0:LRU_WIDTH] * cw_ref[0:1, :]
    xc = xc + h1 * cw_ref[1:2, :]
    xc = xc + h2 * cw_ref[2:3, :]
    xc = xc + xa * cw_ref[3:4, :]
    nconv_ref[:, 0:LRU_WIDTH] = h1
    nconv_ref[:, LRU_WIDTH:2 * LRU_WIDTH] = h2
    nconv_ref[:, 2 * LRU_WIDTH:3 * LRU_WIDTH] = xa
    a, bt = _lru_gates(xc, wa_ref, wx_ref, ba_ref, bx_ref, lam_ref)
    hn = bt + a * h0_ref[...]
    nh_ref[...] = hn
    oc_ref[:, 0:LRU_WIDTH] = (hn * pm_ref[rs,C_YA:C_YA + LRU_WIDTH]).astype(BF16)

    nk_ref[:, 0:WINDOW - 1, :] = ck_ref[:, 1:WINDOW, :]
    nv_ref[:, 0:WINDOW - 1, :] = cv_ref[:, 1:WINDOW, :]
    for b in range(gsz):
        nk_ref[b, WINDOW - 1:WINDOW, :] = pm_ref[b:b + 1, C_KS:C_KS + 128]
        nv_ref[b, WINDOW - 1:WINDOW, :] = pm_ref[b:b + 1, C_VS:C_VS + 128]

    for r in range(SWA_HEADS):
        h = r // SWA_GROUP
        slab = pm_ref[rs,C_QS + (r // 2) * 128:C_QS + (r // 2 + 1) * 128]
        if r % 2 != h:
            slab = pltpu.roll(slab, 64, 1)
        qb[r * gsz:(r + 1) * gsz, :] = jnp.where(lo if h == 0 else jnp.logical_not(lo), slab, 0.0)

    ang = float(PAST_LEN) * inv_ref[...]
    cos2, sin_signed, first_half = _rope_operands(jnp.cos(ang), jnp.sin(ang))
    qc = _rope(pm_ref[rs,C_QR:C_QR + 256], cos2, sin_signed, first_half)
    kc = _rope(pm_ref[rs,C_KR:C_KR + 256], cos2, sin_signed, first_half)
    lane256 = lax.broadcasted_iota(jnp.int32, (1, RET_HEADS * RET_DK), 1)
    for r in range(RET_HEADS):
        hm = (lane256 >= r * RET_DK) & (lane256 < (r + 1) * RET_DK)
        qm = jnp.where(hm, qc, 0.0)
        km = jnp.where(hm, kc, 0.0)
        for c in range(2):
            qr_s[c, r * gsz:(r + 1) * gsz, :] = qm[:, c * 128:(c + 1) * 128]
            kr_s[c, r * gsz:(r + 1) * gsz, :] = km[:, c * 128:(c + 1) * 128]
        v4_s[r * gsz:(r + 1) * gsz, :] = pm_ref[rs,C_VR + r * RET_DV:C_VR + (r + 1) * RET_DV]

    srow = lax.broadcasted_iota(jnp.int32, (RET_HEADS * RET_DK, 1), 0)
    gcol = jnp.full((RET_HEADS * RET_DK, 1), math.exp(LOG_G[0]), F32)
    for r in range(1, RET_HEADS):
        gcol = jnp.where(srow >= r * RET_DK, math.exp(LOG_G[r]), gcol)
    sk = sk_ref[:, 0:1]

    lanes = 8

    def per_group(i, carry):
        seqs = [i * lanes + q for q in range(lanes)]
        rows = [pl.ds(b, nrow, stride=gsz) for b in seqs]
        s = [_dot_nt(qb[r, :].astype(BF16), nk_ref[b].astype(BF16)) for b, r in zip(seqs, rows)]
        sb = [s_ref[b] for b in seqs]
        o2 = [_dot(qr_s[0, r, :].astype(BF16), x[0:128, :].astype(BF16))
              + _dot(qr_s[1, r, :].astype(BF16), x[128:256, :].astype(BF16)) for r, x in zip(rows, sb)]
        v4 = [v4_s[r, :].astype(BF16) for r in rows]
        kv = [jnp.concatenate([_dot_tn(kr_s[0, r, :].astype(BF16), v), _dot_tn(kr_s[1, r, :].astype(BF16), v)],
                              axis=0) for r, v in zip(rows, v4)]
        m = [jnp.maximum(jnp.max(x, axis=-1, keepdims=True), sk) for x in s]
        e = [jnp.exp(x - y) for x, y in zip(s, m)]
        den = [jnp.sum(x, axis=-1, keepdims=True) + jnp.exp(sk - y) for x, y in zip(e, m)]
        o = [_dot(x.astype(BF16), nv_ref[b].astype(BF16)) / d for x, b, d in zip(e, seqs, den)]
        for q, (b, r) in enumerate(zip(seqs, rows)):
            o2_s[r, :] = o2[q]
            ns_ref[b] = gcol * sb[q] + kv[q]
            o8[r, :] = o[q]
        return carry

    lax.fori_loop(0, gsz // lanes, per_group, 0)

    for sl in range(4):
        h = sl // 2
        ev = o8[(2 * sl) * gsz:(2 * sl + 1) * gsz, :]
        od = o8[(2 * sl + 1) * gsz:(2 * sl + 2) * gsz, :]
        if h != 0:
            ev = pltpu.roll(ev, 64, 1)
        if h != 1:
            od = pltpu.roll(od, 64, 1)
        oc_ref[:, LRU_WIDTH + sl * 128:LRU_WIDTH + (sl + 1) * 128] = jnp.where(lo, ev, od).astype(BF16)

    prod = qc * kc
    p_hi = prod.astype(BF16)
    p_lo = (prod - p_hi.astype(F32)).astype(BF16)
    er = lax.broadcasted_iota(jnp.int32, (RET_HEADS * RET_DK, RET_HEADS * RET_DV), 0) // RET_DK
    ec = lax.broadcasted_iota(jnp.int32, (RET_HEADS * RET_DK, RET_HEADS * RET_DV), 1) // RET_DV
    expand = jnp.where(er == ec, 1.0, 0.0).astype(BF16)
    qk = _dot(p_hi, expand) + _dot(p_lo, expand)
    for r in range(RET_HEADS):
        cs = slice(r * RET_DV, (r + 1) * RET_DV)
        o = qk[:, cs] * pm_ref[rs,C_VR + r * RET_DV:C_VR + (r + 1) * RET_DV]
        o = o + o2_s[r * gsz:(r + 1) * gsz, :] * math.exp(LOG_G[r])
        oc = _group_norm_gate(o, gn_ref[:, cs], pm_ref[rs,C_GR + r * RET_DV:C_GR + (r + 1) * RET_DV])
        oc_ref[:, 2 * LRU_WIDTH + r * RET_DV:2 * LRU_WIDTH + (r + 1) * RET_DV] = oc.astype(BF16)


def _call_mix_sample(pm, ck, cv, conv, h0, sret, stacked, p, layer):
    nseq = pm.shape[0]
    depth = ck.shape[0]
    whole = pl.BlockSpec(memory_space=pl.ANY)
    gsz = G_SEQ

    def lspec(shape):
        nd = len(shape)
        return pl.BlockSpec((None,) + tuple(shape[1:]), lambda i: (layer,) + (0,) * (nd - 1))

    def seq2(width):
        return pl.BlockSpec((gsz, width), lambda i: (i, 0))

    def seq3(layered, d1, d2):
        if layered:
            return pl.BlockSpec((None, gsz, d1, d2), lambda i: (layer, i, 0, 0))
        return pl.BlockSpec((gsz, d1, d2), lambda i: (i, 0, 0))

    sdim = RET_HEADS * RET_DK
    return pl.pallas_call(
        _mixs_body,
        grid=(nseq // gsz,),
        in_specs=[seq2(N_MIX), seq3(True, WINDOW, 128), seq3(True, WINDOW, 128),
                  pl.BlockSpec((None, gsz, 3 * LRU_WIDTH), lambda i: (layer, i, 0)),
                  pl.BlockSpec((None, gsz, LRU_WIDTH), lambda i: (layer, i, 0)),
                  seq3(True, sdim, RET_DV),
                  lspec(p['conv_w'].shape), lspec(p['conv_b'].shape), lspec(p['lru_wa_bd'].shape),
                  lspec(p['lru_wx_bd'].shape), lspec(p['lru_b_a'].shape), lspec(p['lru_b_x'].shape),
                  lspec(p['lru_lambda'].shape), pl.BlockSpec((1, 128), lambda i: (0, 0)),
                  lspec(p['ret_norm'].shape), lspec(p['sinks16'].shape), whole, whole, whole],
        out_specs=[seq2(N_OC), seq3(True, WINDOW, 128), seq3(True, WINDOW, 128), seq2(3 * LRU_WIDTH),
                   seq2(LRU_WIDTH), seq3(True, sdim, RET_DV)],
        out_shape=[jax.ShapeDtypeStruct((nseq, N_OC), BF16),
                   jax.ShapeDtypeStruct((depth, nseq, WINDOW, 128), F32),
                   jax.ShapeDtypeStruct((depth, nseq, WINDOW, 128), F32),
                   jax.ShapeDtypeStruct((nseq, 3 * LRU_WIDTH), F32), jax.ShapeDtypeStruct((nseq, LRU_WIDTH), F32),
                   jax.ShapeDtypeStruct((depth, nseq, sdim, RET_DV), F32)],
        input_output_aliases={16: 1, 17: 2, 18: 5},
        scratch_shapes=[pltpu.VMEM((16 * gsz, 128), F32), pltpu.VMEM((16 * gsz, 128), F32),
                        pltpu.VMEM((2, 16 * gsz, 128), F32), pltpu.VMEM((2, 16 * gsz, 128), F32),
                        pltpu.VMEM((16 * gsz, 128), F32), pltpu.VMEM((16 * gsz, 128), F32)],
        compiler_params=pltpu.CompilerParams(dimension_semantics=("arbitrary",), vmem_limit_bytes=VMEM_LIMIT),
        name='mix_sample',
    )(pm, ck, cv, conv, h0, sret, p['conv_w'], p['conv_b'], p['lru_wa_bd'], p['lru_wx_bd'], p['lru_b_a'],
      p['lru_b_x'], p['lru_lambda'], p['rope_inv'], p['ret_norm'], p['sinks16'], *stacked)


def _block_diag(w):
    depth = w.shape[0]
    w = w.reshape(depth, 2, 4, LRU_BW, LRU_BW)
    eye = jnp.eye(4, dtype=w.dtype)
    return jnp.einsum('lsncd,nm->lsncmd', w, eye).reshape(depth, 2, 4 * LRU_BW, 4 * LRU_BW)


def kernel(x_prompt, x_sample, cache_swa_k, cache_swa_v, state_conv, state_lru, state_ret, meta_tokens, ffn1_norm,
           ffn1_w_gu, ffn1_w_down, mix_norm, w_in, conv_w, conv_b, lru_w_a, lru_b_a, lru_w_x, lru_b_x, lru_lambda,
           swa_sinks, ret_norm, w_branch_a, w_branch_b, w_branch_c, w_out, ffn2_norm, ffn2_w_gu, ffn2_w_down,
           final_norm):
    depth = w_in.shape[0]
    bsz, seq, _ = x_prompt.shape
    nseq = x_sample.shape[0]
    buf = cache_swa_k.shape[2]
    assert buf == WINDOW == BLK and x_sample.shape[1] == 1 and nseq % G_SEQ == 0
    t = seq + N_META
    pad = (-t) % BLK
    tp = t + pad
    assert (bsz * tp) % TM_DENSE == 0

    def row(v):
        return v.reshape(depth, 1, -1).astype(F32)

    def bf16(w):
        return w.astype(BF16)

    assert SWA_HEAD_DIM ** -0.5 == 0.125 and RET_DK ** -0.5 == 0.125
    cols = jnp.arange(N_MIX)
    col_scale = jnp.where(((cols >= C_QS) & (cols < C_KS)) | ((cols >= C_KR) & (cols < C_VR)), 0.125, 1.0)
    half = jnp.arange(128) % (RET_DK // 2)
    p = {
        'ffn1_norm': row(ffn1_norm), 'ffn1_w_gu': bf16(ffn1_w_gu), 'ffn1_w_down': bf16(ffn1_w_down),
        'mix_norm': row(mix_norm), 'w_mix': bf16(w_in[:, :, :N_MIX] * col_scale), 'w_gate': bf16(w_in[:, :, N_MIX:]),
        'conv_w': conv_w.astype(F32), 'conv_b': row(conv_b),
        'lru_wa_bd': _block_diag(lru_w_a).astype(BF16), 'lru_wx_bd': _block_diag(lru_w_x).astype(BF16),
        'lru_b_a': row(lru_b_a), 'lru_b_x': row(lru_b_x), 'lru_lambda': row(lru_lambda),
        'swa_sinks': swa_sinks.astype(F32),
        'sinks16': jnp.pad(jnp.broadcast_to(swa_sinks.astype(F32)[:, :, None], (depth, SWA_HEADS, 128)),
                           ((0, 0), (0, 16 - SWA_HEADS), (0, 0))),
        'ret_norm': row(ret_norm),
        'w_branch_a': bf16(w_branch_a), 'w_branch_b': bf16(w_branch_b),
        'w_branch_c': bf16(w_branch_c), 'w_out': bf16(w_out),
        'ffn2_norm': row(ffn2_norm), 'ffn2_w_gu': bf16(ffn2_w_gu), 'ffn2_w_down': bf16(ffn2_w_down),
        'final_norm': final_norm.reshape(1, D_MODEL).astype(F32),
        'rope_inv': (ROPE_BASE ** (-half.astype(F32) / (RET_DK // 2))).reshape(1, 128),
    }

    assert pad + N_META == BLK
    head = jnp.concatenate([jnp.zeros((pad, D_MODEL), F32), meta_tokens.astype(F32)], axis=0)
    xp = x_prompt.reshape(bsz * seq, D_MODEL)
    xs = x_sample.reshape(nseq, D_MODEL)
    ck = cache_swa_k.reshape(depth, nseq, buf, 128)
    cv = cache_swa_v.reshape(depth, nseq, buf, 128)
    conv = state_conv.reshape(depth, nseq, 3 * LRU_WIDTH)
    sret = state_ret.reshape(depth, nseq, RET_HEADS * RET_DK, RET_DV)
    cos, sin = _call_ropetab(p['rope_inv'], tp, pad)

    outs_p = [[] for _ in range(5)]
    outs_s = [[], []]
    stacked = (jnp.zeros(ck.shape, F32), jnp.zeros(cv.shape, F32), jnp.zeros(sret.shape, F32))
    for layer in range(depth):
        final = layer == depth - 1
        if layer == 0:
            x1, pm, pg, tails = _call_in(xp, p, layer, TM_DENSE, head=head, nblk=tp // BLK, seq=True)
        else:
            x1, pm, pg, tails = _call_in(xp, p, layer, TM_DENSE, seq=True)
        cl = tails[tp // BLK - 1::tp // BLK]
        xp, kl, vl, hl, sl = _call_mix_out(pm, cos, sin, x1, pg, p, layer, bsz, pad, final)
        for acc, o in zip(outs_p, (kl.reshape(bsz, buf, SWA_KV_HEADS, SWA_HEAD_DIM),
                                   vl.reshape(bsz, buf, SWA_KV_HEADS, SWA_HEAD_DIM),
                                   cl[:, 8 - (CONV_WIDTH - 1):, :], hl[:, 7, :],
                                   sl.reshape(bsz, RET_HEADS, RET_DK, RET_DV))):
            acc.append(o)

        x1, pm, pg = _call_in(xs, p, layer, nseq)
        oc, nk, nv, nc, nh, ns = _call_mix_sample(pm, ck, cv, conv, state_lru, sret, stacked, p, layer)
        stacked = (nk, nv, ns)
        xs = _call_out(x1, pg, oc, p, layer, nseq, final)
        outs_s[0].append(nc.reshape(nseq, CONV_WIDTH - 1, LRU_WIDTH))
        outs_s[1].append(nh)

    yp = xp.reshape(bsz, tp, D_MODEL)[:, pad + N_META:]
    ys = xs.reshape(nseq, 1, D_MODEL)
    nk, nv, ns = stacked
    return ((yp, ys) + tuple(jnp.stack(a) for a in outs_p)
            + (nk.reshape(depth, nseq, buf, SWA_KV_HEADS, SWA_HEAD_DIM),
               nv.reshape(depth, nseq, buf, SWA_KV_HEADS, SWA_HEAD_DIM),
               jnp.stack(outs_s[0]), jnp.stack(outs_s[1]),
               ns.reshape(depth, nseq, RET_HEADS, RET_DK, RET_DV)))
```
